```python
import jax, jax.numpy as jnp
from jax import lax
import numpy as np

D_MODEL = 1024
BATCH = 4
SEQ = 4096
DEPTH = 4

N_HEADS = 8
HEAD_DIM = 64
ATT_WIDTH = N_HEADS * HEAD_DIM
POOL_GROUPS = 4
POOL_WIDTH = D_MODEL - ATT_WIDTH
POOL_GROUP_DIM = POOL_WIDTH // POOL_GROUPS
POOL_WINDOWS = (2, 4, 8, 16)
MIX_WIDTH = ATT_WIDTH + POOL_WIDTH
IN_WIDTH = 3 * ATT_WIDTH + POOL_WIDTH
DILATED_BRANCHES = ((128, 1), (512, 4), (2048, 16))
ROT_DIM = HEAD_DIM // 4
ROPE_THETA = 500000.0
N_EXPERT_GROUPS = 4
EXPERTS_PER_GROUP = 4
N_EXPERTS = N_EXPERT_GROUPS * EXPERTS_PER_GROUP
TOP_K_IN_GROUP = 2
D_EXPERT = D_MODEL // 4
RMS_EPS = 1e-6
NEG_INF = -1e30

kernel_name = "hybrid_dilated_attn_pool_hiermoe"


def rms_norm(x, g):
    x32 = x.astype(jnp.float32)
    y = x32 * lax.rsqrt(jnp.mean(x32 * x32, axis=-1, keepdims=True) + RMS_EPS)
    return (y * g.astype(jnp.float32)).astype(x.dtype)


def rope_tables(seq):
    inv_freq = ROPE_THETA ** (-jnp.arange(0, ROT_DIM, 2, dtype=jnp.float32) / ROT_DIM)
    ang = jnp.arange(seq, dtype=jnp.float32)[:, None] * inv_freq[None, :]
    return jnp.cos(ang), jnp.sin(ang)


def apply_partial_rope(t, cos, sin):
    half = ROT_DIM // 2
    t1, t2, rest = t[..., :half], t[..., half:ROT_DIM], t[..., ROT_DIM:]
    c = cos[None, :, None, :]
    s = sin[None, :, None, :]
    return jnp.concatenate([t1 * c - t2 * s, t2 * c + t1 * s, rest], axis=-1)


def dilated_branch(q, k, v, window, dilation):
    B, S, H, Dh = q.shape
    span = window // dilation
    blk = span
    unit = dilation * blk
    P = -(-S // unit) * unit
    L = P // dilation
    nb = L // blk

    def split(t):
        t = jnp.pad(t, ((0, 0), (0, P - S), (0, 0), (0, 0)))
        t = t.reshape(B, L, dilation, H, Dh).transpose(0, 2, 3, 1, 4)
        return t.reshape(B, dilation, H, nb, blk, Dh)

    qb, kb, vb = split(q), split(k), split(v)
    pad_prev = ((0, 0), (0, 0), (0, 0), (1, 0), (0, 0), (0, 0))
    kcat = jnp.concatenate([jnp.pad(kb, pad_prev)[:, :, :, :-1], kb], axis=4)
    vcat = jnp.concatenate([jnp.pad(vb, pad_prev)[:, :, :, :-1], vb], axis=4)

    qi = jnp.arange(blk)[:, None]
    kj = jnp.arange(2 * blk)[None, :]
    dist = blk + qi - kj
    band = (dist >= 0) & (dist <= span)
    has_prev = (jnp.arange(nb) > 0)[:, None, None] | (kj >= blk)[None]
    mask = band[None] & has_prev

    s = jnp.einsum('brhnqc,brhnkc->brhnqk', qb, kcat)
    s = jnp.where(mask, s, NEG_INF)
    m = jnp.max(s, axis=-1, keepdims=True)
    p = jnp.exp(s - m)
    den = jnp.sum(p, axis=-1, keepdims=True)
    num = jnp.einsum('brhnqk,brhnkc->brhnqc', p, vcat)

    def merge(t):
        X = t.shape[-1]
        t = t.reshape(B, dilation, H, L, X).transpose(0, 3, 1, 2, 4)
        return t.reshape(B, P, H, X)[:, :S]

    return merge(num), merge(m), merge(den)


def dilated_attention(q, k, v):
    outs = [dilated_branch(q, k, v, w, d) for (w, d) in DILATED_BRANCHES]
    m_all = jnp.max(jnp.stack([o[1] for o in outs], axis=0), axis=0)
    num = sum(jnp.exp(o[1] - m_all) * o[0] for o in outs)
    den = sum(jnp.exp(o[1] - m_all) * o[2] for o in outs)
    return num / den


def pool_mixer(u, w_pool, pool_scale):
    B, S, _ = u.shape
    ug = u.astype(jnp.float32).reshape(B, S, POOL_GROUPS, POOL_GROUP_DIM)
    cs = jnp.cumsum(ug, axis=1)
    t1 = jnp.arange(1, S + 1, dtype=jnp.float32)
    outs = []
    for gi, w in enumerate(POOL_WINDOWS):
        c = cs[:, :, gi]
        lower = jnp.pad(c, ((0, 0), (w, 0), (0, 0)))[:, :S]
        cnt = jnp.minimum(t1, float(w))[None, :, None]
        outs.append((c - lower) / cnt - ug[:, :, gi])
    r = jnp.stack(outs, axis=2)
    y = jnp.einsum('bsgc,gce->bsge', r, w_pool.astype(jnp.float32)).reshape(B, S, POOL_WIDTH)
    return (y * pool_scale.astype(jnp.float32)).astype(u.dtype)


def hier_moe(h, w_group, b_group, w_router, b_router, w_gate, w_up, w_down):
    B, S, D = h.shape
    t = h.reshape(B * S, D)
    g_prob = jax.nn.softmax((t @ w_group + b_group).astype(jnp.float32), axis=-1)
    p_top, g_idx = lax.top_k(g_prob, 1)
    e_logit = (t @ w_router + b_router).astype(jnp.float32).reshape(-1, N_EXPERT_GROUPS, EXPERTS_PER_GROUP)
    g_onehot = jax.nn.one_hot(g_idx[:, 0], N_EXPERT_GROUPS, dtype=jnp.float32)
    e_sel = jnp.einsum('tg,tge->te', g_onehot, e_logit)
    v2, i2 = lax.top_k(e_sel, TOP_K_IN_GROUP)
    w2 = jax.nn.softmax(v2, axis=-1) * p_top
    eid = g_idx * EXPERTS_PER_GROUP + i2
    gates = jnp.sum(jax.nn.one_hot(eid, N_EXPERTS, dtype=jnp.float32) * w2[..., None], axis=1)
    hg = jnp.einsum('td,edf->tef', t, w_gate)
    hu = jnp.einsum('td,edf->tef', t, w_up)
    a = jax.nn.silu(hg) * hu * gates[..., None].astype(t.dtype)
    y = jnp.einsum('tef,efd->td', a, w_down)
    return y.reshape(B, S, D)


def setup_inputs(seed: int = 0) -> dict:
    key = jax.random.key(seed)
    ks = jax.random.split(key, 16)
    f32 = jnp.float32
    nrm = lambda k, shape: jax.random.normal(k, shape, dtype=f32)
    return {
        "x": nrm(ks[0], (BATCH, SEQ, D_MODEL)),
        "norm1_gain": 1.0 + 0.02 * nrm(ks[1], (DEPTH, D_MODEL)),
        "w_in": nrm(ks[2], (DEPTH, D_MODEL, IN_WIDTH)) * D_MODEL ** -0.5,
        "q_norm_gain": 1.0 + 0.02 * nrm(ks[3], (DEPTH, HEAD_DIM)),
        "k_norm_gain": 1.0 + 0.02 * nrm(ks[4], (DEPTH, HEAD_DIM)),
        "w_pool": nrm(ks[5], (DEPTH, POOL_GROUPS, POOL_GROUP_DIM, POOL_GROUP_DIM)) * POOL_GROUP_DIM ** -0.5,
        "pool_scale": 1.0 + 0.02 * nrm(ks[6], (DEPTH, POOL_WIDTH)),
        "w_out": nrm(ks[7], (DEPTH, MIX_WIDTH, D_MODEL)) * MIX_WIDTH ** -0.5,
        "norm2_gain": 1.0 + 0.02 * nrm(ks[8], (DEPTH, D_MODEL)),
        "w_group": nrm(ks[9], (DEPTH, D_MODEL, N_EXPERT_GROUPS)) * D_MODEL ** -0.5,
        "b_group": 0.01 * nrm(ks[10], (DEPTH, N_EXPERT_GROUPS)),
        "w_router": nrm(ks[11], (DEPTH, D_MODEL, N_EXPERTS)) * D_MODEL ** -0.5,
        "b_router": 0.01 * nrm(ks[12], (DEPTH, N_EXPERTS)),
        "w_gate": nrm(ks[13], (DEPTH, N_EXPERTS, D_MODEL, D_EXPERT)) * D_MODEL ** -0.5,
        "w_up": nrm(ks[14], (DEPTH, N_EXPERTS, D_MODEL, D_EXPERT)) * D_MODEL ** -0.5,
        "w_down": nrm(ks[15], (DEPTH, N_EXPERTS, D_EXPERT, D_MODEL)) * D_EXPERT ** -0.5,
    }


def reference(x, norm1_gain, w_in, q_norm_gain, k_norm_gain, w_pool, pool_scale, w_out,
              norm2_gain, w_group, b_group, w_router, b_router, w_gate, w_up, w_down):
    B, S, _ = x.shape
    cos, sin = rope_tables(S)
    scale = HEAD_DIM ** -0.5
    for l in range(DEPTH):
        h = rms_norm(x, norm1_gain[l])
        z = h @ w_in[l]
        q, k, v, u = jnp.split(z, [ATT_WIDTH, 2 * ATT_WIDTH, 3 * ATT_WIDTH], axis=-1)
        q = rms_norm(q.reshape(B, S, N_HEADS, HEAD_DIM), q_norm_gain[l]).astype(jnp.float32)
        k = rms_norm(k.reshape(B, S, N_HEADS, HEAD_DIM), k_norm_gain[l]).astype(jnp.float32)
        q = apply_partial_rope(q, cos, sin) * scale
        k = apply_partial_rope(k, cos, sin)
        v = v.reshape(B, S, N_HEADS, HEAD_DIM).astype(jnp.float32)
        att = dilated_attention(q, k, v).reshape(B, S, ATT_WIDTH).astype(x.dtype)
        pool = pool_mixer(u, w_pool[l], pool_scale[l])
        x = x + jnp.concatenate([att, pool], axis=-1) @ w_out[l]
        x = x + hier_moe(rms_norm(x, norm2_gain[l]), w_group[l], b_group[l], w_router[l], b_router[l],
                         w_gate[l], w_up[l], w_down[l])
    return x
```

```python
import functools

import jax
import jax.numpy as jnp
from jax import lax
from jax.experimental import pallas as pl
from jax.experimental.pallas import tpu as pltpu

D_MODEL = 1024
N_HEADS = 8
HEAD_DIM = 64
ATT_WIDTH = N_HEADS * HEAD_DIM
POOL_GROUPS = 4
POOL_GROUP_DIM = 128
POOL_WIDTH = POOL_GROUPS * POOL_GROUP_DIM
POOL_WINDOWS = (2, 4, 8, 16)
IN_WIDTH = 3 * ATT_WIDTH + POOL_WIDTH
ROT_DIM = 16
ROPE_THETA = 500000.0
N_EXPERT_GROUPS = 4
EXPERTS_PER_GROUP = 4
D_EXPERT = 256
RMS_EPS = 1e-6
NEG_INF = -1e30

LANES = 128
SUBLANES = 8
RES = 16
ATT_BLK = 128
PAIRS_PER_GROUP = 6
N_BUCKETS = N_EXPERT_GROUPS * PAIRS_PER_GROUP
ROW_TILE = 512
MOE_TILE = 256
META = LANES
ROW_EXT = D_MODEL + META
VMEM_LIMIT = 56 * 1024 * 1024

F32 = jnp.float32
BF16 = jnp.bfloat16
I32 = jnp.int32


def _dot(a, b):
    return jnp.dot(a, b, preferred_element_type=F32)


def _dot_nt(a, b):
    return lax.dot_general(a, b, (((1,), (1,)), ((), ())), preferred_element_type=F32)


def _params(*sem):
    return pltpu.CompilerParams(dimension_semantics=sem, vmem_limit_bytes=VMEM_LIMIT)


def _in_kernel(x_ref, g1_ref, w_ref, qg_ref, kg_ref, cos_ref, sa_ref, sb_ref, bd_ref,
               q_ref, k_ref, v_ref, u_ref):
    x = x_ref[...]
    ms = jnp.mean(x * x, axis=-1, keepdims=True)
    h = (x * lax.rsqrt(ms + RMS_EPS) * g1_ref[...]).astype(BF16)
    cos = cos_ref[...]
    sa = sa_ref[...]
    sb = sb_ref[...]
    bd = bd_ref[...]

    def qk(col0, gain, out_ref, scale):
        z = _dot(h, w_ref[:, col0:col0 + ATT_WIDTH])
        for c in range(ATT_WIDTH // LANES):
            zc = z[:, c * LANES:(c + 1) * LANES]
            zz = zc * zc
            hi = zz.astype(BF16)
            lo = (zz - hi.astype(F32)).astype(BF16)
            ssq = _dot(hi, bd) + _dot(lo, bd)
            y = zc * lax.rsqrt(ssq * (1.0 / HEAD_DIM) + RMS_EPS) * gain
            rot = y * cos + pltpu.roll(y, LANES - ROT_DIM // 2, 1) * sa + pltpu.roll(y, ROT_DIM // 2, 1) * sb
            out_ref[:, c * LANES:(c + 1) * LANES] = (rot * scale).astype(BF16)

    qk(0, qg_ref[...], q_ref, HEAD_DIM ** -0.5)
    qk(ATT_WIDTH, kg_ref[...], k_ref, 1.0)
    v_ref[...] = _dot(h, w_ref[:, 2 * ATT_WIDTH:3 * ATT_WIDTH]).astype(BF16)
    u_ref[...] = _dot(h, w_ref[:, 3 * ATT_WIDTH:]).astype(BF16)


def _in_call(x, g1, w, qg, kg, cos, sa, sb, bd):
    t = x.shape[0]
    seq_tiles = cos.shape[0] // ROW_TILE
    row = lambda i: (i, 0)
    fix = lambda i: (0, 0)
    tab = lambda i: (i % seq_tiles, 0)
    out = jax.ShapeDtypeStruct((t, ATT_WIDTH), BF16)
    return pl.pallas_call(
        _in_kernel,
        grid=(t // ROW_TILE,),
        in_specs=[
            pl.BlockSpec((ROW_TILE, D_MODEL), row),
            pl.BlockSpec((1, D_MODEL), fix),
            pl.BlockSpec((D_MODEL, IN_WIDTH), fix),
            pl.BlockSpec((1, LANES), fix),
            pl.BlockSpec((1, LANES), fix),
            pl.BlockSpec((ROW_TILE, LANES), tab),
            pl.BlockSpec((ROW_TILE, LANES), tab),
            pl.BlockSpec((ROW_TILE, LANES), tab),
            pl.BlockSpec((LANES, LANES), fix),
        ],
        out_specs=[pl.BlockSpec((ROW_TILE, ATT_WIDTH), row)] * 4,
        out_shape=[out] * 4,
        compiler_params=_params("parallel"),
        name="in_proj",
    )(x, g1, w, qg, kg, cos, sa, sb, bd)


def _attn_bias(q_off, k_idx, with_prev):
    ok = (k_idx >= q_off) & (k_idx <= q_off + ATT_BLK)
    if not with_prev:
        ok = ok & (k_idx >= ATT_BLK)
    return jnp.where(ok, 0.0, NEG_INF).astype(F32)


def _attn_kernel(q_ref, k_ref, v_ref, o_ref, q32, k32, v32, m_s, l_s, acc_s, bias_s):
    n_per = q32.shape[1]
    pad = ATT_BLK
    zeros = jnp.zeros((pad, LANES), F32)
    for r in range(RES):
        rows = pl.ds(r * n_per, n_per)
        q32[r] = q_ref[rows, :].astype(F32)
        k32[r, pl.ds(0, pad), :] = zeros
        v32[r, pl.ds(0, pad), :] = zeros
        k32[r, pl.ds(pad, n_per), :] = k_ref[rows, :].astype(F32)
        v32[r, pl.ds(pad, n_per), :] = v_ref[rows, :].astype(F32)

    qi = lax.broadcasted_iota(I32, (2 * ATT_BLK, 2 * ATT_BLK), 0) & (ATT_BLK - 1)
    kc = lax.broadcasted_iota(I32, (2 * ATT_BLK, 2 * ATT_BLK), 1)
    offs = (
        (16 * (qi & 7) + (qi >> 3), 16 * (kc & 15) + (kc >> 4)),
        (4 * (qi & 31) + (qi >> 5), 4 * (kc & 63) + (kc >> 6)),
        (qi, kc),
    )
    for br, (qo, ko) in enumerate(offs):
        bias_s[br, 0] = _attn_bias(qo, ko, False)
        bias_s[br, 1] = _attn_bias(qo, ko, True)

    head_a = lax.broadcasted_iota(I32, (ATT_BLK, LANES), 1) < HEAD_DIM

    def block(qb, ks, vs, bias):
        qa = jnp.where(head_a, qb, 0.0)
        qq = jnp.concatenate([qa, qb - qa], axis=0).astype(BF16)
        s = _dot_nt(qq, ks.astype(BF16)) + bias
        m = jnp.max(s, axis=1, keepdims=True)
        p = jnp.exp(s - m)
        l = jnp.sum(p, axis=1, keepdims=True)
        pv = _dot(p.astype(BF16), vs.astype(BF16))
        m2 = jnp.where(head_a, m[:ATT_BLK], m[ATT_BLK:])
        l2 = jnp.where(head_a, l[:ATT_BLK], l[ATT_BLK:])
        pv2 = jnp.where(head_a, pv[:ATT_BLK], pv[ATT_BLK:])
        return m2, l2, pv2

    def merge(old, new):
        m_o, l_o, a_o = old
        m_n, l_n, a_n = new
        m = jnp.maximum(m_o, m_n)
        e_o = jnp.exp(m_o - m)
        e_n = jnp.exp(m_n - m)
        return m, l_o * e_o + l_n * e_n, a_o * e_o + a_n * e_n

    def body16(r, carry):
        for c in range(n_per // ATT_BLK):
            rows = pl.ds(c * ATT_BLK, ATT_BLK)
            keys = pl.ds(c * ATT_BLK, 2 * ATT_BLK)
            m, l, a = block(q32[r, rows, :], k32[r, keys, :], v32[r, keys, :], bias_s[2, min(c, 1)])
            m_s[r, rows, :] = m
            l_s[r, rows, :] = l
            acc_s[r, rows, :] = a
        return carry

    lax.fori_loop(0, RES, body16, 0)

    sub = ATT_BLK // 4
    for r4 in range(4):
        slabs = [r4 + 4 * m for m in range(4)]

        def body4(c, carry, slabs=slabs):
            rows = pl.ds(pl.multiple_of(c * sub, sub), sub)
            keys = pl.ds(pl.multiple_of(pad - sub + c * sub, sub), 2 * sub)
            cat = lambda ref, idx: jnp.concatenate([ref[s, idx, :] for s in slabs], axis=0)
            new = block(cat(q32, rows), cat(k32, keys), cat(v32, keys), bias_s[1, jnp.minimum(c, 1)])
            old = (cat(m_s, rows), cat(l_s, rows), cat(acc_s, rows))
            m, l, a = merge(old, new)
            for j, s in enumerate(slabs):
                part = slice(j * sub, (j + 1) * sub)
                m_s[s, rows, :] = m[part]
                l_s[s, rows, :] = l[part]
                acc_s[s, rows, :] = a[part]
            return carry

        lax.fori_loop(0, n_per // sub, body4, 0)

    sub1 = ATT_BLK // RES

    def body1(j, carry):
        rows = pl.ds(pl.multiple_of(j * sub1, sub1), sub1)
        keys = pl.ds(pl.multiple_of(pad - sub1 + j * sub1, sub1), 2 * sub1)
        flat = lambda ref, idx, n: ref[:, idx, :].reshape(n, LANES)
        new = block(flat(q32, rows, ATT_BLK), flat(k32, keys, 2 * ATT_BLK), flat(v32, keys, 2 * ATT_BLK),
                    bias_s[0, jnp.minimum(j, 1)])
        old = (flat(m_s, rows, ATT_BLK), flat(l_s, rows, ATT_BLK), flat(acc_s, rows, ATT_BLK))
        m, l, a = merge(old, new)
        m_s[:, rows, :] = m.reshape(RES, sub1, LANES)
        l_s[:, rows, :] = l.reshape(RES, sub1, LANES)
        acc_s[:, rows, :] = a.reshape(RES, sub1, LANES)
        return carry

    lax.fori_loop(0, n_per // sub1, body1, 0)

    for r in range(RES):
        o_ref[pl.ds(r * n_per, n_per), :] = (acc_s[r] / l_s[r]).astype(BF16)


def _attn_call(q, k, v, seq):
    t = q.shape[0]
    n_per = seq // RES
    spec = pl.BlockSpec((seq, LANES), lambda b, h: (b, h))
    state = pltpu.VMEM((RES, n_per, LANES), F32)
    padded = pltpu.VMEM((RES, n_per + ATT_BLK, LANES), F32)
    return pl.pallas_call(
        _attn_kernel,
        grid=(t // seq, ATT_WIDTH // LANES),
        in_specs=[spec, spec, spec],
        out_specs=spec,
        out_shape=jax.ShapeDtypeStruct((t, ATT_WIDTH), BF16),
        scratch_shapes=[state, padded, padded, state, state, state,
                        pltpu.VMEM((3, 2, 2 * ATT_BLK, 2 * ATT_BLK), F32)],
        compiler_params=_params("parallel", "parallel"),
        name="dilated_attn",
    )(q, k, v)


def _row_min_index(cond, lane_f):
    return jnp.min(jnp.where(cond, lane_f, float(LANES)), axis=1, keepdims=True)


def _out_kernel(att_ref, u_ref, uh_ref, x_ref, wp_ref, ps_ref, wo_ref, g2_ref, rh_ref, rl_ref, br_ref,
                xo_ref, he_ref):
    i = pl.program_id(1)
    nb = u_ref.shape[2]
    rows = RES * nb
    u = u_ref[0].astype(F32)
    halo = jnp.where(i > 0, uh_ref[0][:, -1:, :].astype(F32), 0.0)
    u_prev = jnp.concatenate([halo, u[:, :nb - 1, :]], axis=1)

    n_idx = lax.broadcasted_iota(I32, (RES, nb, POOL_GROUP_DIM), 1) + i * nb
    r_idx = lax.broadcasted_iota(I32, (RES, nb, POOL_GROUP_DIM), 0)
    p1 = (RES * n_idx + r_idx + 1).astype(F32)

    pools = []
    for g, w in enumerate(POOL_WINDOWS):
        lanes = slice(g * POOL_GROUP_DIM, (g + 1) * POOL_GROUP_DIM)
        ug = u[:, :, lanes]
        upg = u_prev[:, :, lanes]
        tot = ug
        for j in range(1, w):
            tot = tot + jnp.concatenate([upg[RES - j:], ug[:RES - j]], axis=0)
        rg = tot / jnp.minimum(p1, float(w)) - ug
        y = _dot(rg.reshape(rows, POOL_GROUP_DIM).astype(BF16), wp_ref[g])
        pools.append((y * ps_ref[:, lanes]).astype(BF16))
    mix = jnp.concatenate([att_ref[0].reshape(rows, ATT_WIDTH)] + pools, axis=1)
    x = x_ref[0].reshape(rows, D_MODEL) + _dot(mix, wo_ref[...])
    xo_ref[0] = x.reshape(RES, nb, D_MODEL)

    ms = jnp.mean(x * x, axis=-1, keepdims=True)
    h = x * lax.rsqrt(ms + RMS_EPS) * g2_ref[...]
    he_ref[0, :, :, :D_MODEL] = h.reshape(RES, nb, D_MODEL)

    hh = h.astype(BF16)
    hl = (h - hh.astype(F32)).astype(BF16)
    logits = _dot(hh, rh_ref[...]) + (_dot(hl, rh_ref[...]) + _dot(hh, rl_ref[...])) + br_ref[...]
    lane = lax.broadcasted_iota(I32, (rows, LANES), 1)
    lane_f = lane.astype(F32)
    is_g = lane < N_EXPERT_GROUPS
    gl = jnp.where(is_g, logits, -jnp.inf)
    gm = jnp.max(gl, axis=1, keepdims=True)
    g_idx = _row_min_index(is_g & (gl == gm), lane_f)
    p_top = 1.0 / jnp.sum(jnp.where(is_g, jnp.exp(logits - gm), 0.0), axis=1, keepdims=True)
    e_lane = lane - N_EXPERT_GROUPS
    in_grp = (e_lane >= 0) & (e_lane < N_EXPERT_GROUPS * EXPERTS_PER_GROUP) & \
             ((e_lane >> 2).astype(F32) == g_idx)
    el = jnp.where(in_grp, logits, -jnp.inf)
    v1 = jnp.max(el, axis=1, keepdims=True)
    i1 = _row_min_index(in_grp & (el == v1), lane_f)
    rest = in_grp & (lane_f != i1)
    el2 = jnp.where(rest, logits, -jnp.inf)
    v2 = jnp.max(el2, axis=1, keepdims=True)
    i2 = _row_min_index(rest & (el2 == v2), lane_f)
    e21 = jnp.exp(v2 - v1)
    w1 = p_top / (1.0 + e21)
    w2 = p_top * e21 / (1.0 + e21)
    a1 = i1 - N_EXPERT_GROUPS - EXPERTS_PER_GROUP * g_idx
    a2 = i2 - N_EXPERT_GROUPS - EXPERTS_PER_GROUP * g_idx
    first_low = a1 < a2
    lo = jnp.where(first_low, a1, a2)
    hi = jnp.where(first_low, a2, a1)
    w_lo = jnp.where(first_low, w1, w2)
    w_hi = jnp.where(first_low, w2, w1)
    bucket = g_idx * PAIRS_PER_GROUP + lo * 3.0 - lo * (lo - 1.0) * 0.5 + hi - lo - 1.0
    meta = jnp.where(lane == 0, w_lo, jnp.where(lane == 1, w_hi, jnp.where(lane == 2, bucket, 0.0)))
    he_ref[0, :, :, D_MODEL:] = meta.reshape(RES, nb, META)


def _out_call(att, u, x, wp, ps, wo, g2, rh, rl, br, seq):
    t = x.shape[0]
    b = t // seq
    n_per = seq // RES
    nb = ROW_TILE // RES
    halo_rows = 16
    v4 = lambda a: a.reshape(b, RES, n_per, a.shape[-1])
    tile = lambda bi, i: (bi, 0, i, 0)
    halo = lambda bi, i: (bi, 0, jnp.maximum(i * (nb // halo_rows) - 1, 0), 0)
    fix2 = lambda bi, i: (0, 0)
    fix3 = lambda bi, i: (0, 0, 0)
    xo, he = pl.pallas_call(
        _out_kernel,
        grid=(b, n_per // nb),
        in_specs=[
            pl.BlockSpec((1, RES, nb, ATT_WIDTH), tile),
            pl.BlockSpec((1, RES, nb, POOL_WIDTH), tile),
            pl.BlockSpec((1, RES, halo_rows, POOL_WIDTH), halo),
            pl.BlockSpec((1, RES, nb, D_MODEL), tile),
            pl.BlockSpec((POOL_GROUPS, POOL_GROUP_DIM, POOL_GROUP_DIM), fix3),
            pl.BlockSpec((1, POOL_WIDTH), fix2),
            pl.BlockSpec((D_MODEL, D_MODEL), fix2),
            pl.BlockSpec((1, D_MODEL), fix2),
            pl.BlockSpec((D_MODEL, LANES), fix2),
            pl.BlockSpec((D_MODEL, LANES), fix2),
            pl.BlockSpec((1, LANES), fix2),
        ],
        out_specs=[pl.BlockSpec((1, RES, nb, D_MODEL), tile), pl.BlockSpec((1, RES, nb, ROW_EXT), tile)],
        out_shape=[jax.ShapeDtypeStruct((b, RES, n_per, D_MODEL), F32),
                   jax.ShapeDtypeStruct((b, RES, n_per, ROW_EXT), F32)],
        compiler_params=_params("parallel", "parallel"),
        name="out_proj_router",
    )(v4(att), v4(u), v4(u), v4(x), wp, ps, wo, g2, rh, rl, br)
    return xo.reshape(t, D_MODEL), he.reshape(t, ROW_EXT)


def _sort_kernel(meta_ref, pos_ref, tinfo_ref, cnt_s, off_s):
    phase = pl.program_id(0)
    i = pl.program_id(1)
    rows = meta_ref.shape[0]
    lane = lax.broadcasted_iota(I32, (rows, LANES), 1)
    onehot = lane.astype(F32) == meta_ref[:, 2:3]
    oh = onehot.astype(F32)
    tile_count = jnp.sum(oh, axis=0, keepdims=True)

    @pl.when((phase == 0) & (i == 0))
    def _():
        cnt_s[...] = jnp.zeros_like(cnt_s)

    @pl.when(phase == 0)
    def _():
        cnt_s[...] += tile_count

    @pl.when((phase == 1) & (i == 0))
    def _():
        tiles = jnp.floor((cnt_s[...] + (MOE_TILE - 1.0)) * (1.0 / MOE_TILE))
        tiles8 = jnp.broadcast_to(tiles, (SUBLANES, LANES)).astype(BF16)
        sq = (LANES, LANES)
        before = lax.broadcasted_iota(I32, sq, 0) < lax.broadcasted_iota(I32, sq, 1)
        start = _dot(tiles8, before.astype(BF16))
        off_s[...] = start[0:1] * float(MOE_TILE)
        cnt_s[...] = jnp.zeros_like(cnt_s)
        end = (start + tiles8.astype(F32)).astype(BF16)
        eye = (lax.broadcasted_iota(I32, sq, 0) == lax.broadcasted_iota(I32, sq, 1)).astype(BF16)
        end_col = _dot_nt(eye, end)[:, 0:1]
        tile_id = lax.broadcasted_iota(I32, sq, 1).astype(F32)
        tile_bucket = jnp.sum((end_col <= tile_id).astype(F32), axis=0, keepdims=True)
        total = jnp.max(end.astype(F32)[0:1], axis=1, keepdims=True)
        row_lane = lax.broadcasted_iota(I32, (1, LANES), 1)
        tinfo_ref[...] = jnp.where(row_lane == LANES - 1, total, tile_bucket).astype(I32)

    @pl.when(phase == 1)
    def _():
        sq = (rows, rows)
        upto = lax.broadcasted_iota(I32, sq, 1) <= lax.broadcasted_iota(I32, sq, 0)
        prefix = _dot(upto.astype(BF16), oh.astype(BF16))
        posv = jnp.where(onehot, prefix - 1.0 + cnt_s[...] + off_s[...], 0.0)
        hi = jnp.floor(posv * (1.0 / 256.0))
        lo = posv - hi * 256.0
        ones = jnp.ones((SUBLANES, LANES), BF16)
        pos = _dot_nt(ones, hi.astype(BF16)) * 256.0 + _dot_nt(ones, lo.astype(BF16))
        pos_ref[0] = pos[0:1].astype(I32)
        cnt_s[...] += tile_count


def _sort_call(he):
    t = he.shape[0]
    n_tiles = t // ROW_TILE
    return pl.pallas_call(
        _sort_kernel,
        grid=(2, n_tiles),
        in_specs=[pl.BlockSpec((ROW_TILE, META), lambda p, i: (i, D_MODEL // META))],
        out_specs=[pl.BlockSpec((1, 1, ROW_TILE), lambda p, i: (i * p, 0, 0)),
                   pl.BlockSpec((1, LANES), lambda p, i: (0, 0))],
        out_shape=[jax.ShapeDtypeStruct((n_tiles, 1, ROW_TILE), I32),
                   jax.ShapeDtypeStruct((1, LANES), I32)],
        scratch_shapes=[pltpu.VMEM((1, LANES), F32), pltpu.VMEM((1, LANES), F32)],
        compiler_params=_params("arbitrary", "arbitrary"),
        name="bucket_sort",
    )(he)


def _scatter_kernel(pos_ref, he_ref, init_ref, hs_ref, sem):
    del init_ref
    rows = he_ref.shape[0]

    def row_copy(t):
        return pltpu.make_async_copy(he_ref.at[pl.ds(t, 1), :], hs_ref.at[pl.ds(pos_ref[0, 0, t], 1), :], sem)

    def start(t, c):
        row_copy(t).start()
        return c

    def wait(t, c):
        row_copy(t).wait()
        return c

    lax.fori_loop(0, rows, start, 0)
    lax.fori_loop(0, rows, wait, 0)


def _scatter_call(pos, he, sorted_rows):
    t = he.shape[0]
    init = jnp.zeros((sorted_rows, ROW_EXT), F32)
    return pl.pallas_call(
        _scatter_kernel,
        grid=(t // ROW_TILE,),
        in_specs=[pl.BlockSpec((1, 1, ROW_TILE), lambda i: (i, 0, 0), memory_space=pltpu.SMEM),
                  pl.BlockSpec((ROW_TILE, ROW_EXT), lambda i: (i, 0)),
                  pl.BlockSpec(memory_space=pl.ANY)],
        out_specs=pl.BlockSpec(memory_space=pl.ANY),
        out_shape=jax.ShapeDtypeStruct((sorted_rows, ROW_EXT), F32),
        scratch_shapes=[pltpu.SemaphoreType.DMA(())],
        input_output_aliases={2: 0},
        compiler_params=_params("arbitrary"),
        name="row_scatter",
    )(pos, he, init)


def _gather_kernel(pos_ref, x_ref, ys_ref, o_ref, buf, sem):
    rows = x_ref.shape[0]

    def row_copy(t):
        return pltpu.make_async_copy(ys_ref.at[pl.ds(pos_ref[0, 0, t], 1), :], buf.at[pl.ds(t, 1), :], sem)

    def start(t, c):
        row_copy(t).start()
        return c

    def wait(t, c):
        row_copy(t).wait()
        return c

    lax.fori_loop(0, rows, start, 0)
    lax.fori_loop(0, rows, wait, 0)
    o_ref[...] = x_ref[...] + buf[...]


def _gather_call(pos, x, ys):
    t = x.shape[0]
    return pl.pallas_call(
        _gather_kernel,
        grid=(t // ROW_TILE,),
        in_specs=[pl.BlockSpec((1, 1, ROW_TILE), lambda i: (i, 0, 0), memory_space=pltpu.SMEM),
                  pl.BlockSpec((ROW_TILE, D_MODEL), lambda i: (i, 0)),
                  pl.BlockSpec(memory_space=pl.ANY)],
        out_specs=pl.BlockSpec((ROW_TILE, D_MODEL), lambda i: (i, 0)),
        out_shape=jax.ShapeDtypeStruct((t, D_MODEL), F32),
        scratch_shapes=[pltpu.VMEM((ROW_TILE, D_MODEL), F32), pltpu.SemaphoreType.DMA(())],
        compiler_params=_params("arbitrary"),
        name="row_gather_residual",
    )(pos, x, ys)


def _tile_group(j, tinfo):
    used = tinfo[LANES - 1]
    return tinfo[jnp.minimum(j, used - 1)] // PAIRS_PER_GROUP


def _moe_kernel(tinfo, hs_ref, wg_ref, wu_ref, wd_ref, ys_ref):
    j = pl.program_id(0)
    used = tinfo[LANES - 1]

    @pl.when(j < used)
    def _():
        pair = tinfo[j] % PAIRS_PER_GROUP
        e_lo = (pair >= 3).astype(I32) + (pair >= 5).astype(I32)
        e_hi = pair - (e_lo * 3 - (e_lo * (e_lo - 1)) // 2) + e_lo + 1
        xt = hs_ref[:, :D_MODEL].astype(BF16)

        def expert(e, gate):
            hg = _dot(xt, wg_ref[0, e])
            hu = _dot(xt, wu_ref[0, e])
            act = hg * (1.0 / (1.0 + jnp.exp(-hg))) * hu * gate
            return _dot(act.astype(BF16), wd_ref[0, e])

        ys_ref[...] = expert(e_lo, hs_ref[:, D_MODEL:D_MODEL + 1]) + \
            expert(e_hi, hs_ref[:, D_MODEL + 1:D_MODEL + 2])

    @pl.when(j >= used)
    def _():
        ys_ref[...] = jnp.zeros_like(ys_ref)


def _moe_call(tinfo, hs, wg, wu, wd):
    sorted_rows = hs.shape[0]
    row = lambda j, ti: (jnp.minimum(j, ti[LANES - 1] - 1), 0)
    grp = lambda j, ti: (_tile_group(j, ti), 0, 0, 0)
    return pl.pallas_call(
        _moe_kernel,
        grid_spec=pltpu.PrefetchScalarGridSpec(
            num_scalar_prefetch=1,
            grid=(sorted_rows // MOE_TILE,),
            in_specs=[pl.BlockSpec((MOE_TILE, ROW_EXT), row),
                      pl.BlockSpec((1, EXPERTS_PER_GROUP, D_MODEL, D_EXPERT), grp),
                      pl.BlockSpec((1, EXPERTS_PER_GROUP, D_MODEL, D_EXPERT), grp),
                      pl.BlockSpec((1, EXPERTS_PER_GROUP, D_EXPERT, D_MODEL), grp)],
            out_specs=pl.BlockSpec((MOE_TILE, D_MODEL), lambda j, ti: (j, 0)),
        ),
        out_shape=jax.ShapeDtypeStruct((sorted_rows, D_MODEL), F32),
        compiler_params=_params("arbitrary"),
        name="grouped_experts",
    )(tinfo, hs, wg, wu, wd)


def _rope_tables(seq):
    half = ROT_DIM // 2
    inv_freq = ROPE_THETA ** (-jnp.arange(0, ROT_DIM, 2, dtype=F32) / ROT_DIM)
    row = jnp.arange(seq)
    pos = (RES * (row % (seq // RES)) + row // (seq // RES)).astype(F32)
    ang = pos[:, None] * inv_freq[None, :]
    cos, sin = jnp.cos(ang), jnp.sin(ang)
    d = jnp.arange(LANES) % HEAD_DIM
    cos_l = jnp.where(d[None, :] < ROT_DIM, cos[:, d % half], 1.0)
    sin_l = sin[:, d % half]
    sa = jnp.where(d[None, :] < half, -sin_l, 0.0)
    sb = jnp.where((d[None, :] >= half) & (d[None, :] < ROT_DIM), sin_l, 0.0)
    return cos_l, sa, sb


def kernel(x, norm1_gain, w_in, q_norm_gain, k_norm_gain, w_pool, pool_scale, w_out, norm2_gain, w_group, b_group, w_router, b_router, w_gate, w_up, w_down):
    b, seq, d = x.shape
    depth = w_in.shape[0]
    t = b * seq
    n_per = seq // RES
    assert d == D_MODEL and seq % (RES * ATT_BLK) == 0 and t % ROW_TILE == 0
    sorted_rows = t + N_BUCKETS * MOE_TILE

    cos, sa, sb = _rope_tables(seq)
    lane_head = jnp.arange(LANES) // HEAD_DIM
    block_diag = (lane_head[:, None] == lane_head[None, :]).astype(BF16)

    w_in_b = w_in.astype(BF16)
    w_out_b = w_out.astype(BF16)
    w_pool_b = w_pool.astype(BF16)
    grouped = lambda w: w.astype(BF16).reshape(depth, N_EXPERT_GROUPS, EXPERTS_PER_GROUP, *w.shape[2:])
    w_gate_b, w_up_b, w_down_b = grouped(w_gate), grouped(w_up), grouped(w_down)
    n_logits = N_EXPERT_GROUPS * (1 + EXPERTS_PER_GROUP)
    w_r = jnp.pad(jnp.concatenate([w_group, w_router], axis=-1), ((0, 0), (0, 0), (0, LANES - n_logits)))
    w_r_hi = w_r.astype(BF16)
    w_r_lo = (w_r - w_r_hi.astype(F32)).astype(BF16)
    b_r = jnp.pad(jnp.concatenate([b_group, b_router], axis=-1), ((0, 0), (0, LANES - n_logits)))
    two_heads = lambda g: jnp.tile(g, (1, LANES // HEAD_DIM))

    xr = x.reshape(b, n_per, RES, d).transpose(0, 2, 1, 3).reshape(t, d)
    for l in range(depth):
        q, k, v, u = _in_call(xr, norm1_gain[l:l + 1], w_in_b[l], two_heads(q_norm_gain[l:l + 1]),
                              two_heads(k_norm_gain[l:l + 1]), cos, sa, sb, block_diag)
        att = _attn_call(q, k, v, seq)
        xr, he = _out_call(att, u, xr, w_pool_b[l], pool_scale[l:l + 1], w_out_b[l], norm2_gain[l:l + 1],
                           w_r_hi[l], w_r_lo[l], b_r[l:l + 1], seq)
        pos, tinfo = _sort_call(he)
        hs = _scatter_call(pos, he, sorted_rows)
        ys = _moe_call(tinfo.reshape(LANES), hs, w_gate_b[l], w_up_b[l], w_down_b[l])
        xr = _gather_call(pos, xr, ys)
    return xr.reshape(b, RES, n_per, d).transpose(0, 2, 1, 3).reshape(b, seq, d)
```

```python
import functools

import jax
import jax.numpy as jnp
from jax import lax
from jax.experimental import pallas as pl
from jax.experimental.pallas import tpu as pltpu

D_MODEL = 1024
N_HEADS = 8
HEAD_DIM = 64
ATT_WIDTH = N_HEADS * HEAD_DIM
POOL_GROUPS = 4
POOL_GROUP_DIM = 128
POOL_WIDTH = POOL_GROUPS * POOL_GROUP_DIM
POOL_WINDOWS = (2, 4, 8, 16)
IN_WIDTH = 3 * ATT_WIDTH + POOL_WIDTH
ROT_DIM = 16
ROPE_THETA = 500000.0
N_EXPERT_GROUPS = 4
EXPERTS_PER_GROUP = 4
D_EXPERT = 256
RMS_EPS = 1e-6
NEG_INF = -1e30
LOG2_E = 1.4426950408889634

LANES = 128
SUBLANES = 8
RES = 16
ATT_BLK = 128
PAIRS_PER_GROUP = 6
N_BUCKETS = N_EXPERT_GROUPS * PAIRS_PER_GROUP
ROW_TILE = 512
MOE_TILE = 256
ISSUE_UNROLL = 8
META = LANES
ROW_EXT = D_MODEL + META
VMEM_LIMIT = 56 * 1024 * 1024

F32 = jnp.float32
BF16 = jnp.bfloat16
I32 = jnp.int32


def _dot(a, b):
    return jnp.dot(a, b, preferred_element_type=F32)


def _dot_nt(a, b):
    return lax.dot_general(a, b, (((1,), (1,)), ((), ())), preferred_element_type=F32)


def _params(*sem):
    return pltpu.CompilerParams(dimension_semantics=sem, vmem_limit_bytes=VMEM_LIMIT)


def _in_kernel(x_ref, g1_ref, w_ref, qg_ref, kg_ref, cos_ref, sa_ref, sb_ref, bd_ref,
               q_ref, k_ref, v_ref, u_ref):
    x = x_ref[...]
    ms = jnp.mean(x * x, axis=-1, keepdims=True)
    h = (x * lax.rsqrt(ms + RMS_EPS) * g1_ref[...]).astype(BF16)
    cos = cos_ref[...]
    sa = sa_ref[...]
    sb = sb_ref[...]
    bd = bd_ref[...]

    def qk(col0, gain, out_ref, scale):
        z = _dot(h, w_ref[:, col0:col0 + ATT_WIDTH])
        for c in range(ATT_WIDTH // LANES):
            zc = z[:, c * LANES:(c + 1) * LANES]
            zz = zc * zc
            hi = zz.astype(BF16)
            lo = (zz - hi.astype(F32)).astype(BF16)
            ssq = _dot(hi, bd) + _dot(lo, bd)
            y = zc * lax.rsqrt(ssq * (1.0 / HEAD_DIM) + RMS_EPS) * gain
            rot = y * cos + pltpu.roll(y, LANES - ROT_DIM // 2, 1) * sa + pltpu.roll(y, ROT_DIM // 2, 1) * sb
            out_ref[:, c * LANES:(c + 1) * LANES] = (rot * scale).astype(BF16)

    qk(0, qg_ref[...], q_ref, HEAD_DIM ** -0.5 * LOG2_E)
    qk(ATT_WIDTH, kg_ref[...], k_ref, 1.0)
    v_ref[...] = _dot(h, w_ref[:, 2 * ATT_WIDTH:3 * ATT_WIDTH]).astype(BF16)
    u_ref[...] = _dot(h, w_ref[:, 3 * ATT_WIDTH:]).astype(BF16)


def _in_call(x, g1, w, layer, qg, kg, cos, sa, sb, bd):
    t = x.shape[0]
    seq_tiles = cos.shape[0] // ROW_TILE
    row = lambda i: (i, 0)
    fix = lambda i: (0, 0)
    tab = lambda i: (i % seq_tiles, 0)
    out = jax.ShapeDtypeStruct((t, ATT_WIDTH), BF16)
    return pl.pallas_call(
        _in_kernel,
        grid=(t // ROW_TILE,),
        in_specs=[
            pl.BlockSpec((ROW_TILE, D_MODEL), row),
            pl.BlockSpec((1, D_MODEL), fix),
            pl.BlockSpec((None, D_MODEL, IN_WIDTH), lambda i: (layer, 0, 0)),
            pl.BlockSpec((1, LANES), fix),
            pl.BlockSpec((1, LANES), fix),
            pl.BlockSpec((ROW_TILE, LANES), tab),
            pl.BlockSpec((ROW_TILE, LANES), tab),
            pl.BlockSpec((ROW_TILE, LANES), tab),
            pl.BlockSpec((LANES, LANES), fix),
        ],
        out_specs=[pl.BlockSpec((ROW_TILE, ATT_WIDTH), row)] * 4,
        out_shape=[out] * 4,
        compiler_params=_params("parallel"),
        name="in_proj",
    )(x, g1, w, qg, kg, cos, sa, sb, bd)


def _attn_bias(q_off, k_idx, with_prev):
    ok = (k_idx >= q_off) & (k_idx <= q_off + ATT_BLK)
    if not with_prev:
        ok = ok & (k_idx >= ATT_BLK)
    return jnp.where(ok, 0.0, NEG_INF).astype(F32)


def _attn_kernel(q_ref, k_ref, v_ref, o_ref, q32, k32, v32, m_s, l_s, acc_s, bias_s):
    n_per = q32.shape[1]
    pad = ATT_BLK
    zeros = jnp.zeros((pad, LANES), F32)
    for r in range(RES):
        rows = pl.ds(r * n_per, n_per)
        q32[r] = q_ref[rows, :].astype(F32)
        k32[r, pl.ds(0, pad), :] = zeros
        v32[r, pl.ds(0, pad), :] = zeros
        k32[r, pl.ds(pad, n_per), :] = k_ref[rows, :].astype(F32)
        v32[r, pl.ds(pad, n_per), :] = v_ref[rows, :].astype(F32)

    qi = lax.broadcasted_iota(I32, (2 * ATT_BLK, 2 * ATT_BLK), 0) & (ATT_BLK - 1)
    kc = lax.broadcasted_iota(I32, (2 * ATT_BLK, 2 * ATT_BLK), 1)
    offs = (
        (16 * (qi & 7) + (qi >> 3), 16 * (kc & 15) + (kc >> 4)),
        (4 * (qi & 31) + (qi >> 5), 4 * (kc & 63) + (kc >> 6)),
        (qi, kc),
    )
    for br, (qo, ko) in enumerate(offs):
        bias_s[br, 0] = _attn_bias(qo, ko, False)
        bias_s[br, 1] = _attn_bias(qo, ko, True)

    head_a = lax.broadcasted_iota(I32, (ATT_BLK, LANES), 1) < HEAD_DIM

    def block(qb, ks, vs, bias):
        qa = jnp.where(head_a, qb, 0.0)
        qq = jnp.concatenate([qa, qb - qa], axis=0).astype(BF16)
        s = _dot_nt(qq, ks.astype(BF16)) + bias
        m = jnp.max(s, axis=1, keepdims=True)
        p = jnp.exp2(s - m)
        l = jnp.sum(p, axis=1, keepdims=True)
        pv = _dot(p.astype(BF16), vs.astype(BF16))
        m2 = jnp.where(head_a, m[:ATT_BLK], m[ATT_BLK:])
        l2 = jnp.where(head_a, l[:ATT_BLK], l[ATT_BLK:])
        pv2 = jnp.where(head_a, pv[:ATT_BLK], pv[ATT_BLK:])
        return m2, l2, pv2

    def merge(old, new):
        m_o, l_o, a_o = old
        m_n, l_n, a_n = new
        m = jnp.maximum(m_o, m_n)
        e_o = jnp.exp2(m_o - m)
        e_n = jnp.exp2(m_n - m)
        return m, l_o * e_o + l_n * e_n, a_o * e_o + a_n * e_n


    def body16(rr, carry):
        slabs = [2 * rr, 2 * rr + 1]
        data = [(q32[r], k32[r], v32[r]) for r in slabs]
        outs = []
        for qf, kf, vf in data:
            res = []
            for c in range(n_per // ATT_BLK):
                rows = slice(c * ATT_BLK, (c + 1) * ATT_BLK)
                keys = slice(c * ATT_BLK, (c + 2) * ATT_BLK)
                res.append(block(qf[rows], kf[keys], vf[keys], bias_s[2, min(c, 1)]))
            outs.append(res)
        for r, res in zip(slabs, outs):
            for ref, idx in ((m_s, 0), (l_s, 1), (acc_s, 2)):
                ref[r] = jnp.concatenate([blk[idx] for blk in res], axis=0)
        return carry

    lax.fori_loop(0, RES // 2, body16, 0)

    sub = ATT_BLK // 4

    def body4(c, carry):
        rows = pl.ds(pl.multiple_of(c * sub, sub), sub)
        keys = pl.ds(pl.multiple_of(pad - sub + c * sub, sub), 2 * sub)
        groups = [[r4 + 4 * m for m in range(4)] for r4 in range(4)]
        cat = lambda ref, idx, slabs: jnp.concatenate([ref[s, idx, :] for s in slabs], axis=0)
        ins = [(cat(q32, rows, g), cat(k32, keys, g), cat(v32, keys, g)) for g in groups]
        olds = [(cat(m_s, rows, g), cat(l_s, rows, g), cat(acc_s, rows, g)) for g in groups]
        bias = bias_s[1, jnp.minimum(c, 1)]
        news = [block(qb, ks, vs, bias) for qb, ks, vs in ins]
        for g, old, new in zip(groups, olds, news):
            m, l, a = merge(old, new)
            for j, s in enumerate(g):
                part = slice(j * sub, (j + 1) * sub)
                m_s[s, rows, :] = m[part]
                l_s[s, rows, :] = l[part]
                acc_s[s, rows, :] = a[part]
        return carry

    lax.fori_loop(0, n_per // sub, body4, 0)

    sub1 = ATT_BLK // RES
    per_step = 4
    span = per_step * sub1

    def body1(jj, carry):
        rows = pl.ds(pl.multiple_of(jj * span, span), span)
        keys = pl.ds(pl.multiple_of(pad - sub1 + jj * span, sub1), span + sub1)
        qg, kg, vg = q32[:, rows, :], k32[:, keys, :], v32[:, keys, :]
        og = (m_s[:, rows, :], l_s[:, rows, :], acc_s[:, rows, :])
        res = []
        for g in range(per_step):
            mine = slice(g * sub1, (g + 1) * sub1)
            both = slice(g * sub1, (g + 2) * sub1)
            bias = bias_s[0, jnp.minimum(jj, 1)] if g == 0 else bias_s[0, 1]
            new = block(qg[:, mine].reshape(ATT_BLK, LANES), kg[:, both].reshape(2 * ATT_BLK, LANES),
                        vg[:, both].reshape(2 * ATT_BLK, LANES), bias)
            old = tuple(o[:, mine].reshape(ATT_BLK, LANES) for o in og)
            res.append(merge(old, new))
        for ref, idx in ((m_s, 0), (l_s, 1), (acc_s, 2)):
            ref[:, rows, :] = jnp.concatenate([r[idx].reshape(RES, sub1, LANES) for r in res], axis=1)
        return carry

    lax.fori_loop(0, n_per // span, body1, 0)

    for r in range(RES):
        o_ref[pl.ds(r * n_per, n_per), :] = (acc_s[r] / l_s[r]).astype(BF16)


def _attn_call(q, k, v, seq):
    t = q.shape[0]
    n_per = seq // RES
    spec = pl.BlockSpec((seq, LANES), lambda b, h: (b, h))
    state = pltpu.VMEM((RES, n_per, LANES), F32)
    padded = pltpu.VMEM((RES, n_per + ATT_BLK, LANES), F32)
    return pl.pallas_call(
        _attn_kernel,
        grid=(t // seq, ATT_WIDTH // LANES),
        in_specs=[spec, spec, spec],
        out_specs=spec,
        out_shape=jax.ShapeDtypeStruct((t, ATT_WIDTH), BF16),
        scratch_shapes=[state, padded, padded, state, state, state,
                        pltpu.VMEM((3, 2, 2 * ATT_BLK, 2 * ATT_BLK), F32)],
        compiler_params=_params("parallel", "parallel"),
        name="dilated_attn",
    )(q, k, v)


def _row_min_index(cond, lane_f):
    return jnp.min(jnp.where(cond, lane_f, float(LANES)), axis=1, keepdims=True)


def _out_kernel(att_ref, u_ref, uh_ref, x_ref, wp_ref, ps_ref, wo_ref, g2_ref, rh_ref, rl_ref, br_ref,
                xo_ref, he_ref):
    i = pl.program_id(1)
    nb = u_ref.shape[2]
    rows = RES * nb
    u = u_ref[0].astype(F32)
    halo = jnp.where(i > 0, uh_ref[0][:, -1:, :].astype(F32), 0.0)
    u_prev = jnp.concatenate([halo, u[:, :nb - 1, :]], axis=1)

    n_idx = lax.broadcasted_iota(I32, (RES, nb, POOL_GROUP_DIM), 1) + i * nb
    r_idx = lax.broadcasted_iota(I32, (RES, nb, POOL_GROUP_DIM), 0)
    p1 = (RES * n_idx + r_idx + 1).astype(F32)

    pools = []
    for g, w in enumerate(POOL_WINDOWS):
        lanes = slice(g * POOL_GROUP_DIM, (g + 1) * POOL_GROUP_DIM)
        ug = u[:, :, lanes]
        upg = u_prev[:, :, lanes]
        tot = ug
        for j in range(1, w):
            tot = tot + jnp.concatenate([upg[RES - j:], ug[:RES - j]], axis=0)
        rg = tot / jnp.minimum(p1, float(w)) - ug
        y = _dot(rg.reshape(rows, POOL_GROUP_DIM).astype(BF16), wp_ref[g])
        pools.append((y * ps_ref[:, lanes]).astype(BF16))
    mix = jnp.concatenate([att_ref[0].reshape(rows, ATT_WIDTH)] + pools, axis=1)
    x = x_ref[0].reshape(rows, D_MODEL) + _dot(mix, wo_ref[...])
    xo_ref[0] = x.reshape(RES, nb, D_MODEL)

    ms = jnp.mean(x * x, axis=-1, keepdims=True)
    h = x * lax.rsqrt(ms + RMS_EPS) * g2_ref[...]
    he_ref[0, :, :, :D_MODEL] = h.reshape(RES, nb, D_MODEL)

    hh = h.astype(BF16)
    hl = (h - hh.astype(F32)).astype(BF16)
    logits = _dot(hh, rh_ref[...]) + (_dot(hl, rh_ref[...]) + _dot(hh, rl_ref[...])) + br_ref[...]
    lane = lax.broadcasted_iota(I32, (rows, LANES), 1)
    lane_f = lane.astype(F32)
    is_g = lane < N_EXPERT_GROUPS
    gl = jnp.where(is_g, logits, -jnp.inf)
    gm = jnp.max(gl, axis=1, keepdims=True)
    g_idx = _row_min_index(is_g & (gl == gm), lane_f)
    p_top = 1.0 / jnp.sum(jnp.where(is_g, jnp.exp(logits - gm), 0.0), axis=1, keepdims=True)
    e_lane = lane - N_EXPERT_GROUPS
    in_grp = (e_lane >= 0) & (e_lane < N_EXPERT_GROUPS * EXPERTS_PER_GROUP) & \
             ((e_lane >> 2).astype(F32) == g_idx)
    el = jnp.where(in_grp, logits, -jnp.inf)
    v1 = jnp.max(el, axis=1, keepdims=True)
    i1 = _row_min_index(in_grp & (el == v1), lane_f)
    rest = in_grp & (lane_f != i1)
    el2 = jnp.where(rest, logits, -jnp.inf)
    v2 = jnp.max(el2, axis=1, keepdims=True)
    i2 = _row_min_index(rest & (el2 == v2), lane_f)
    e21 = jnp.exp(v2 - v1)
    w1 = p_top / (1.0 + e21)
    w2 = p_top * e21 / (1.0 + e21)
    a1 = i1 - N_EXPERT_GROUPS - EXPERTS_PER_GROUP * g_idx
    a2 = i2 - N_EXPERT_GROUPS - EXPERTS_PER_GROUP * g_idx
    first_low = a1 < a2
    lo = jnp.where(first_low, a1, a2)
    hi = jnp.where(first_low, a2, a1)
    w_lo = jnp.where(first_low, w1, w2)
    w_hi = jnp.where(first_low, w2, w1)
    bucket = g_idx * PAIRS_PER_GROUP + lo * 3.0 - lo * (lo - 1.0) * 0.5 + hi - lo - 1.0
    meta = jnp.where(lane == 0, w_lo, jnp.where(lane == 1, w_hi, jnp.where(lane == 2, bucket, 0.0)))
    he_ref[0, :, :, D_MODEL:] = meta.reshape(RES, nb, META)


def _out_call(att, u, x, wp, ps, wo, layer, g2, rh, rl, br, seq):
    t = x.shape[0]
    b = t // seq
    n_per = seq // RES
    nb = ROW_TILE // RES
    halo_rows = 16
    v4 = lambda a: a.reshape(b, RES, n_per, a.shape[-1])
    tile = lambda bi, i: (bi, 0, i, 0)
    halo = lambda bi, i: (bi, 0, jnp.maximum(i * (nb // halo_rows) - 1, 0), 0)
    fix2 = lambda bi, i: (0, 0)
    xo, he = pl.pallas_call(
        _out_kernel,
        grid=(b, n_per // nb),
        in_specs=[
            pl.BlockSpec((1, RES, nb, ATT_WIDTH), tile),
            pl.BlockSpec((1, RES, nb, POOL_WIDTH), tile),
            pl.BlockSpec((1, RES, halo_rows, POOL_WIDTH), halo),
            pl.BlockSpec((1, RES, nb, D_MODEL), tile),
            pl.BlockSpec((None, POOL_GROUPS, POOL_GROUP_DIM, POOL_GROUP_DIM), lambda bi, i: (layer, 0, 0, 0)),
            pl.BlockSpec((1, POOL_WIDTH), fix2),
            pl.BlockSpec((None, D_MODEL, D_MODEL), lambda bi, i: (layer, 0, 0)),
            pl.BlockSpec((1, D_MODEL), fix2),
            pl.BlockSpec((D_MODEL, LANES), fix2),
            pl.BlockSpec((D_MODEL, LANES), fix2),
            pl.BlockSpec((1, LANES), fix2),
        ],
        out_specs=[pl.BlockSpec((1, RES, nb, D_MODEL), tile), pl.BlockSpec((1, RES, nb, ROW_EXT), tile)],
        out_shape=[jax.ShapeDtypeStruct((b, RES, n_per, D_MODEL), F32),
                   jax.ShapeDtypeStruct((b, RES, n_per, ROW_EXT), F32)],
        compiler_params=_params("parallel", "parallel"),
        name="out_proj_router",
    )(v4(att), v4(u), v4(u), v4(x), wp, ps, wo, g2, rh, rl, br)
    return xo.reshape(t, D_MODEL), he.reshape(t, ROW_EXT)


def _sort_kernel(meta_ref, pos_ref, tinfo_ref, cnt_s, off_s):
    phase = pl.program_id(0)
    i = pl.program_id(1)
    rows = meta_ref.shape[0]
    lane = lax.broadcasted_iota(I32, (rows, LANES), 1)
    onehot = lane.astype(F32) == meta_ref[:, 2:3]
    oh = onehot.astype(F32)
    tile_count = jnp.sum(oh, axis=0, keepdims=True)

    @pl.when((phase == 0) & (i == 0))
    def _():
        cnt_s[...] = jnp.zeros_like(cnt_s)

    @pl.when(phase == 0)
    def _():
        cnt_s[...] += tile_count

    @pl.when((phase == 1) & (i == 0))
    def _():
        tiles = jnp.floor((cnt_s[...] + (MOE_TILE - 1.0)) * (1.0 / MOE_TILE))
        tiles8 = jnp.broadcast_to(tiles, (SUBLANES, LANES)).astype(BF16)
        sq = (LANES, LANES)
        before = lax.broadcasted_iota(I32, sq, 0) < lax.broadcasted_iota(I32, sq, 1)
        start = _dot(tiles8, before.astype(BF16))
        off_s[...] = start[0:1] * float(MOE_TILE)
        cnt_s[...] = jnp.zeros_like(cnt_s)
        end = (start + tiles8.astype(F32)).astype(BF16)
        eye = (lax.broadcasted_iota(I32, sq, 0) == lax.broadcasted_iota(I32, sq, 1)).astype(BF16)
        end_col = _dot_nt(eye, end)[:, 0:1]
        tile_id = lax.broadcasted_iota(I32, sq, 1).astype(F32)
        tile_bucket = jnp.sum((end_col <= tile_id).astype(F32), axis=0, keepdims=True)
        total = jnp.max(end.astype(F32)[0:1], axis=1, keepdims=True)
        row_lane = lax.broadcasted_iota(I32, (1, LANES), 1)
        tinfo_ref[...] = jnp.where(row_lane == LANES - 1, total, tile_bucket).astype(I32)

    @pl.when(phase == 1)
    def _():
        sq = (rows, rows)
        upto = lax.broadcasted_iota(I32, sq, 1) <= lax.broadcasted_iota(I32, sq, 0)
        prefix = _dot(upto.astype(BF16), oh.astype(BF16))
        posv = jnp.where(onehot, prefix - 1.0 + cnt_s[...] + off_s[...], 0.0)
        hi = jnp.floor(posv * (1.0 / 256.0))
        lo = posv - hi * 256.0
        ones = jnp.ones((SUBLANES, LANES), BF16)
        pos = _dot_nt(ones, hi.astype(BF16)) * 256.0 + _dot_nt(ones, lo.astype(BF16))
        pos_ref[0] = pos[0:1].astype(I32)
        cnt_s[...] += tile_count


def _sort_call(he):
    t = he.shape[0]
    n_tiles = t // ROW_TILE
    return pl.pallas_call(
        _sort_kernel,
        grid=(2, n_tiles),
        in_specs=[pl.BlockSpec((ROW_TILE, META), lambda p, i: (i, D_MODEL // META))],
        out_specs=[pl.BlockSpec((1, 1, ROW_TILE), lambda p, i: (i * p, 0, 0)),
                   pl.BlockSpec((1, LANES), lambda p, i: (0, 0))],
        out_shape=[jax.ShapeDtypeStruct((n_tiles, 1, ROW_TILE), I32),
                   jax.ShapeDtypeStruct((1, LANES), I32)],
        scratch_shapes=[pltpu.VMEM((1, LANES), F32), pltpu.VMEM((1, LANES), F32)],
        compiler_params=_params("arbitrary", "arbitrary"),
        name="bucket_sort",
    )(he)


def _scatter_kernel(pos_ref, he_ref, init_ref, hs_ref, sem):
    del init_ref
    rows = he_ref.shape[0]

    def row_copy(t):
        return pltpu.make_async_copy(he_ref.at[pl.ds(t, 1), :], hs_ref.at[pl.ds(pos_ref[0, 0, t], 1), :], sem)

    def start(t, c):
        row_copy(t).start()
        return c

    lax.fori_loop(0, rows, start, 0, unroll=ISSUE_UNROLL)
    pltpu.make_async_copy(he_ref, hs_ref.at[pl.ds(0, rows), :], sem).wait()


def _scatter_call(pos, he, sorted_rows):
    t = he.shape[0]
    init = jnp.zeros((sorted_rows, ROW_EXT), F32)
    return pl.pallas_call(
        _scatter_kernel,
        grid=(t // ROW_TILE,),
        in_specs=[pl.BlockSpec((1, 1, ROW_TILE), lambda i: (i, 0, 0), memory_space=pltpu.SMEM),
                  pl.BlockSpec((ROW_TILE, ROW_EXT), lambda i: (i, 0)),
                  pl.BlockSpec(memory_space=pl.ANY)],
        out_specs=pl.BlockSpec(memory_space=pl.ANY),
        out_shape=jax.ShapeDtypeStruct((sorted_rows, ROW_EXT), F32),
        scratch_shapes=[pltpu.SemaphoreType.DMA(())],
        input_output_aliases={2: 0},
        compiler_params=_params("arbitrary"),
        name="row_scatter",
    )(pos, he, init)


def _gather_kernel(pos_ref, x_ref, ys_ref, o_ref, buf, sem):
    rows = x_ref.shape[0]

    def row_copy(t):
        return pltpu.make_async_copy(ys_ref.at[pl.ds(pos_ref[0, 0, t], 1), :], buf.at[pl.ds(t, 1), :], sem)

    def start(t, c):
        row_copy(t).start()
        return c

    lax.fori_loop(0, rows, start, 0, unroll=ISSUE_UNROLL)
    pltpu.make_async_copy(ys_ref.at[pl.ds(0, rows), :], buf, sem).wait()
    o_ref[...] = x_ref[...] + buf[...]


def _gather_call(pos, x, ys):
    t = x.shape[0]
    return pl.pallas_call(
        _gather_kernel,
        grid=(t // ROW_TILE,),
        in_specs=[pl.BlockSpec((1, 1, ROW_TILE), lambda i: (i, 0, 0), memory_space=pltpu.SMEM),
                  pl.BlockSpec((ROW_TILE, D_MODEL), lambda i: (i, 0)),
                  pl.BlockSpec(memory_space=pl.ANY)],
        out_specs=pl.BlockSpec((ROW_TILE, D_MODEL), lambda i: (i, 0)),
        out_shape=jax.ShapeDtypeStruct((t, D_MODEL), F32),
        scratch_shapes=[pltpu.VMEM((ROW_TILE, D_MODEL), F32), pltpu.SemaphoreType.DMA(())],
        compiler_params=_params("arbitrary"),
        name="row_gather_residual",
    )(pos, x, ys)


def _tile_group(j, tinfo):
    used = tinfo[LANES - 1]
    return tinfo[jnp.minimum(j, used - 1)] // PAIRS_PER_GROUP


def _moe_kernel(tinfo, hs_ref, wg_ref, wu_ref, wd_ref, ys_ref):
    j = pl.program_id(0)
    used = tinfo[LANES - 1]

    @pl.when(j < used)
    def _():
        pair = tinfo[j] % PAIRS_PER_GROUP
        e_lo = (pair >= 3).astype(I32) + (pair >= 5).astype(I32)
        e_hi = pair - (e_lo * 3 - (e_lo * (e_lo - 1)) // 2) + e_lo + 1
        xt = hs_ref[:, :D_MODEL].astype(BF16)

        def expert(e, gate):
            hg = _dot(xt, wg_ref[e])
            hu = _dot(xt, wu_ref[e])
            act = hg * (1.0 / (1.0 + jnp.exp(-hg))) * hu * gate
            return _dot(act.astype(BF16), wd_ref[e])

        ys_ref[...] = expert(e_lo, hs_ref[:, D_MODEL:D_MODEL + 1]) + \
            expert(e_hi, hs_ref[:, D_MODEL + 1:D_MODEL + 2])

    @pl.when(j >= used)
    def _():
        ys_ref[...] = jnp.zeros_like(ys_ref)


def _moe_call(tinfo, hs, wg, wu, wd, layer):
    sorted_rows = hs.shape[0]
    row = lambda j, ti: (jnp.minimum(j, ti[LANES - 1] - 1), 0)
    grp = lambda j, ti: (layer, _tile_group(j, ti), 0, 0, 0)
    return pl.pallas_call(
        _moe_kernel,
        grid_spec=pltpu.PrefetchScalarGridSpec(
            num_scalar_prefetch=1,
            grid=(sorted_rows // MOE_TILE,),
            in_specs=[pl.BlockSpec((MOE_TILE, ROW_EXT), row),
                      pl.BlockSpec((None, None, EXPERTS_PER_GROUP, D_MODEL, D_EXPERT), grp),
                      pl.BlockSpec((None, None, EXPERTS_PER_GROUP, D_MODEL, D_EXPERT), grp),
                      pl.BlockSpec((None, None, EXPERTS_PER_GROUP, D_EXPERT, D_MODEL), grp)],
            out_specs=pl.BlockSpec((MOE_TILE, D_MODEL), lambda j, ti: (j, 0)),
        ),
        out_shape=jax.ShapeDtypeStruct((sorted_rows, D_MODEL), F32),
        compiler_params=_params("arbitrary"),
        name="grouped_experts",
    )(tinfo, hs, wg, wu, wd)


def _rope_tables(seq):
    half = ROT_DIM // 2
    inv_freq = ROPE_THETA ** (-jnp.arange(0, ROT_DIM, 2, dtype=F32) / ROT_DIM)
    row = jnp.arange(seq)
    pos = (RES * (row % (seq // RES)) + row // (seq // RES)).astype(F32)
    ang = pos[:, None] * inv_freq[None, :]
    cos, sin = jnp.cos(ang), jnp.sin(ang)
    d = jnp.arange(LANES) % HEAD_DIM
    cos_l = jnp.where(d[None, :] < ROT_DIM, cos[:, d % half], 1.0)
    sin_l = sin[:, d % half]
    sa = jnp.where(d[None, :] < half, -sin_l, 0.0)
    sb = jnp.where((d[None, :] >= half) & (d[None, :] < ROT_DIM), sin_l, 0.0)
    return cos_l, sa, sb


def kernel(x, norm1_gain, w_in, q_norm_gain, k_norm_gain, w_pool, pool_scale, w_out, norm2_gain, w_group, b_group, w_router, b_router, w_gate, w_up, w_down):
    b, seq, d = x.shape
    depth = w_in.shape[0]
    t = b * seq
    n_per = seq // RES
    assert d == D_MODEL and seq % (RES * ATT_BLK) == 0 and t % ROW_TILE == 0
    sorted_rows = t + N_BUCKETS * MOE_TILE

    cos, sa, sb = _rope_tables(seq)
    lane_head = jnp.arange(LANES) // HEAD_DIM
    block_diag = (lane_head[:, None] == lane_head[None, :]).astype(BF16)

    w_in_b = w_in.astype(BF16)
    w_out_b = w_out.astype(BF16)
    w_pool_b = w_pool.astype(BF16)
    grouped = lambda w: w.astype(BF16).reshape(depth, N_EXPERT_GROUPS, EXPERTS_PER_GROUP, *w.shape[2:])
    w_gate_b, w_up_b, w_down_b = grouped(w_gate), grouped(w_up), grouped(w_down)
    n_logits = N_EXPERT_GROUPS * (1 + EXPERTS_PER_GROUP)
    w_r = jnp.pad(jnp.concatenate([w_group, w_router], axis=-1), ((0, 0), (0, 0), (0, LANES - n_logits)))
    w_r_hi = w_r.astype(BF16)
    w_r_lo = (w_r - w_r_hi.astype(F32)).astype(BF16)
    b_r = jnp.pad(jnp.concatenate([b_group, b_router], axis=-1), ((0, 0), (0, LANES - n_logits)))
    two_heads = lambda g: jnp.tile(g, (1, LANES // HEAD_DIM))

    xr = x.reshape(b, n_per, RES, d).transpose(0, 2, 1, 3).reshape(t, d)
    for l in range(depth):
        q, k, v, u = _in_call(xr, norm1_gain[l:l + 1], w_in_b, l, two_heads(q_norm_gain[l:l + 1]),
                              two_heads(k_norm_gain[l:l + 1]), cos, sa, sb, block_diag)
        att = _attn_call(q, k, v, seq)
        xr, he = _out_call(att, u, xr, w_pool_b, pool_scale[l:l + 1], w_out_b, l, norm2_gain[l:l + 1],
                           w_r_hi[l], w_r_lo[l], b_r[l:l + 1], seq)
        pos, tinfo = _sort_call(he)
        hs = _scatter_call(pos, he, sorted_rows)
        ys = _moe_call(tinfo.reshape(LANES), hs, w_gate_b, w_up_b, w_down_b, l)
        xr = _gather_call(pos, xr, ys)
    return xr.reshape(b, RES, n_per, d).transpose(0, 2, 1, 3).reshape(b, seq, d)
```

```python
import functools

import jax
import jax.numpy as jnp
from jax import lax
from jax.experimental import pallas as pl
from jax.experimental.pallas import tpu as pltpu

D_MODEL = 1024
N_HEADS = 8
HEAD_DIM = 64
ATT_WIDTH = N_HEADS * HEAD_DIM
POOL_GROUPS = 4
POOL_GROUP_DIM = 128
POOL_WIDTH = POOL_GROUPS * POOL_GROUP_DIM
POOL_WINDOWS = (2, 4, 8, 16)
IN_WIDTH = 3 * ATT_WIDTH + POOL_WIDTH
ROT_DIM = 16
ROPE_THETA = 500000.0
N_EXPERT_GROUPS = 4
EXPERTS_PER_GROUP = 4
D_EXPERT = 256
RMS_EPS = 1e-6
NEG_INF = -1e30
LOG2_E = 1.4426950408889634

LANES = 128
SUBLANES = 8
RES = 16
ATT_BLK = 128
ATT_STEP_BLOCKS = 32
PAIRS_PER_GROUP = 6
N_BUCKETS = N_EXPERT_GROUPS * PAIRS_PER_GROUP
ROW_TILE = 512
MOE_TILE = 256
ISSUE_UNROLL = 8
END_LANE = 96
META = LANES
ROW_EXT = D_MODEL + META
VMEM_LIMIT = 56 * 1024 * 1024

F32 = jnp.float32
BF16 = jnp.bfloat16
I32 = jnp.int32


def _dot(a, b):
    return jnp.dot(a, b, preferred_element_type=F32)


def _dot_nt(a, b):
    return lax.dot_general(a, b, (((1,), (1,)), ((), ())), preferred_element_type=F32)


def _params(*sem):
    return pltpu.CompilerParams(dimension_semantics=sem, vmem_limit_bytes=VMEM_LIMIT)


def _in_kernel(x_ref, g1_ref, w_ref, qg_ref, kg_ref, cos_ref, sa_ref, sb_ref, bd_ref,
               q_ref, k_ref, v_ref, u_ref):
    x = x_ref[...]
    ms = jnp.mean(x * x, axis=-1, keepdims=True)
    h = (x * lax.rsqrt(ms + RMS_EPS) * g1_ref[...]).astype(BF16)
    cos = cos_ref[...]
    sa = sa_ref[...]
    sb = sb_ref[...]
    bd = bd_ref[...]

    def qk(col0, gain, out_ref, scale):
        z = _dot(h, w_ref[:, col0:col0 + ATT_WIDTH])
        for c in range(ATT_WIDTH // LANES):
            zc = z[:, c * LANES:(c + 1) * LANES]
            zz = zc * zc
            hi = zz.astype(BF16)
            lo = (zz - hi.astype(F32)).astype(BF16)
            ssq = _dot(hi, bd) + _dot(lo, bd)
            y = zc * lax.rsqrt(ssq * (1.0 / HEAD_DIM) + RMS_EPS) * gain
            rot = y * cos + pltpu.roll(y, LANES - ROT_DIM // 2, 1) * sa + pltpu.roll(y, ROT_DIM // 2, 1) * sb
            out_ref[:, c * LANES:(c + 1) * LANES] = (rot * scale).astype(BF16)

    qk(0, qg_ref[...], q_ref, HEAD_DIM ** -0.5 * LOG2_E)
    qk(ATT_WIDTH, kg_ref[...], k_ref, 1.0)
    v_ref[...] = _dot(h, w_ref[:, 2 * ATT_WIDTH:3 * ATT_WIDTH]).astype(BF16)
    u_ref[...] = _dot(h, w_ref[:, 3 * ATT_WIDTH:]).astype(BF16)


def _in_call(x, g1, w, layer, qg, kg, cos, sa, sb, bd):
    t = x.shape[0]
    seq_tiles = cos.shape[0] // ROW_TILE
    row = lambda i: (i, 0)
    fix = lambda i: (0, 0)
    tab = lambda i: (i % seq_tiles, 0)
    out = jax.ShapeDtypeStruct((t, ATT_WIDTH), BF16)
    return pl.pallas_call(
        _in_kernel,
        grid=(t // ROW_TILE,),
        in_specs=[
            pl.BlockSpec((ROW_TILE, D_MODEL), row),
            pl.BlockSpec((1, D_MODEL), fix),
            pl.BlockSpec((None, D_MODEL, IN_WIDTH), lambda i: (layer, 0, 0)),
            pl.BlockSpec((1, LANES), fix),
            pl.BlockSpec((1, LANES), fix),
            pl.BlockSpec((ROW_TILE, LANES), tab),
            pl.BlockSpec((ROW_TILE, LANES), tab),
            pl.BlockSpec((ROW_TILE, LANES), tab),
            pl.BlockSpec((LANES, LANES), fix),
        ],
        out_specs=[pl.BlockSpec((ROW_TILE, ATT_WIDTH), row)] * 4,
        out_shape=[out] * 4,
        compiler_params=_params("parallel"),
        name="in_proj",
    )(x, g1, w, qg, kg, cos, sa, sb, bd)


def _attn_bias(q_off, k_idx, with_prev):
    ok = (k_idx >= q_off) & (k_idx <= q_off + ATT_BLK)
    if not with_prev:
        ok = ok & (k_idx >= ATT_BLK)
    return jnp.where(ok, 0.0, NEG_INF).astype(F32)


def _attn_kernel(q_ref, k_ref, v_ref, o_ref, q32, k32, v32, m_s, l_s, acc_s, bias_s):
    n_per = q32.shape[1]
    pad = ATT_BLK
    zeros = jnp.zeros((pad, LANES), F32)
    for r in range(RES):
        rows = pl.ds(r * n_per, n_per)
        q32[r] = q_ref[rows, :].astype(F32)
        k32[r, pl.ds(0, pad), :] = zeros
        v32[r, pl.ds(0, pad), :] = zeros
        k32[r, pl.ds(pad, n_per), :] = k_ref[rows, :].astype(F32)
        v32[r, pl.ds(pad, n_per), :] = v_ref[rows, :].astype(F32)

    qi = lax.broadcasted_iota(I32, (2 * ATT_BLK, 2 * ATT_BLK), 0) & (ATT_BLK - 1)
    kc = lax.broadcasted_iota(I32, (2 * ATT_BLK, 2 * ATT_BLK), 1)
    offs = (
        (16 * (qi & 7) + (qi >> 3), 16 * (kc & 15) + (kc >> 4)),
        (4 * (qi & 31) + (qi >> 5), 4 * (kc & 63) + (kc >> 6)),
        (qi, kc),
    )
    for br, (qo, ko) in enumerate(offs):
        bias_s[br, 0] = _attn_bias(qo, ko, False)
        bias_s[br, 1] = _attn_bias(qo, ko, True)

    head_a = lax.broadcasted_iota(I32, (ATT_BLK, LANES), 1) < HEAD_DIM

    def block(qb, ks, vs, bias):
        qa = jnp.where(head_a, qb, 0.0)
        qq = jnp.concatenate([qa, qb - qa], axis=0).astype(BF16)
        s = _dot_nt(qq, ks.astype(BF16)) + bias
        m = jnp.max(s, axis=1, keepdims=True)
        p = jnp.exp2(s - m)
        l = jnp.sum(p, axis=1, keepdims=True)
        pv = _dot(p.astype(BF16), vs.astype(BF16))
        m2 = jnp.where(head_a, m[:ATT_BLK], m[ATT_BLK:])
        l2 = jnp.where(head_a, l[:ATT_BLK], l[ATT_BLK:])
        pv2 = jnp.where(head_a, pv[:ATT_BLK], pv[ATT_BLK:])
        return m2, l2, pv2

    def store(br, slab, rows, triple, shape=None):
        for ref, val in zip((m_s, l_s, acc_s), triple):
            ref[br, slab, rows, :] = val if shape is None else val.reshape(shape)

    slabs16 = ATT_STEP_BLOCKS // (n_per // ATT_BLK)

    def body16(rr, carry):
        for i in range(slabs16):
            r = slabs16 * rr + i
            for c in range(n_per // ATT_BLK):
                rows = pl.ds(c * ATT_BLK, ATT_BLK)
                keys = pl.ds(c * ATT_BLK, 2 * ATT_BLK)
                store(2, r, rows, block(q32[r, rows, :], k32[r, keys, :], v32[r, keys, :], bias_s[2, min(c, 1)]))
        return carry

    lax.fori_loop(0, RES // slabs16, body16, 0)

    sub = ATT_BLK // 4
    per4 = ATT_STEP_BLOCKS // 4

    def body4(cc, carry):
        for r4 in range(4):
            slabs = [r4 + 4 * m for m in range(4)]
            for c in range(per4):
                first = pl.multiple_of((cc * per4 + c) * sub, sub)
                rows = pl.ds(first, sub)
                keys = pl.ds(first + pad - sub, 2 * sub)
                cat = lambda ref, idx: jnp.concatenate([ref[s, idx, :] for s in slabs], axis=0)
                bias = bias_s[1, jnp.minimum(cc, 1)] if c == 0 else bias_s[1, 1]
                triple = block(cat(q32, rows), cat(k32, keys), cat(v32, keys), bias)
                for j, s in enumerate(slabs):
                    store(1, s, rows, [x[j * sub:(j + 1) * sub] for x in triple])
        return carry

    lax.fori_loop(0, n_per // (per4 * sub), body4, 0)

    sub1 = ATT_BLK // RES

    def body1(jj, carry):
        for g in range(ATT_STEP_BLOCKS):
            first = pl.multiple_of((jj * ATT_STEP_BLOCKS + g) * sub1, sub1)
            rows = pl.ds(first, sub1)
            keys = pl.ds(first + pad - sub1, 2 * sub1)
            bias = bias_s[0, jnp.minimum(jj, 1)] if g == 0 else bias_s[0, 1]
            triple = block(q32[:, rows, :].reshape(ATT_BLK, LANES), k32[:, keys, :].reshape(2 * ATT_BLK, LANES),
                           v32[:, keys, :].reshape(2 * ATT_BLK, LANES), bias)
            store(0, slice(None), rows, triple, (RES, sub1, LANES))
        return carry

    lax.fori_loop(0, n_per // (ATT_STEP_BLOCKS * sub1), body1, 0)

    for r in range(RES):
        ms = [m_s[br, r] for br in range(3)]
        top = jnp.maximum(jnp.maximum(ms[0], ms[1]), ms[2])
        ws = [jnp.exp2(m - top) for m in ms]
        num = sum(w * acc_s[br, r] for br, w in enumerate(ws))
        den = sum(w * l_s[br, r] for br, w in enumerate(ws))
        o_ref[pl.ds(r * n_per, n_per), :] = (num / den).astype(BF16)


def _attn_call(q, k, v, seq):
    t = q.shape[0]
    n_per = seq // RES
    spec = pl.BlockSpec((seq, LANES), lambda b, h: (b, h))
    staged = pltpu.VMEM((RES, n_per, LANES), F32)
    state = pltpu.VMEM((3, RES, n_per, LANES), F32)
    padded = pltpu.VMEM((RES, n_per + ATT_BLK, LANES), F32)
    return pl.pallas_call(
        _attn_kernel,
        grid=(t // seq, ATT_WIDTH // LANES),
        in_specs=[spec, spec, spec],
        out_specs=spec,
        out_shape=jax.ShapeDtypeStruct((t, ATT_WIDTH), BF16),
        scratch_shapes=[staged, padded, padded, state, state, state,
                        pltpu.VMEM((3, 2, 2 * ATT_BLK, 2 * ATT_BLK), F32)],
        compiler_params=_params("parallel", "parallel"),
        name="dilated_attn",
    )(q, k, v)


def _row_min_index(cond, lane_f):
    return jnp.min(jnp.where(cond, lane_f, float(LANES)), axis=1, keepdims=True)


def _out_kernel(att_ref, u_ref, uh_ref, x_ref, wp_ref, ps_ref, wo_ref, g2_ref, rh_ref, rl_ref, br_ref,
                xo_ref, he_ref):
    i = pl.program_id(1)
    nb = u_ref.shape[2]
    rows = RES * nb
    u = u_ref[0].astype(F32)
    halo = jnp.where(i > 0, uh_ref[0][:, -1:, :].astype(F32), 0.0)
    u_prev = jnp.concatenate([halo, u[:, :nb - 1, :]], axis=1)

    n_idx = lax.broadcasted_iota(I32, (RES, nb, POOL_GROUP_DIM), 1) + i * nb
    r_idx = lax.broadcasted_iota(I32, (RES, nb, POOL_GROUP_DIM), 0)
    p1 = (RES * n_idx + r_idx + 1).astype(F32)

    pools = []
    for g, w in enumerate(POOL_WINDOWS):
        lanes = slice(g * POOL_GROUP_DIM, (g + 1) * POOL_GROUP_DIM)
        ug = u[:, :, lanes]
        upg = u_prev[:, :, lanes]
        tot = ug
        for j in range(1, w):
            tot = tot + jnp.concatenate([upg[RES - j:], ug[:RES - j]], axis=0)
        rg = tot / jnp.minimum(p1, float(w)) - ug
        y = _dot(rg.reshape(rows, POOL_GROUP_DIM).astype(BF16), wp_ref[g])
        pools.append((y * ps_ref[:, lanes]).astype(BF16))
    mix = jnp.concatenate([att_ref[0].reshape(rows, ATT_WIDTH)] + pools, axis=1)
    x = x_ref[0].reshape(rows, D_MODEL) + _dot(mix, wo_ref[...])
    xo_ref[0] = x.reshape(RES, nb, D_MODEL)

    ms = jnp.mean(x * x, axis=-1, keepdims=True)
    h = x * lax.rsqrt(ms + RMS_EPS) * g2_ref[...]
    he_ref[0, :, :, :D_MODEL] = h.reshape(RES, nb, D_MODEL)

    hh = h.astype(BF16)
    hl = (h - hh.astype(F32)).astype(BF16)
    logits = _dot(hh, rh_ref[...]) + (_dot(hl, rh_ref[...]) + _dot(hh, rl_ref[...])) + br_ref[...]
    lane = lax.broadcasted_iota(I32, (rows, LANES), 1)
    lane_f = lane.astype(F32)
    is_g = lane < N_EXPERT_GROUPS
    gl = jnp.where(is_g, logits, -jnp.inf)
    gm = jnp.max(gl, axis=1, keepdims=True)
    g_idx = _row_min_index(is_g & (gl == gm), lane_f)
    p_top = 1.0 / jnp.sum(jnp.where(is_g, jnp.exp(logits - gm), 0.0), axis=1, keepdims=True)
    e_lane = lane - N_EXPERT_GROUPS
    in_grp = (e_lane >= 0) & (e_lane < N_EXPERT_GROUPS * EXPERTS_PER_GROUP) & \
             ((e_lane >> 2).astype(F32) == g_idx)
    el = jnp.where(in_grp, logits, -jnp.inf)
    v1 = jnp.max(el, axis=1, keepdims=True)
    i1 = _row_min_index(in_grp & (el == v1), lane_f)
    rest = in_grp & (lane_f != i1)
    el2 = jnp.where(rest, logits, -jnp.inf)
    v2 = jnp.max(el2, axis=1, keepdims=True)
    i2 = _row_min_index(rest & (el2 == v2), lane_f)
    e21 = jnp.exp(v2 - v1)
    w1 = p_top / (1.0 + e21)
    w2 = p_top * e21 / (1.0 + e21)
    a1 = i1 - N_EXPERT_GROUPS - EXPERTS_PER_GROUP * g_idx
    a2 = i2 - N_EXPERT_GROUPS - EXPERTS_PER_GROUP * g_idx
    first_low = a1 < a2
    lo = jnp.where(first_low, a1, a2)
    hi = jnp.where(first_low, a2, a1)
    w_lo = jnp.where(first_low, w1, w2)
    w_hi = jnp.where(first_low, w2, w1)
    bucket = g_idx * PAIRS_PER_GROUP + lo * 3.0 - lo * (lo - 1.0) * 0.5 + hi - lo - 1.0
    meta = jnp.where(lane == 0, w_lo, jnp.where(lane == 1, w_hi, jnp.where(lane == 2, bucket, 0.0)))
    he_ref[0, :, :, D_MODEL:] = meta.reshape(RES, nb, META)


def _out_call(att, u, x, wp, ps, wo, layer, g2, rh, rl, br, seq):
    t = x.shape[0]
    b = t // seq
    n_per = seq // RES
    nb = ROW_TILE // RES
    halo_rows = 16
    v4 = lambda a: a.reshape(b, RES, n_per, a.shape[-1])
    tile = lambda bi, i: (bi, 0, i, 0)
    halo = lambda bi, i: (bi, 0, jnp.maximum(i * (nb // halo_rows) - 1, 0), 0)
    fix2 = lambda bi, i: (0, 0)
    xo, he = pl.pallas_call(
        _out_kernel,
        grid=(b, n_per // nb),
        in_specs=[
            pl.BlockSpec((1, RES, nb, ATT_WIDTH), tile),
            pl.BlockSpec((1, RES, nb, POOL_WIDTH), tile),
            pl.BlockSpec((1, RES, halo_rows, POOL_WIDTH), halo),
            pl.BlockSpec((1, RES, nb, D_MODEL), tile),
            pl.BlockSpec((None, POOL_GROUPS, POOL_GROUP_DIM, POOL_GROUP_DIM), lambda bi, i: (layer, 0, 0, 0)),
            pl.BlockSpec((1, POOL_WIDTH), fix2),
            pl.BlockSpec((None, D_MODEL, D_MODEL), lambda bi, i: (layer, 0, 0)),
            pl.BlockSpec((1, D_MODEL), fix2),
            pl.BlockSpec((D_MODEL, LANES), fix2),
            pl.BlockSpec((D_MODEL, LANES), fix2),
            pl.BlockSpec((1, LANES), fix2),
        ],
        out_specs=[pl.BlockSpec((1, RES, nb, D_MODEL), tile), pl.BlockSpec((1, RES, nb, ROW_EXT), tile)],
        out_shape=[jax.ShapeDtypeStruct((b, RES, n_per, D_MODEL), F32),
                   jax.ShapeDtypeStruct((b, RES, n_per, ROW_EXT), F32)],
        compiler_params=_params("parallel", "parallel"),
        name="out_proj_router",
    )(v4(att), v4(u), v4(u), v4(x), wp, ps, wo, g2, rh, rl, br)
    return xo.reshape(t, D_MODEL), he.reshape(t, ROW_EXT)


def _sort_kernel(meta_ref, pos_ref, tinfo_ref, cnt_s, off_s):
    phase = pl.program_id(0)
    i = pl.program_id(1)
    rows = meta_ref.shape[0]
    lane = lax.broadcasted_iota(I32, (rows, LANES), 1)
    onehot = lane.astype(F32) == meta_ref[:, 2:3]
    oh = onehot.astype(F32)
    tile_count = jnp.sum(oh, axis=0, keepdims=True)

    @pl.when((phase == 0) & (i == 0))
    def _():
        cnt_s[...] = jnp.zeros_like(cnt_s)

    @pl.when(phase == 0)
    def _():
        cnt_s[...] += tile_count

    @pl.when((phase == 1) & (i == 0))
    def _():
        tiles = jnp.floor((cnt_s[...] + (MOE_TILE - 1.0)) * (1.0 / MOE_TILE))
        tiles8 = jnp.broadcast_to(tiles, (SUBLANES, LANES)).astype(BF16)
        sq = (LANES, LANES)
        before = lax.broadcasted_iota(I32, sq, 0) < lax.broadcasted_iota(I32, sq, 1)
        start = _dot(tiles8, before.astype(BF16))
        off_s[...] = start[0:1] * float(MOE_TILE)
        cnt_s[...] = jnp.zeros_like(cnt_s)
        end = (start + tiles8.astype(F32)).astype(BF16)
        eye = (lax.broadcasted_iota(I32, sq, 0) == lax.broadcasted_iota(I32, sq, 1)).astype(BF16)
        end_col = _dot_nt(eye, end)[:, 0:1]
        tile_id = lax.broadcasted_iota(I32, sq, 1).astype(F32)
        tile_bucket = jnp.sum((end_col <= tile_id).astype(F32), axis=0, keepdims=True)
        total = jnp.max(end.astype(F32)[0:1], axis=1, keepdims=True)
        row_lane = lax.broadcasted_iota(I32, (1, LANES), 1)
        ends = pltpu.roll(end.astype(F32)[0:1], END_LANE, 1)
        is_end = (row_lane >= END_LANE) & (row_lane < END_LANE + N_BUCKETS)
        tinfo_ref[...] = jnp.where(row_lane == LANES - 1, total, jnp.where(is_end, ends, tile_bucket)).astype(I32)

    @pl.when(phase == 1)
    def _():
        sq = (rows, rows)
        upto = lax.broadcasted_iota(I32, sq, 1) <= lax.broadcasted_iota(I32, sq, 0)
        prefix = _dot(upto.astype(BF16), oh.astype(BF16))
        posv = jnp.where(onehot, prefix - 1.0 + cnt_s[...] + off_s[...], 0.0)
        hi = jnp.floor(posv * (1.0 / 256.0))
        lo = posv - hi * 256.0
        ones = jnp.ones((SUBLANES, LANES), BF16)
        pos = _dot_nt(ones, hi.astype(BF16)) * 256.0 + _dot_nt(ones, lo.astype(BF16))
        pos_ref[0] = pos[0:1].astype(I32)
        cnt_s[...] += tile_count


def _sort_call(he):
    t = he.shape[0]
    n_tiles = t // ROW_TILE
    return pl.pallas_call(
        _sort_kernel,
        grid=(2, n_tiles),
        in_specs=[pl.BlockSpec((ROW_TILE, META), lambda p, i: (i, D_MODEL // META))],
        out_specs=[pl.BlockSpec((1, 1, ROW_TILE), lambda p, i: (i * p, 0, 0)),
                   pl.BlockSpec((1, LANES), lambda p, i: (0, 0))],
        out_shape=[jax.ShapeDtypeStruct((n_tiles, 1, ROW_TILE), I32),
                   jax.ShapeDtypeStruct((1, LANES), I32)],
        scratch_shapes=[pltpu.VMEM((1, LANES), F32), pltpu.VMEM((1, LANES), F32)],
        compiler_params=_params("arbitrary", "arbitrary"),
        name="bucket_sort",
    )(he)


def _scatter_kernel(tinfo_ref, pos_ref, he_ref, hs_ref, zeros_ref, sem, zsem):
    rows = he_ref.shape[0]

    @pl.when(pl.program_id(0) == 0)
    def _():
        zeros_ref[...] = jnp.zeros_like(zeros_ref)

        def fill(b):
            end = tinfo_ref[0, END_LANE + b]
            begin = tinfo_ref[0, END_LANE + b - 1] if b else 0
            first = pl.multiple_of((end - 1) * MOE_TILE, MOE_TILE)
            return end > begin, pltpu.make_async_copy(zeros_ref, hs_ref.at[pl.ds(first, MOE_TILE), :], zsem)

        def unused(j):
            return pltpu.make_async_copy(zeros_ref, hs_ref.at[pl.ds(pl.multiple_of(j * MOE_TILE, MOE_TILE), MOE_TILE), :], zsem)

        tiles = hs_ref.shape[0] // MOE_TILE
        used = tinfo_ref[0, LANES - 1]
        for b in range(N_BUCKETS):
            nonempty, copy = fill(b)
            pl.when(nonempty)(copy.start)
        lax.fori_loop(used, tiles, lambda j, c: (unused(j).start(), c)[1], 0)
        for b in range(N_BUCKETS):
            nonempty, copy = fill(b)
            pl.when(nonempty)(copy.wait)
        lax.fori_loop(used, tiles, lambda j, c: (unused(j).wait(), c)[1], 0)

    def row_copy(t):
        return pltpu.make_async_copy(he_ref.at[pl.ds(t, 1), :], hs_ref.at[pl.ds(pos_ref[0, 0, t], 1), :], sem)

    def start(t, c):
        row_copy(t).start()
        return c

    lax.fori_loop(0, rows, start, 0, unroll=ISSUE_UNROLL)
    pltpu.make_async_copy(he_ref, hs_ref.at[pl.ds(0, rows), :], sem).wait()


def _scatter_call(tinfo, pos, he, sorted_rows):
    t = he.shape[0]
    return pl.pallas_call(
        _scatter_kernel,
        grid=(t // ROW_TILE,),
        in_specs=[pl.BlockSpec(memory_space=pltpu.SMEM),
                  pl.BlockSpec((1, 1, ROW_TILE), lambda i: (i, 0, 0), memory_space=pltpu.SMEM),
                  pl.BlockSpec((ROW_TILE, ROW_EXT), lambda i: (i, 0))],
        out_specs=pl.BlockSpec(memory_space=pl.ANY),
        out_shape=jax.ShapeDtypeStruct((sorted_rows, ROW_EXT), F32),
        scratch_shapes=[pltpu.VMEM((MOE_TILE, ROW_EXT), F32), pltpu.SemaphoreType.DMA(()),
                        pltpu.SemaphoreType.DMA(())],
        compiler_params=_params("arbitrary"),
        name="row_scatter",
    )(tinfo, pos, he)


def _gather_kernel(pos_ref, x_ref, ys_ref, o_ref, buf, sem):
    rows = x_ref.shape[0]

    def row_copy(t):
        return pltpu.make_async_copy(ys_ref.at[pl.ds(pos_ref[0, 0, t], 1), :], buf.at[pl.ds(t, 1), :], sem)

    def start(t, c):
        row_copy(t).start()
        return c

    lax.fori_loop(0, rows, start, 0, unroll=ISSUE_UNROLL)
    pltpu.make_async_copy(ys_ref.at[pl.ds(0, rows), :], buf, sem).wait()
    o_ref[...] = x_ref[...] + buf[...]


def _gather_call(pos, x, ys):
    t = x.shape[0]
    return pl.pallas_call(
        _gather_kernel,
        grid=(t // ROW_TILE,),
        in_specs=[pl.BlockSpec((1, 1, ROW_TILE), lambda i: (i, 0, 0), memory_space=pltpu.SMEM),
                  pl.BlockSpec((ROW_TILE, D_MODEL), lambda i: (i, 0)),
                  pl.BlockSpec(memory_space=pl.ANY)],
        out_specs=pl.BlockSpec((ROW_TILE, D_MODEL), lambda i: (i, 0)),
        out_shape=jax.ShapeDtypeStruct((t, D_MODEL), F32),
        scratch_shapes=[pltpu.VMEM((ROW_TILE, D_MODEL), F32), pltpu.SemaphoreType.DMA(())],
        compiler_params=_params("arbitrary"),
        name="row_gather_residual",
    )(pos, x, ys)


def _tile_group(j, tinfo):
    used = tinfo[LANES - 1]
    return tinfo[jnp.minimum(j, used - 1)] // PAIRS_PER_GROUP


def _moe_kernel(tinfo, hs_ref, wg_ref, wu_ref, wd_ref, ys_ref):
    j = pl.program_id(0)
    used = tinfo[LANES - 1]

    @pl.when(j < used)
    def _():
        pair = tinfo[j] % PAIRS_PER_GROUP
        e_lo = (pair >= 3).astype(I32) + (pair >= 5).astype(I32)
        e_hi = pair - (e_lo * 3 - (e_lo * (e_lo - 1)) // 2) + e_lo + 1
        xt = hs_ref[:, :D_MODEL].astype(BF16)

        def expert(e, gate):
            hg = _dot(xt, wg_ref[e])
            hu = _dot(xt, wu_ref[e])
            act = hg * (1.0 / (1.0 + jnp.exp(-hg))) * hu * gate
            return _dot(act.astype(BF16), wd_ref[e])

        ys_ref[...] = expert(e_lo, hs_ref[:, D_MODEL:D_MODEL + 1]) + \
            expert(e_hi, hs_ref[:, D_MODEL + 1:D_MODEL + 2])

    @pl.when(j >= used)
    def _():
        ys_ref[...] = jnp.zeros_like(ys_ref)


def _moe_call(tinfo, hs, wg, wu, wd, layer):
    sorted_rows = hs.shape[0]
    row = lambda j, ti: (jnp.minimum(j, ti[LANES - 1] - 1), 0)
    grp = lambda j, ti: (layer, _tile_group(j, ti), 0, 0, 0)
    return pl.pallas_call(
        _moe_kernel,
        grid_spec=pltpu.PrefetchScalarGridSpec(
            num_scalar_prefetch=1,
            grid=(sorted_rows // MOE_TILE,),
            in_specs=[pl.BlockSpec((MOE_TILE, ROW_EXT), row),
                      pl.BlockSpec((None, None, EXPERTS_PER_GROUP, D_MODEL, D_EXPERT), grp),
                      pl.BlockSpec((None, None, EXPERTS_PER_GROUP, D_MODEL, D_EXPERT), grp),
                      pl.BlockSpec((None, None, EXPERTS_PER_GROUP, D_EXPERT, D_MODEL), grp)],
            out_specs=pl.BlockSpec((MOE_TILE, D_MODEL), lambda j, ti: (j, 0)),
        ),
        out_shape=jax.ShapeDtypeStruct((sorted_rows, D_MODEL), F32),
        compiler_params=_params("arbitrary"),
        name="grouped_experts",
    )(tinfo, hs, wg, wu, wd)


def _rope_tables(seq):
    half = ROT_DIM // 2
    inv_freq = ROPE_THETA ** (-jnp.arange(0, ROT_DIM, 2, dtype=F32) / ROT_DIM)
    row = jnp.arange(seq)
    pos = (RES * (row % (seq // RES)) + row // (seq // RES)).astype(F32)
    ang = pos[:, None] * inv_freq[None, :]
    cos, sin = jnp.cos(ang), jnp.sin(ang)
    d = jnp.arange(LANES) % HEAD_DIM
    cos_l = jnp.where(d[None, :] < ROT_DIM, cos[:, d % half], 1.0)
    sin_l = sin[:, d % half]
    sa = jnp.where(d[None, :] < half, -sin_l, 0.0)
    sb = jnp.where((d[None, :] >= half) & (d[None, :] < ROT_DIM), sin_l, 0.0)
    return cos_l, sa, sb


def kernel(x, norm1_gain, w_in, q_norm_gain, k_norm_gain, w_pool, pool_scale, w_out, norm2_gain, w_group, b_group, w_router, b_router, w_gate, w_up, w_down):
    b, seq, d = x.shape
    depth = w_in.shape[0]
    t = b * seq
    n_per = seq // RES
    assert d == D_MODEL and seq % (RES * ATT_BLK) == 0 and t % ROW_TILE == 0
    sorted_rows = t + N_BUCKETS * MOE_TILE

    cos, sa, sb = _rope_tables(seq)
    lane_head = jnp.arange(LANES) // HEAD_DIM
    block_diag = (lane_head[:, None] == lane_head[None, :]).astype(BF16)

    w_in_b = w_in.astype(BF16)
    w_out_b = w_out.astype(BF16)
    w_pool_b = w_pool.astype(BF16)
    grouped = lambda w: w.astype(BF16).reshape(depth, N_EXPERT_GROUPS, EXPERTS_PER_GROUP, *w.shape[2:])
    w_gate_b, w_up_b, w_down_b = grouped(w_gate), grouped(w_up), grouped(w_down)
    n_logits = N_EXPERT_GROUPS * (1 + EXPERTS_PER_GROUP)
    w_r = jnp.pad(jnp.concatenate([w_group, w_router], axis=-1), ((0, 0), (0, 0), (0, LANES - n_logits)))
    w_r_hi = w_r.astype(BF16)
    w_r_lo = (w_r - w_r_hi.astype(F32)).astype(BF16)
    b_r = jnp.pad(jnp.concatenate([b_group, b_router], axis=-1), ((0, 0), (0, LANES - n_logits)))
    two_heads = lambda g: jnp.tile(g, (1, LANES // HEAD_DIM))

    xr = x.reshape(b, n_per, RES, d).transpose(0, 2, 1, 3).reshape(t, d)
    for l in range(depth):
        q, k, v, u = _in_call(xr, norm1_gain[l:l + 1], w_in_b, l, two_heads(q_norm_gain[l:l + 1]),
                              two_heads(k_norm_gain[l:l + 1]), cos, sa, sb, block_diag)
        att = _attn_call(q, k, v, seq)
        xr, he = _out_call(att, u, xr, w_pool_b, pool_scale[l:l + 1], w_out_b, l, norm2_gain[l:l + 1],
                           w_r_hi[l], w_r_lo[l], b_r[l:l + 1], seq)
        pos, tinfo = _sort_call(he)
        hs = _scatter_call(tinfo, pos, he, sorted_rows)
        ys = _moe_call(tinfo.reshape(LANES), hs, w_gate_b, w_up_b, w_down_b, l)
        xr = _gather_call(pos, xr, ys)
    return xr.reshape(b, RES, n_per, d).transpose(0, 2, 1, 3).reshape(b, seq, d)
```

```python
import functools

import jax
import jax.numpy as jnp
from jax import lax
from jax.experimental import pallas as pl
from jax.experimental.pallas import tpu as pltpu

D_MODEL = 1024
N_HEADS = 8
HEAD_DIM = 64
ATT_WIDTH = N_HEADS * HEAD_DIM
POOL_GROUPS = 4
POOL_GROUP_DIM = 128
POOL_WIDTH = POOL_GROUPS * POOL_GROUP_DIM
POOL_WINDOWS = (2, 4, 8, 16)
IN_WIDTH = 3 * ATT_WIDTH + POOL_WIDTH
ROT_DIM = 16
ROPE_THETA = 500000.0
N_EXPERT_GROUPS = 4
EXPERTS_PER_GROUP = 4
D_EXPERT = 256
RMS_EPS = 1e-6
NEG_INF = -1e30
LOG2_E = 1.4426950408889634

LANES = 128
SUBLANES = 8
RES = 16
ATT_BLK = 128
ATT_STEP_BLOCKS = 32
PAIRS_PER_GROUP = 6
N_BUCKETS = N_EXPERT_GROUPS * PAIRS_PER_GROUP
ROW_TILE = 512
MOE_TILE = 256
ISSUE_UNROLL = 8
ROW_PITCH = SUBLANES + 1
END_LANE = 96
VMEM_LIMIT = 56 * 1024 * 1024

F32 = jnp.float32
BF16 = jnp.bfloat16
I32 = jnp.int32


def _dot(a, b):
    return jnp.dot(a, b, preferred_element_type=F32)


def _dot_nt(a, b):
    return lax.dot_general(a, b, (((1,), (1,)), ((), ())), preferred_element_type=F32)


def _row_tile(ref, row, n=1):
    return ref.at[pl.ds(row * ROW_PITCH, n * ROW_PITCH), :]


def _row_part(n, s):
    return pl.ds(s, n, stride=ROW_PITCH)


def _params(*sem):
    return pltpu.CompilerParams(dimension_semantics=sem, vmem_limit_bytes=VMEM_LIMIT)


def _in_kernel(x_ref, g1_ref, w_ref, qg_ref, kg_ref, cos_ref, sa_ref, sb_ref, bd_ref,
               q_ref, k_ref, v_ref, u_ref):
    x = x_ref[...]
    ms = jnp.mean(x * x, axis=-1, keepdims=True)
    h = (x * lax.rsqrt(ms + RMS_EPS) * g1_ref[...]).astype(BF16)
    cos = cos_ref[...]
    sa = sa_ref[...]
    sb = sb_ref[...]
    bd = bd_ref[...]

    def qk(col0, gain, out_ref, scale):
        z = _dot(h, w_ref[:, col0:col0 + ATT_WIDTH])
        for c in range(ATT_WIDTH // LANES):
            zc = z[:, c * LANES:(c + 1) * LANES]
            zz = zc * zc
            hi = zz.astype(BF16)
            lo = (zz - hi.astype(F32)).astype(BF16)
            ssq = _dot(hi, bd) + _dot(lo, bd)
            y = zc * lax.rsqrt(ssq * (1.0 / HEAD_DIM) + RMS_EPS) * gain
            rot = y * cos + pltpu.roll(y, LANES - ROT_DIM // 2, 1) * sa + pltpu.roll(y, ROT_DIM // 2, 1) * sb
            out_ref[:, c * LANES:(c + 1) * LANES] = (rot * scale).astype(BF16)

    qk(0, qg_ref[...], q_ref, HEAD_DIM ** -0.5 * LOG2_E)
    qk(ATT_WIDTH, kg_ref[...], k_ref, 1.0)
    v_ref[...] = _dot(h, w_ref[:, 2 * ATT_WIDTH:3 * ATT_WIDTH]).astype(BF16)
    u_ref[...] = _dot(h, w_ref[:, 3 * ATT_WIDTH:]).astype(BF16)


def _in_call(x, g1, w, layer, qg, kg, cos, sa, sb, bd):
    t = x.shape[0]
    seq_tiles = cos.shape[0] // ROW_TILE
    row = lambda i: (i, 0)
    fix = lambda i: (0, 0)
    tab = lambda i: (i % seq_tiles, 0)
    out = jax.ShapeDtypeStruct((t, ATT_WIDTH), BF16)
    return pl.pallas_call(
        _in_kernel,
        grid=(t // ROW_TILE,),
        in_specs=[
            pl.BlockSpec((ROW_TILE, D_MODEL), row),
            pl.BlockSpec((1, D_MODEL), fix),
            pl.BlockSpec((None, D_MODEL, IN_WIDTH), lambda i: (layer, 0, 0)),
            pl.BlockSpec((1, LANES), fix),
            pl.BlockSpec((1, LANES), fix),
            pl.BlockSpec((ROW_TILE, LANES), tab),
            pl.BlockSpec((ROW_TILE, LANES), tab),
            pl.BlockSpec((ROW_TILE, LANES), tab),
            pl.BlockSpec((LANES, LANES), fix),
        ],
        out_specs=[pl.BlockSpec((ROW_TILE, ATT_WIDTH), row)] * 4,
        out_shape=[out] * 4,
        compiler_params=_params("parallel"),
        name="in_proj",
    )(x, g1, w, qg, kg, cos, sa, sb, bd)


def _attn_bias(q_off, k_idx, with_prev):
    ok = (k_idx >= q_off) & (k_idx <= q_off + ATT_BLK)
    if not with_prev:
        ok = ok & (k_idx >= ATT_BLK)
    return jnp.where(ok, 0.0, NEG_INF).astype(F32)


def _attn_kernel(q_ref, k_ref, v_ref, o_ref, q32, k32, v32, m_s, l_s, acc_s, bias_s):
    n_per = q32.shape[1]
    pad = ATT_BLK
    zeros = jnp.zeros((pad, LANES), F32)
    for r in range(RES):
        rows = pl.ds(r * n_per, n_per)
        q32[r] = q_ref[rows, :].astype(F32)
        k32[r, pl.ds(0, pad), :] = zeros
        v32[r, pl.ds(0, pad), :] = zeros
        k32[r, pl.ds(pad, n_per), :] = k_ref[rows, :].astype(F32)
        v32[r, pl.ds(pad, n_per), :] = v_ref[rows, :].astype(F32)

    qi = lax.broadcasted_iota(I32, (2 * ATT_BLK, 2 * ATT_BLK), 0) & (ATT_BLK - 1)
    kc = lax.broadcasted_iota(I32, (2 * ATT_BLK, 2 * ATT_BLK), 1)
    offs = (
        (16 * (qi & 7) + (qi >> 3), 16 * (kc & 15) + (kc >> 4)),
        (4 * (qi & 31) + (qi >> 5), 4 * (kc & 63) + (kc >> 6)),
        (qi, kc),
    )
    for br, (qo, ko) in enumerate(offs):
        bias_s[br, 0] = _attn_bias(qo, ko, False)
        bias_s[br, 1] = _attn_bias(qo, ko, True)

    head_a = lax.broadcasted_iota(I32, (ATT_BLK, LANES), 1) < HEAD_DIM

    def block(qb, ks, vs, bias):
        qa = jnp.where(head_a, qb, 0.0)
        qq = jnp.concatenate([qa, qb - qa], axis=0).astype(BF16)
        s = _dot_nt(qq, ks.astype(BF16)) + bias
        m = jnp.max(s, axis=1, keepdims=True)
        p = jnp.exp2(s - m)
        l = jnp.sum(p, axis=1, keepdims=True)
        pv = _dot(p.astype(BF16), vs.astype(BF16))
        m2 = jnp.where(head_a, m[:ATT_BLK], m[ATT_BLK:])
        l2 = jnp.where(head_a, l[:ATT_BLK], l[ATT_BLK:])
        pv2 = jnp.where(head_a, pv[:ATT_BLK], pv[ATT_BLK:])
        return m2, l2, pv2

    def store(br, slab, rows, triple, shape=None):
        for ref, val in zip((m_s, l_s, acc_s), triple):
            ref[br, slab, rows, :] = val if shape is None else val.reshape(shape)

    slabs16 = ATT_STEP_BLOCKS // (n_per // ATT_BLK)

    def body16(rr, carry):
        for i in range(slabs16):
            r = slabs16 * rr + i
            for c in range(n_per // ATT_BLK):
                rows = pl.ds(c * ATT_BLK, ATT_BLK)
                keys = pl.ds(c * ATT_BLK, 2 * ATT_BLK)
                store(2, r, rows, block(q32[r, rows, :], k32[r, keys, :], v32[r, keys, :], bias_s[2, min(c, 1)]))
        return carry

    lax.fori_loop(0, RES // slabs16, body16, 0)

    sub = ATT_BLK // 4
    per4 = ATT_STEP_BLOCKS // 4

    def body4(cc, carry):
        for r4 in range(4):
            slabs = [r4 + 4 * m for m in range(4)]
            for c in range(per4):
                first = pl.multiple_of((cc * per4 + c) * sub, sub)
                rows = pl.ds(first, sub)
                keys = pl.ds(first + pad - sub, 2 * sub)
                cat = lambda ref, idx: jnp.concatenate([ref[s, idx, :] for s in slabs], axis=0)
                bias = bias_s[1, jnp.minimum(cc, 1)] if c == 0 else bias_s[1, 1]
                triple = block(cat(q32, rows), cat(k32, keys), cat(v32, keys), bias)
                for j, s in enumerate(slabs):
                    store(1, s, rows, [x[j * sub:(j + 1) * sub] for x in triple])
        return carry

    lax.fori_loop(0, n_per // (per4 * sub), body4, 0)

    sub1 = ATT_BLK // RES

    def body1(jj, carry):
        for g in range(ATT_STEP_BLOCKS):
            first = pl.multiple_of((jj * ATT_STEP_BLOCKS + g) * sub1, sub1)
            rows = pl.ds(first, sub1)
            keys = pl.ds(first + pad - sub1, 2 * sub1)
            bias = bias_s[0, jnp.minimum(jj, 1)] if g == 0 else bias_s[0, 1]
            triple = block(q32[:, rows, :].reshape(ATT_BLK, LANES), k32[:, keys, :].reshape(2 * ATT_BLK, LANES),
                           v32[:, keys, :].reshape(2 * ATT_BLK, LANES), bias)
            store(0, slice(None), rows, triple, (RES, sub1, LANES))
        return carry

    lax.fori_loop(0, n_per // (ATT_STEP_BLOCKS * sub1), body1, 0)

    for r in range(RES):
        ms = [m_s[br, r] for br in range(3)]
        top = jnp.maximum(jnp.maximum(ms[0], ms[1]), ms[2])
        ws = [jnp.exp2(m - top) for m in ms]
        num = sum(w * acc_s[br, r] for br, w in enumerate(ws))
        den = sum(w * l_s[br, r] for br, w in enumerate(ws))
        o_ref[pl.ds(r * n_per, n_per), :] = (num / den).astype(BF16)


def _attn_call(q, k, v, seq):
    t = q.shape[0]
    n_per = seq // RES
    spec = pl.BlockSpec((seq, LANES), lambda b, h: (b, h))
    staged = pltpu.VMEM((RES, n_per, LANES), F32)
    state = pltpu.VMEM((3, RES, n_per, LANES), F32)
    padded = pltpu.VMEM((RES, n_per + ATT_BLK, LANES), F32)
    return pl.pallas_call(
        _attn_kernel,
        grid=(t // seq, ATT_WIDTH // LANES),
        in_specs=[spec, spec, spec],
        out_specs=spec,
        out_shape=jax.ShapeDtypeStruct((t, ATT_WIDTH), BF16),
        scratch_shapes=[staged, padded, padded, state, state, state,
                        pltpu.VMEM((3, 2, 2 * ATT_BLK, 2 * ATT_BLK), F32)],
        compiler_params=_params("parallel", "parallel"),
        name="dilated_attn",
    )(q, k, v)


def _row_min_index(cond, lane_f):
    return jnp.min(jnp.where(cond, lane_f, float(LANES)), axis=1, keepdims=True)


def _out_kernel(att_ref, u_ref, uh_ref, x_ref, wp_ref, ps_ref, wo_ref, g2_ref, rh_ref, rl_ref, br_ref,
                xo_ref, he_ref, bk_ref):
    i = pl.program_id(1)
    nb = u_ref.shape[2]
    rows = RES * nb
    u = u_ref[0].astype(F32)
    halo = jnp.where(i > 0, uh_ref[0][:, -1:, :].astype(F32), 0.0)
    u_prev = jnp.concatenate([halo, u[:, :nb - 1, :]], axis=1)

    n_idx = lax.broadcasted_iota(I32, (RES, nb, POOL_GROUP_DIM), 1) + i * nb
    r_idx = lax.broadcasted_iota(I32, (RES, nb, POOL_GROUP_DIM), 0)
    p1 = (RES * n_idx + r_idx + 1).astype(F32)

    pools = []
    for g, w in enumerate(POOL_WINDOWS):
        lanes = slice(g * POOL_GROUP_DIM, (g + 1) * POOL_GROUP_DIM)
        ug = u[:, :, lanes]
        upg = u_prev[:, :, lanes]
        tot = ug
        for j in range(1, w):
            tot = tot + jnp.concatenate([upg[RES - j:], ug[:RES - j]], axis=0)
        rg = tot / jnp.minimum(p1, float(w)) - ug
        y = _dot(rg.reshape(rows, POOL_GROUP_DIM).astype(BF16), wp_ref[g])
        pools.append((y * ps_ref[:, lanes]).astype(BF16))
    mix = jnp.concatenate([att_ref[0].reshape(rows, ATT_WIDTH)] + pools, axis=1)
    x = x_ref[0].reshape(rows, D_MODEL) + _dot(mix, wo_ref[...])
    xo_ref[0] = x.reshape(RES, nb, D_MODEL)

    ms = jnp.mean(x * x, axis=-1, keepdims=True)
    h = x * lax.rsqrt(ms + RMS_EPS) * g2_ref[...]

    hh = h.astype(BF16)
    hl = (h - hh.astype(F32)).astype(BF16)
    logits = _dot(hh, rh_ref[...]) + (_dot(hl, rh_ref[...]) + _dot(hh, rl_ref[...])) + br_ref[...]
    lane = lax.broadcasted_iota(I32, (rows, LANES), 1)
    lane_f = lane.astype(F32)
    is_g = lane < N_EXPERT_GROUPS
    gl = jnp.where(is_g, logits, -jnp.inf)
    gm = jnp.max(gl, axis=1, keepdims=True)
    g_idx = _row_min_index(is_g & (gl == gm), lane_f)
    p_top = 1.0 / jnp.sum(jnp.where(is_g, jnp.exp(logits - gm), 0.0), axis=1, keepdims=True)
    e_lane = lane - N_EXPERT_GROUPS
    in_grp = (e_lane >= 0) & (e_lane < N_EXPERT_GROUPS * EXPERTS_PER_GROUP) & \
             ((e_lane >> 2).astype(F32) == g_idx)
    el = jnp.where(in_grp, logits, -jnp.inf)
    v1 = jnp.max(el, axis=1, keepdims=True)
    i1 = _row_min_index(in_grp & (el == v1), lane_f)
    rest = in_grp & (lane_f != i1)
    el2 = jnp.where(rest, logits, -jnp.inf)
    v2 = jnp.max(el2, axis=1, keepdims=True)
    i2 = _row_min_index(rest & (el2 == v2), lane_f)
    e21 = jnp.exp(v2 - v1)
    w1 = p_top / (1.0 + e21)
    w2 = p_top * e21 / (1.0 + e21)
    a1 = i1 - N_EXPERT_GROUPS - EXPERTS_PER_GROUP * g_idx
    a2 = i2 - N_EXPERT_GROUPS - EXPERTS_PER_GROUP * g_idx
    first_low = a1 < a2
    lo = jnp.where(first_low, a1, a2)
    hi = jnp.where(first_low, a2, a1)
    w_lo = jnp.where(first_low, w1, w2)
    w_hi = jnp.where(first_low, w2, w1)
    bucket = g_idx * PAIRS_PER_GROUP + lo * 3.0 - lo * (lo - 1.0) * 0.5 + hi - lo - 1.0
    meta = jnp.where(lane == 0, w_lo, jnp.where(lane == 1, w_hi, bucket))
    bk_ref[0] = meta.reshape(RES, nb, LANES)

    for s in range(ROW_PITCH):
        part = h[:, s * LANES:(s + 1) * LANES] if s < SUBLANES else meta
        for r in range(RES):
            he_ref[0, r, _row_part(nb, s), :] = part[r * nb:(r + 1) * nb]


def _out_call(att, u, x, wp, ps, wo, layer, g2, rh, rl, br, seq):
    t = x.shape[0]
    b = t // seq
    n_per = seq // RES
    nb = ROW_TILE // RES
    halo_rows = 16
    v4 = lambda a: a.reshape(b, RES, n_per, a.shape[-1])
    tile = lambda bi, i: (bi, 0, i, 0)
    halo = lambda bi, i: (bi, 0, jnp.maximum(i * (nb // halo_rows) - 1, 0), 0)
    fix2 = lambda bi, i: (0, 0)
    xo, he, bk = pl.pallas_call(
        _out_kernel,
        grid=(b, n_per // nb),
        in_specs=[
            pl.BlockSpec((1, RES, nb, ATT_WIDTH), tile),
            pl.BlockSpec((1, RES, nb, POOL_WIDTH), tile),
            pl.BlockSpec((1, RES, halo_rows, POOL_WIDTH), halo),
            pl.BlockSpec((1, RES, nb, D_MODEL), tile),
            pl.BlockSpec((None, POOL_GROUPS, POOL_GROUP_DIM, POOL_GROUP_DIM), lambda bi, i: (layer, 0, 0, 0)),
            pl.BlockSpec((1, POOL_WIDTH), fix2),
            pl.BlockSpec((None, D_MODEL, D_MODEL), lambda bi, i: (layer, 0, 0)),
            pl.BlockSpec((1, D_MODEL), fix2),
            pl.BlockSpec((D_MODEL, LANES), fix2),
            pl.BlockSpec((D_MODEL, LANES), fix2),
            pl.BlockSpec((1, LANES), fix2),
        ],
        out_specs=[pl.BlockSpec((1, RES, nb, D_MODEL), tile),
                   pl.BlockSpec((1, RES, nb * ROW_PITCH, LANES), tile),
                   pl.BlockSpec((1, RES, nb, LANES), tile)],
        out_shape=[jax.ShapeDtypeStruct((b, RES, n_per, D_MODEL), F32),
                   jax.ShapeDtypeStruct((b, RES, n_per * ROW_PITCH, LANES), F32),
                   jax.ShapeDtypeStruct((b, RES, n_per, LANES), F32)],
        compiler_params=_params("parallel", "parallel"),
        name="out_proj_router",
    )(v4(att), v4(u), v4(u), v4(x), wp, ps, wo, g2, rh, rl, br)
    return xo.reshape(t, D_MODEL), he.reshape(t * ROW_PITCH, LANES), bk.reshape(t, LANES)


def _sort_kernel(meta_ref, pos_ref, tinfo_ref, cnt_s, off_s):
    phase = pl.program_id(0)
    i = pl.program_id(1)
    rows = meta_ref.shape[0]
    lane = lax.broadcasted_iota(I32, (rows, LANES), 1)
    onehot = lane.astype(F32) == meta_ref[:, 2:3]
    oh = onehot.astype(F32)
    tile_count = jnp.sum(oh, axis=0, keepdims=True)

    @pl.when((phase == 0) & (i == 0))
    def _():
        cnt_s[...] = jnp.zeros_like(cnt_s)

    @pl.when(phase == 0)
    def _():
        cnt_s[...] += tile_count

    @pl.when((phase == 1) & (i == 0))
    def _():
        tiles = jnp.floor((cnt_s[...] + (MOE_TILE - 1.0)) * (1.0 / MOE_TILE))
        tiles8 = jnp.broadcast_to(tiles, (SUBLANES, LANES)).astype(BF16)
        sq = (LANES, LANES)
        before = lax.broadcasted_iota(I32, sq, 0) < lax.broadcasted_iota(I32, sq, 1)
        start = _dot(tiles8, before.astype(BF16))
        off_s[...] = start[0:1] * float(MOE_TILE)
        cnt_s[...] = jnp.zeros_like(cnt_s)
        end = (start + tiles8.astype(F32)).astype(BF16)
        eye = (lax.broadcasted_iota(I32, sq, 0) == lax.broadcasted_iota(I32, sq, 1)).astype(BF16)
        end_col = _dot_nt(eye, end)[:, 0:1]
        tile_id = lax.broadcasted_iota(I32, sq, 1).astype(F32)
        tile_bucket = jnp.sum((end_col <= tile_id).astype(F32), axis=0, keepdims=True)
        total = jnp.max(end.astype(F32)[0:1], axis=1, keepdims=True)
        row_lane = lax.broadcasted_iota(I32, (1, LANES), 1)
        ends = pltpu.roll(end.astype(F32)[0:1], END_LANE, 1)
        is_end = (row_lane >= END_LANE) & (row_lane < END_LANE + N_BUCKETS)
        tinfo_ref[...] = jnp.where(row_lane == LANES - 1, total, jnp.where(is_end, ends, tile_bucket)).astype(I32)

    @pl.when(phase == 1)
    def _():
        sq = (rows, rows)
        upto = lax.broadcasted_iota(I32, sq, 1) <= lax.broadcasted_iota(I32, sq, 0)
        prefix = _dot(upto.astype(BF16), oh.astype(BF16))
        posv = jnp.where(onehot, prefix - 1.0 + cnt_s[...] + off_s[...], 0.0)
        hi = jnp.floor(posv * (1.0 / 256.0))
        lo = posv - hi * 256.0
        ones = jnp.ones((SUBLANES, LANES), BF16)
        pos = _dot_nt(ones, hi.astype(BF16)) * 256.0 + _dot_nt(ones, lo.astype(BF16))
        pos_ref[0] = pos[0:1].astype(I32)
        cnt_s[...] += tile_count


def _sort_call(bucket):
    t = bucket.shape[0]
    n_tiles = t // ROW_TILE
    return pl.pallas_call(
        _sort_kernel,
        grid=(2, n_tiles),
        in_specs=[pl.BlockSpec((ROW_TILE, LANES), lambda p, i: (i, 0))],
        out_specs=[pl.BlockSpec((1, 1, ROW_TILE), lambda p, i: (i * p, 0, 0)),
                   pl.BlockSpec((1, LANES), lambda p, i: (0, 0))],
        out_shape=[jax.ShapeDtypeStruct((n_tiles, 1, ROW_TILE), I32),
                   jax.ShapeDtypeStruct((1, LANES), I32)],
        scratch_shapes=[pltpu.VMEM((1, LANES), F32), pltpu.VMEM((1, LANES), F32)],
        compiler_params=_params("arbitrary", "arbitrary"),
        name="bucket_sort",
    )(bucket)


def _scatter_kernel(tinfo_ref, pos_ref, he_ref, hs_ref, zeros_ref, sem, zsem):
    rows = he_ref.shape[0] // ROW_PITCH

    @pl.when(pl.program_id(0) == 0)
    def _():
        zeros_ref[...] = jnp.zeros_like(zeros_ref)

        def tile_fill(j):
            return pltpu.make_async_copy(zeros_ref, _row_tile(hs_ref, j * MOE_TILE, MOE_TILE), zsem)

        def fill(b):
            end = tinfo_ref[0, END_LANE + b]
            begin = tinfo_ref[0, END_LANE + b - 1] if b else 0
            return end > begin, tile_fill(end - 1)

        unused = tile_fill
        tiles = hs_ref.shape[0] // (MOE_TILE * ROW_PITCH)
        used = tinfo_ref[0, LANES - 1]
        for b in range(N_BUCKETS):
            nonempty, copy = fill(b)
            pl.when(nonempty)(copy.start)
        lax.fori_loop(used, tiles, lambda j, c: (unused(j).start(), c)[1], 0)
        for b in range(N_BUCKETS):
            nonempty, copy = fill(b)
            pl.when(nonempty)(copy.wait)
        lax.fori_loop(used, tiles, lambda j, c: (unused(j).wait(), c)[1], 0)

    def start(g, c):
        for u in range(ISSUE_UNROLL):
            t = g * ISSUE_UNROLL + u
            pltpu.make_async_copy(_row_tile(he_ref, t), _row_tile(hs_ref, pos_ref[0, 0, t]),
                                  sem).start(priority=u % 2)
        return c

    lax.fori_loop(0, rows // ISSUE_UNROLL, start, 0)
    pltpu.make_async_copy(he_ref, _row_tile(hs_ref, 0, rows), sem).wait()


def _scatter_call(tinfo, pos, he, sorted_rows):
    t = he.shape[0] // ROW_PITCH
    return pl.pallas_call(
        _scatter_kernel,
        grid=(t // ROW_TILE,),
        in_specs=[pl.BlockSpec(memory_space=pltpu.SMEM),
                  pl.BlockSpec((1, 1, ROW_TILE), lambda i: (i, 0, 0), memory_space=pltpu.SMEM),
                  pl.BlockSpec((ROW_TILE * ROW_PITCH, LANES), lambda i: (i, 0))],
        out_specs=pl.BlockSpec(memory_space=pl.ANY),
        out_shape=jax.ShapeDtypeStruct((sorted_rows * ROW_PITCH, LANES), F32),
        scratch_shapes=[pltpu.VMEM((MOE_TILE * ROW_PITCH, LANES), F32), pltpu.SemaphoreType.DMA(()),
                        pltpu.SemaphoreType.DMA(())],
        compiler_params=_params("arbitrary"),
        name="row_scatter",
    )(tinfo, pos, he)


def _gather_kernel(pos_ref, x_ref, ys_ref, o_ref, buf, sem):
    rows = buf.shape[0] // ROW_PITCH

    def start(g, c):
        for u in range(ISSUE_UNROLL):
            t = g * ISSUE_UNROLL + u
            pltpu.make_async_copy(_row_tile(ys_ref, pos_ref[0, 0, t]), _row_tile(buf, t), sem).start(priority=u % 2)
        return c

    lax.fori_loop(0, rows // ISSUE_UNROLL, start, 0)
    pltpu.make_async_copy(_row_tile(ys_ref, 0, rows), buf, sem).wait()
    for s in range(SUBLANES):
        lanes = slice(s * LANES, (s + 1) * LANES)
        o_ref[:, lanes] = x_ref[:, lanes] + buf[_row_part(rows, s), :]


def _gather_call(pos, x, ys):
    t = x.shape[0]
    return pl.pallas_call(
        _gather_kernel,
        grid=(t // ROW_TILE,),
        in_specs=[pl.BlockSpec((1, 1, ROW_TILE), lambda i: (i, 0, 0), memory_space=pltpu.SMEM),
                  pl.BlockSpec((ROW_TILE, D_MODEL), lambda i: (i, 0)),
                  pl.BlockSpec(memory_space=pl.ANY)],
        out_specs=pl.BlockSpec((ROW_TILE, D_MODEL), lambda i: (i, 0)),
        out_shape=jax.ShapeDtypeStruct((t, D_MODEL), F32),
        scratch_shapes=[pltpu.VMEM((ROW_TILE * ROW_PITCH, LANES), F32), pltpu.SemaphoreType.DMA(())],
        compiler_params=_params("arbitrary"),
        name="row_gather_residual",
    )(pos, x, ys)


def _tile_group(j, tinfo):
    used = tinfo[LANES - 1]
    return tinfo[jnp.minimum(j, used - 1)] // PAIRS_PER_GROUP


def _moe_kernel(tinfo, hs_ref, wg_ref, wu_ref, wd_ref, ys_ref):
    j = pl.program_id(0)
    used = tinfo[LANES - 1]

    @pl.when(j < used)
    def _():
        pair = tinfo[j] % PAIRS_PER_GROUP
        e_lo = (pair >= 3).astype(I32) + (pair >= 5).astype(I32)
        e_hi = pair - (e_lo * 3 - (e_lo * (e_lo - 1)) // 2) + e_lo + 1
        xt = jnp.concatenate([hs_ref[_row_part(MOE_TILE, s), :].astype(BF16) for s in range(SUBLANES)], axis=1)

        scalars = hs_ref[_row_part(MOE_TILE, SUBLANES), :]

        def expert(e, gate):
            hg = _dot(xt, wg_ref[e])
            hu = _dot(xt, wu_ref[e])
            act = hg * (1.0 / (1.0 + jnp.exp(-hg))) * hu * gate
            return _dot(act.astype(BF16), wd_ref[e])

        y = expert(e_lo, scalars[:, 0:1]) + expert(e_hi, scalars[:, 1:2])
        for s in range(SUBLANES):
            ys_ref[_row_part(MOE_TILE, s), :] = y[:, s * LANES:(s + 1) * LANES]
        ys_ref[_row_part(MOE_TILE, SUBLANES), :] = jnp.zeros((MOE_TILE, LANES), F32)

    @pl.when(j >= used)
    def _():
        ys_ref[...] = jnp.zeros_like(ys_ref)


def _moe_call(tinfo, hs, wg, wu, wd, layer):
    sorted_rows = hs.shape[0] // ROW_PITCH
    row = lambda j, ti: (jnp.minimum(j, ti[LANES - 1] - 1), 0)
    grp = lambda j, ti: (layer, _tile_group(j, ti), 0, 0, 0)
    return pl.pallas_call(
        _moe_kernel,
        grid_spec=pltpu.PrefetchScalarGridSpec(
            num_scalar_prefetch=1,
            grid=(sorted_rows // MOE_TILE,),
            in_specs=[pl.BlockSpec((MOE_TILE * ROW_PITCH, LANES), row),
                      pl.BlockSpec((None, None, EXPERTS_PER_GROUP, D_MODEL, D_EXPERT), grp),
                      pl.BlockSpec((None, None, EXPERTS_PER_GROUP, D_MODEL, D_EXPERT), grp),
                      pl.BlockSpec((None, None, EXPERTS_PER_GROUP, D_EXPERT, D_MODEL), grp)],
            out_specs=pl.BlockSpec((MOE_TILE * ROW_PITCH, LANES), lambda j, ti: (j, 0)),
        ),
        out_shape=jax.ShapeDtypeStruct((sorted_rows * ROW_PITCH, LANES), F32),
        compiler_params=_params("arbitrary"),
        name="grouped_experts",
    )(tinfo, hs, wg, wu, wd)


def _rope_tables(seq):
    half = ROT_DIM // 2
    inv_freq = ROPE_THETA ** (-jnp.arange(0, ROT_DIM, 2, dtype=F32) / ROT_DIM)
    row = jnp.arange(seq)
    pos = (RES * (row % (seq // RES)) + row // (seq // RES)).astype(F32)
    ang = pos[:, None] * inv_freq[None, :]
    cos, sin = jnp.cos(ang), jnp.sin(ang)
    d = jnp.arange(LANES) % HEAD_DIM
    cos_l = jnp.where(d[None, :] < ROT_DIM, cos[:, d % half], 1.0)
    sin_l = sin[:, d % half]
    sa = jnp.where(d[None, :] < half, -sin_l, 0.0)
    sb = jnp.where((d[None, :] >= half) & (d[None, :] < ROT_DIM), sin_l, 0.0)
    return cos_l, sa, sb


def kernel(x, norm1_gain, w_in, q_norm_gain, k_norm_gain, w_pool, pool_scale, w_out, norm2_gain, w_group, b_group, w_router, b_router, w_gate, w_up, w_down):
    b, seq, d = x.shape
    depth = w_in.shape[0]
    t = b * seq
    n_per = seq // RES
    assert d == D_MODEL and seq % (RES * ATT_BLK) == 0 and t % ROW_TILE == 0
    sorted_rows = t + N_BUCKETS * MOE_TILE

    cos, sa, sb = _rope_tables(seq)
    lane_head = jnp.arange(LANES) // HEAD_DIM
    block_diag = (lane_head[:, None] == lane_head[None, :]).astype(BF16)

    w_in_b = w_in.astype(BF16)
    w_out_b = w_out.astype(BF16)
    w_pool_b = w_pool.astype(BF16)
    grouped = lambda w: w.astype(BF16).reshape(depth, N_EXPERT_GROUPS, EXPERTS_PER_GROUP, *w.shape[2:])
    w_gate_b, w_up_b, w_down_b = grouped(w_gate), grouped(w_up), grouped(w_down)
    n_logits = N_EXPERT_GROUPS * (1 + EXPERTS_PER_GROUP)
    w_r = jnp.pad(jnp.concatenate([w_group, w_router], axis=-1), ((0, 0), (0, 0), (0, LANES - n_logits)))
    w_r_hi = w_r.astype(BF16)
    w_r_lo = (w_r - w_r_hi.astype(F32)).astype(BF16)
    b_r = jnp.pad(jnp.concatenate([b_group, b_router], axis=-1), ((0, 0), (0, LANES - n_logits)))
    two_heads = lambda g: jnp.tile(g, (1, LANES // HEAD_DIM))

    xr = x.reshape(b, n_per, RES, d).transpose(0, 2, 1, 3).reshape(t, d)
    for l in range(depth):
        q, k, v, u = _in_call(xr, norm1_gain[l:l + 1], w_in_b, l, two_heads(q_norm_gain[l:l + 1]),
                              two_heads(k_norm_gain[l:l + 1]), cos, sa, sb, block_diag)
        att = _attn_call(q, k, v, seq)
        xr, he, meta = _out_call(att, u, xr, w_pool_b, pool_scale[l:l + 1], w_out_b, l, norm2_gain[l:l + 1],
                                 w_r_hi[l], w_r_lo[l], b_r[l:l + 1], seq)
        pos, tinfo = _sort_call(meta)
        hs = _scatter_call(tinfo, pos, he, sorted_rows)
        ys = _moe_call(tinfo.reshape(LANES), hs, w_gate_b, w_up_b, w_down_b, l)
        xr = _gather_call(pos, xr, ys)
    return xr.reshape(b, RES, n_per, d).transpose(0, 2, 1, 3).reshape(b, seq, d)
```

```python
import functools

import jax
import jax.numpy as jnp
from jax import lax
from jax.experimental import pallas as pl
from jax.experimental.pallas import tpu as pltpu

D_MODEL = 1024
N_HEADS = 8
HEAD_DIM = 64
ATT_WIDTH = N_HEADS * HEAD_DIM
POOL_GROUPS = 4
POOL_GROUP_DIM = 128
POOL_WIDTH = POOL_GROUPS * POOL_GROUP_DIM
POOL_WINDOWS = (2, 4, 8, 16)
IN_WIDTH = 3 * ATT_WIDTH + POOL_WIDTH
ROT_DIM = 16
ROPE_THETA = 500000.0
N_EXPERT_GROUPS = 4
EXPERTS_PER_GROUP = 4
D_EXPERT = 256
RMS_EPS = 1e-6
NEG_INF = -1e30
LOG2_E = 1.4426950408889634

LANES = 128
SUBLANES = 8
MXU_WIDTH = 256
RES = 16
ATT_BLK = 128
ATT_STEP_BLOCKS = 32
PAIRS_PER_GROUP = 6
N_BUCKETS = N_EXPERT_GROUPS * PAIRS_PER_GROUP
ROW_TILE = 512
MOE_TILE = 256
ISSUE_UNROLL = 8
ROW_PITCH = SUBLANES + 1
END_LANE = 96
VMEM_LIMIT = 56 * 1024 * 1024

F32 = jnp.float32
BF16 = jnp.bfloat16
I32 = jnp.int32


def _dot(a, b):
    return jnp.dot(a, b, preferred_element_type=F32)


def _dot_nt(a, b):
    return lax.dot_general(a, b, (((1,), (1,)), ((), ())), preferred_element_type=F32)


def _row_tile(ref, row, n=1):
    return ref.at[pl.ds(row * ROW_PITCH, n * ROW_PITCH), :]


def _row_part(n, s):
    return pl.ds(s, n, stride=ROW_PITCH)


def _params(*sem):
    return pltpu.CompilerParams(dimension_semantics=sem, vmem_limit_bytes=VMEM_LIMIT)


def _in_kernel(x_ref, g1_ref, w_ref, qg_ref, kg_ref, cos_ref, sa_ref, sb_ref, bd_ref,
               q_ref, k_ref, v_ref, u_ref, wb_ref):
    @pl.when(pl.program_id(0) == 0)
    def _():
        wb_ref[...] = w_ref[...].astype(BF16)

    x = x_ref[...]
    ms = jnp.mean(x * x, axis=-1, keepdims=True)
    h = (x * lax.rsqrt(ms + RMS_EPS) * g1_ref[...]).astype(BF16)
    cos = cos_ref[...]
    sa = sa_ref[...]
    sb = sb_ref[...]
    bd = bd_ref[...]
    wide = bd.shape[0]

    def qk(col0, gain, out_ref, scale):
        z = _dot(h, wb_ref[:, col0:col0 + ATT_WIDTH])
        for w0 in range(0, ATT_WIDTH, wide):
            zw = z[:, w0:w0 + wide]
            zz = zw * zw
            hi = zz.astype(BF16)
            lo = (zz - hi.astype(F32)).astype(BF16)
            ssq = _dot(hi, bd) + _dot(lo, bd)
            yw = zw * lax.rsqrt(ssq * (1.0 / HEAD_DIM) + RMS_EPS)
            for c in range(wide // LANES):
                y = yw[:, c * LANES:(c + 1) * LANES] * gain
                rot = y * cos + pltpu.roll(y, LANES - ROT_DIM // 2, 1) * sa + pltpu.roll(y, ROT_DIM // 2, 1) * sb
                out_ref[:, w0 + c * LANES:w0 + (c + 1) * LANES] = (rot * scale).astype(BF16)

    qk(0, qg_ref[...], q_ref, HEAD_DIM ** -0.5 * LOG2_E)
    qk(ATT_WIDTH, kg_ref[...], k_ref, 1.0)
    v_ref[...] = _dot(h, wb_ref[:, 2 * ATT_WIDTH:3 * ATT_WIDTH]).astype(BF16)
    u_ref[...] = _dot(h, wb_ref[:, 3 * ATT_WIDTH:]).astype(BF16)


def _in_call(x, g1, w, layer, qg, kg, cos, sa, sb, bd):
    t = x.shape[0]
    seq_tiles = cos.shape[0] // ROW_TILE
    row = lambda i: (i, 0)
    fix = lambda i: (0, 0)
    tab = lambda i: (i % seq_tiles, 0)
    out = jax.ShapeDtypeStruct((t, ATT_WIDTH), BF16)
    return pl.pallas_call(
        _in_kernel,
        grid=(t // ROW_TILE,),
        in_specs=[
            pl.BlockSpec((ROW_TILE, D_MODEL), row),
            pl.BlockSpec((1, D_MODEL), fix),
            pl.BlockSpec((None, D_MODEL, IN_WIDTH), lambda i: (layer, 0, 0)),
            pl.BlockSpec((1, LANES), fix),
            pl.BlockSpec((1, LANES), fix),
            pl.BlockSpec((ROW_TILE, LANES), tab),
            pl.BlockSpec((ROW_TILE, LANES), tab),
            pl.BlockSpec((ROW_TILE, LANES), tab),
            pl.BlockSpec(bd.shape, fix),
        ],
        out_specs=[pl.BlockSpec((ROW_TILE, ATT_WIDTH), row)] * 4,
        out_shape=[out] * 4,
        scratch_shapes=[pltpu.VMEM((D_MODEL, IN_WIDTH), BF16)],
        compiler_params=_params("arbitrary"),
        name="in_proj",
    )(x, g1, w, qg, kg, cos, sa, sb, bd)


def _attn_bias(q_off, k_idx, with_prev):
    ok = (k_idx >= q_off) & (k_idx <= q_off + ATT_BLK)
    if not with_prev:
        ok = ok & (k_idx >= ATT_BLK)
    return jnp.where(ok, 0.0, NEG_INF).astype(F32)


def _attn_kernel(q_ref, k_ref, v_ref, o_ref, q32, k32, v32, m_s, l_s, acc_s, bias_s):
    n_per = q32.shape[1]
    pad = ATT_BLK
    zeros = jnp.zeros((pad, LANES), F32)
    for r in range(RES):
        rows = pl.ds(r * n_per, n_per)
        q32[r] = q_ref[rows, :].astype(F32)
        k32[r, pl.ds(0, pad), :] = zeros
        v32[r, pl.ds(0, pad), :] = zeros
        k32[r, pl.ds(pad, n_per), :] = k_ref[rows, :].astype(F32)
        v32[r, pl.ds(pad, n_per), :] = v_ref[rows, :].astype(F32)

    qi = lax.broadcasted_iota(I32, (2 * ATT_BLK, 2 * ATT_BLK), 0) & (ATT_BLK - 1)
    kc = lax.broadcasted_iota(I32, (2 * ATT_BLK, 2 * ATT_BLK), 1)
    offs = (
        (16 * (qi & 7) + (qi >> 3), 16 * (kc & 15) + (kc >> 4)),
        (4 * (qi & 31) + (qi >> 5), 4 * (kc & 63) + (kc >> 6)),
        (qi, kc),
    )
    for br, (qo, ko) in enumerate(offs):
        bias_s[br, 0] = _attn_bias(qo, ko, False)
        bias_s[br, 1] = _attn_bias(qo, ko, True)

    head_a = lax.broadcasted_iota(I32, (ATT_BLK, LANES), 1) < HEAD_DIM

    def block(qb, ks, vs, bias):
        qa = jnp.where(head_a, qb, 0.0)
        qq = jnp.concatenate([qa, qb - qa], axis=0).astype(BF16)
        s = _dot_nt(qq, ks.astype(BF16)) + bias
        m = jnp.max(s, axis=1, keepdims=True)
        p = jnp.exp2(s - m)
        l = jnp.sum(p, axis=1, keepdims=True)
        pv = _dot(p.astype(BF16), vs.astype(BF16))
        m2 = jnp.where(head_a, m[:ATT_BLK], m[ATT_BLK:])
        l2 = jnp.where(head_a, l[:ATT_BLK], l[ATT_BLK:])
        pv2 = jnp.where(head_a, pv[:ATT_BLK], pv[ATT_BLK:])
        return m2, l2, pv2

    def store(br, slab, rows, triple, shape=None):
        for ref, val in zip((m_s, l_s, acc_s), triple):
            ref[br, slab, rows, :] = val if shape is None else val.reshape(shape)

    slabs16 = ATT_STEP_BLOCKS // (n_per // ATT_BLK)

    def body16(rr, carry):
        for i in range(slabs16):
            r = slabs16 * rr + i
            for c in range(n_per // ATT_BLK):
                rows = pl.ds(c * ATT_BLK, ATT_BLK)
                keys = pl.ds(c * ATT_BLK, 2 * ATT_BLK)
                store(2, r, rows, block(q32[r, rows, :], k32[r, keys, :], v32[r, keys, :], bias_s[2, min(c, 1)]))
        return carry

    lax.fori_loop(0, RES // slabs16, body16, 0)

    sub = ATT_BLK // 4
    per4 = ATT_STEP_BLOCKS // 4

    def body4(cc, carry):
        for r4 in range(4):
            slabs = [r4 + 4 * m for m in range(4)]
            for c in range(per4):
                first = pl.multiple_of((cc * per4 + c) * sub, sub)
                rows = pl.ds(first, sub)
                keys = pl.ds(first + pad - sub, 2 * sub)
                cat = lambda ref, idx: jnp.concatenate([ref[s, idx, :] for s in slabs], axis=0)
                bias = bias_s[1, jnp.minimum(cc, 1)] if c == 0 else bias_s[1, 1]
                triple = block(cat(q32, rows), cat(k32, keys), cat(v32, keys), bias)
                for j, s in enumerate(slabs):
                    store(1, s, rows, [x[j * sub:(j + 1) * sub] for x in triple])
        return carry

    lax.fori_loop(0, n_per // (per4 * sub), body4, 0)

    sub1 = ATT_BLK // RES

    def body1(jj, carry):
        for g in range(ATT_STEP_BLOCKS):
            first = pl.multiple_of((jj * ATT_STEP_BLOCKS + g) * sub1, sub1)
            rows = pl.ds(first, sub1)
            keys = pl.ds(first + pad - sub1, 2 * sub1)
            bias = bias_s[0, jnp.minimum(jj, 1)] if g == 0 else bias_s[0, 1]
            triple = block(q32[:, rows, :].reshape(ATT_BLK, LANES), k32[:, keys, :].reshape(2 * ATT_BLK, LANES),
                           v32[:, keys, :].reshape(2 * ATT_BLK, LANES), bias)
            store(0, slice(None), rows, triple, (RES, sub1, LANES))
        return carry

    lax.fori_loop(0, n_per // (ATT_STEP_BLOCKS * sub1), body1, 0)

    for r in range(RES):
        ms = [m_s[br, r] for br in range(3)]
        top = jnp.maximum(jnp.maximum(ms[0], ms[1]), ms[2])
        ws = [jnp.exp2(m - top) for m in ms]
        num = sum(w * acc_s[br, r] for br, w in enumerate(ws))
        den = sum(w * l_s[br, r] for br, w in enumerate(ws))
        o_ref[pl.ds(r * n_per, n_per), :] = (num / den).astype(BF16)


def _attn_call(q, k, v, seq):
    t = q.shape[0]
    n_per = seq // RES
    spec = pl.BlockSpec((seq, LANES), lambda b, h: (b, h))
    staged = pltpu.VMEM((RES, n_per, LANES), F32)
    state = pltpu.VMEM((3, RES, n_per, LANES), F32)
    padded = pltpu.VMEM((RES, n_per + ATT_BLK, LANES), F32)
    return pl.pallas_call(
        _attn_kernel,
        grid=(t // seq, ATT_WIDTH // LANES),
        in_specs=[spec, spec, spec],
        out_specs=spec,
        out_shape=jax.ShapeDtypeStruct((t, ATT_WIDTH), BF16),
        scratch_shapes=[staged, padded, padded, state, state, state,
                        pltpu.VMEM((3, 2, 2 * ATT_BLK, 2 * ATT_BLK), F32)],
        compiler_params=_params("parallel", "parallel"),
        name="dilated_attn",
    )(q, k, v)


def _row_min_index(cond, lane_f):
    return jnp.min(jnp.where(cond, lane_f, float(LANES)), axis=1, keepdims=True)


def _out_kernel(att_ref, u_ref, uh_ref, x_ref, wp_ref, ps_ref, wo_ref, g2_ref, rw_ref, br_ref,
                xo_ref, he_ref, bk_ref, wob_ref):
    i = pl.program_id(1)

    @pl.when((pl.program_id(0) == 0) & (i == 0))
    def _():
        wob_ref[...] = wo_ref[...].astype(BF16)

    nb = u_ref.shape[2]
    rows = RES * nb
    u = u_ref[0].astype(F32)
    halo = jnp.where(i > 0, uh_ref[0][:, -1:, :].astype(F32), 0.0)
    u_prev = jnp.concatenate([halo, u[:, :nb - 1, :]], axis=1)

    n_idx = lax.broadcasted_iota(I32, (RES, nb, POOL_GROUP_DIM), 1) + i * nb
    r_idx = lax.broadcasted_iota(I32, (RES, nb, POOL_GROUP_DIM), 0)
    p1 = (RES * n_idx + r_idx + 1).astype(F32)

    pools = []
    for g, w in enumerate(POOL_WINDOWS):
        lanes = slice(g * POOL_GROUP_DIM, (g + 1) * POOL_GROUP_DIM)
        ug = u[:, :, lanes]
        upg = u_prev[:, :, lanes]
        tot = ug
        for j in range(1, w):
            tot = tot + jnp.concatenate([upg[RES - j:], ug[:RES - j]], axis=0)
        rg = tot / jnp.minimum(p1, float(w)) - ug
        y = _dot(rg.reshape(rows, POOL_GROUP_DIM).astype(BF16), wp_ref[g])
        pools.append((y * ps_ref[:, lanes]).astype(BF16))
    mix = jnp.concatenate([att_ref[0].reshape(rows, ATT_WIDTH)] + pools, axis=1)
    x = x_ref[0].reshape(rows, D_MODEL) + _dot(mix, wob_ref[...])
    xo_ref[0] = x.reshape(RES, nb, D_MODEL)

    ms = jnp.mean(x * x, axis=-1, keepdims=True)
    h = x * lax.rsqrt(ms + RMS_EPS) * g2_ref[...]

    hh = h.astype(BF16)
    hl = (h - hh.astype(F32)).astype(BF16)
    by_hh = _dot(hh, rw_ref[...])
    logits = by_hh[:, :LANES] + (_dot(hl, rw_ref[:, :LANES]) + by_hh[:, LANES:]) + br_ref[...]
    lane = lax.broadcasted_iota(I32, (rows, LANES), 1)
    lane_f = lane.astype(F32)
    is_g = lane < N_EXPERT_GROUPS
    gl = jnp.where(is_g, logits, -jnp.inf)
    gm = jnp.max(gl, axis=1, keepdims=True)
    g_idx = _row_min_index(is_g & (gl == gm), lane_f)
    p_top = 1.0 / jnp.sum(jnp.where(is_g, jnp.exp(logits - gm), 0.0), axis=1, keepdims=True)
    e_lane = lane - N_EXPERT_GROUPS
    in_grp = (e_lane >= 0) & (e_lane < N_EXPERT_GROUPS * EXPERTS_PER_GROUP) & \
             ((e_lane >> 2).astype(F32) == g_idx)
    el = jnp.where(in_grp, logits, -jnp.inf)
    v1 = jnp.max(el, axis=1, keepdims=True)
    i1 = _row_min_index(in_grp & (el == v1), lane_f)
    rest = in_grp & (lane_f != i1)
    el2 = jnp.where(rest, logits, -jnp.inf)
    v2 = jnp.max(el2, axis=1, keepdims=True)
    i2 = _row_min_index(rest & (el2 == v2), lane_f)
    e21 = jnp.exp(v2 - v1)
    w1 = p_top / (1.0 + e21)
    w2 = p_top * e21 / (1.0 + e21)
    a1 = i1 - N_EXPERT_GROUPS - EXPERTS_PER_GROUP * g_idx
    a2 = i2 - N_EXPERT_GROUPS - EXPERTS_PER_GROUP * g_idx
    first_low = a1 < a2
    lo = jnp.where(first_low, a1, a2)
    hi = jnp.where(first_low, a2, a1)
    w_lo = jnp.where(first_low, w1, w2)
    w_hi = jnp.where(first_low, w2, w1)
    bucket = g_idx * PAIRS_PER_GROUP + lo * 3.0 - lo * (lo - 1.0) * 0.5 + hi - lo - 1.0
    meta = jnp.where(lane == 0, w_lo, jnp.where(lane == 1, w_hi, bucket))
    bk_ref[0] = meta.reshape(RES, nb, LANES)

    for s in range(ROW_PITCH):
        part = h[:, s * LANES:(s + 1) * LANES] if s < SUBLANES else meta
        for r in range(RES):
            he_ref[0, r, _row_part(nb, s), :] = part[r * nb:(r + 1) * nb]


def _out_call(att, u, x, wp, ps, wo, layer, g2, rw, br, seq):
    t = x.shape[0]
    b = t // seq
    n_per = seq // RES
    nb = ROW_TILE // RES
    halo_rows = 16
    v4 = lambda a: a.reshape(b, RES, n_per, a.shape[-1])
    tile = lambda bi, i: (bi, 0, i, 0)
    halo = lambda bi, i: (bi, 0, jnp.maximum(i * (nb // halo_rows) - 1, 0), 0)
    fix2 = lambda bi, i: (0, 0)
    xo, he, bk = pl.pallas_call(
        _out_kernel,
        grid=(b, n_per // nb),
        in_specs=[
            pl.BlockSpec((1, RES, nb, ATT_WIDTH), tile),
            pl.BlockSpec((1, RES, nb, POOL_WIDTH), tile),
            pl.BlockSpec((1, RES, halo_rows, POOL_WIDTH), halo),
            pl.BlockSpec((1, RES, nb, D_MODEL), tile),
            pl.BlockSpec((None, POOL_GROUPS, POOL_GROUP_DIM, POOL_GROUP_DIM), lambda bi, i: (layer, 0, 0, 0)),
            pl.BlockSpec((1, POOL_WIDTH), fix2),
            pl.BlockSpec((None, D_MODEL, D_MODEL), lambda bi, i: (layer, 0, 0)),
            pl.BlockSpec((1, D_MODEL), fix2),
            pl.BlockSpec((D_MODEL, 2 * LANES), fix2),
            pl.BlockSpec((1, LANES), fix2),
        ],
        out_specs=[pl.BlockSpec((1, RES, nb, D_MODEL), tile),
                   pl.BlockSpec((1, RES, nb * ROW_PITCH, LANES), tile),
                   pl.BlockSpec((1, RES, nb, LANES), tile)],
        out_shape=[jax.ShapeDtypeStruct((b, RES, n_per, D_MODEL), F32),
                   jax.ShapeDtypeStruct((b, RES, n_per * ROW_PITCH, LANES), F32),
                   jax.ShapeDtypeStruct((b, RES, n_per, LANES), F32)],
        scratch_shapes=[pltpu.VMEM((D_MODEL, D_MODEL), BF16)],
        compiler_params=_params("arbitrary", "arbitrary"),
        name="out_proj_router",
    )(v4(att), v4(u), v4(u), v4(x), wp, ps, wo, g2, rw, br)
    return xo.reshape(t, D_MODEL), he.reshape(t * ROW_PITCH, LANES), bk.reshape(t, LANES)


def _sort_kernel(meta_ref, pos_ref, tinfo_ref, cnt_s, off_s):
    phase = pl.program_id(0)
    i = pl.program_id(1)
    rows = meta_ref.shape[0]
    lane = lax.broadcasted_iota(I32, (rows, LANES), 1)
    onehot = lane.astype(F32) == meta_ref[:, 2:3]
    oh = onehot.astype(F32)
    tile_count = jnp.sum(oh, axis=0, keepdims=True)

    @pl.when((phase == 0) & (i == 0))
    def _():
        cnt_s[...] = jnp.zeros_like(cnt_s)

    @pl.when(phase == 0)
    def _():
        cnt_s[...] += tile_count

    @pl.when((phase == 1) & (i == 0))
    def _():
        tiles = jnp.floor((cnt_s[...] + (MOE_TILE - 1.0)) * (1.0 / MOE_TILE))
        tiles8 = jnp.broadcast_to(tiles, (SUBLANES, LANES)).astype(BF16)
        sq = (LANES, LANES)
        before = lax.broadcasted_iota(I32, sq, 0) < lax.broadcasted_iota(I32, sq, 1)
        start = _dot(tiles8, before.astype(BF16))
        off_s[...] = start[0:1] * float(MOE_TILE)
        cnt_s[...] = jnp.zeros_like(cnt_s)
        end = (start + tiles8.astype(F32)).astype(BF16)
        eye = (lax.broadcasted_iota(I32, sq, 0) == lax.broadcasted_iota(I32, sq, 1)).astype(BF16)
        end_col = _dot_nt(eye, end)[:, 0:1]
        tile_id = lax.broadcasted_iota(I32, sq, 1).astype(F32)
        tile_bucket = jnp.sum((end_col <= tile_id).astype(F32), axis=0, keepdims=True)
        total = jnp.max(end.astype(F32)[0:1], axis=1, keepdims=True)
        row_lane = lax.broadcasted_iota(I32, (1, LANES), 1)
        ends = pltpu.roll(end.astype(F32)[0:1], END_LANE, 1)
        is_end = (row_lane >= END_LANE) & (row_lane < END_LANE + N_BUCKETS)
        tinfo_ref[...] = jnp.where(row_lane == LANES - 1, total, jnp.where(is_end, ends, tile_bucket)).astype(I32)

    @pl.when(phase == 1)
    def _():
        sq = (rows, rows)
        upto = lax.broadcasted_iota(I32, sq, 1) <= lax.broadcasted_iota(I32, sq, 0)
        prefix = _dot(upto.astype(BF16), oh.astype(BF16))
        posv = jnp.where(onehot, prefix - 1.0 + cnt_s[...] + off_s[...], 0.0)
        hi = jnp.floor(posv * (1.0 / 256.0))
        lo = posv - hi * 256.0
        ones = jnp.ones((SUBLANES, LANES), BF16)
        pos = _dot_nt(ones, hi.astype(BF16)) * 256.0 + _dot_nt(ones, lo.astype(BF16))
        pos_ref[0] = pos[0:1].astype(I32)
        cnt_s[...] += tile_count


def _sort_call(bucket):
    t = bucket.shape[0]
    n_tiles = t // ROW_TILE
    return pl.pallas_call(
        _sort_kernel,
        grid=(2, n_tiles),
        in_specs=[pl.BlockSpec((ROW_TILE, LANES), lambda p, i: (i, 0))],
        out_specs=[pl.BlockSpec((1, 1, ROW_TILE), lambda p, i: (i * p, 0, 0)),
                   pl.BlockSpec((1, LANES), lambda p, i: (0, 0))],
        out_shape=[jax.ShapeDtypeStruct((n_tiles, 1, ROW_TILE), I32),
                   jax.ShapeDtypeStruct((1, LANES), I32)],
        scratch_shapes=[pltpu.VMEM((1, LANES), F32), pltpu.VMEM((1, LANES), F32)],
        compiler_params=_params("arbitrary", "arbitrary"),
        name="bucket_sort",
    )(bucket)


def _scatter_kernel(tinfo_ref, pos_ref, he_ref, hs_ref, zeros_ref, sem, zsem):
    rows = he_ref.shape[0] // ROW_PITCH

    @pl.when(pl.program_id(0) == 0)
    def _():
        zeros_ref[...] = jnp.zeros_like(zeros_ref)

        def tile_fill(j):
            return pltpu.make_async_copy(zeros_ref, _row_tile(hs_ref, j * MOE_TILE, MOE_TILE), zsem)

        def fill(b):
            end = tinfo_ref[0, END_LANE + b]
            begin = tinfo_ref[0, END_LANE + b - 1] if b else 0
            return end > begin, tile_fill(end - 1)

        unused = tile_fill
        tiles = hs_ref.shape[0] // (MOE_TILE * ROW_PITCH)
        used = tinfo_ref[0, LANES - 1]
        for b in range(N_BUCKETS):
            nonempty, copy = fill(b)
            pl.when(nonempty)(copy.start)
        lax.fori_loop(used, tiles, lambda j, c: (unused(j).start(), c)[1], 0)
        for b in range(N_BUCKETS):
            nonempty, copy = fill(b)
            pl.when(nonempty)(copy.wait)
        lax.fori_loop(used, tiles, lambda j, c: (unused(j).wait(), c)[1], 0)

    def start(g, c):
        for u in range(ISSUE_UNROLL):
            t = g * ISSUE_UNROLL + u
            pltpu.make_async_copy(_row_tile(he_ref, t), _row_tile(hs_ref, pos_ref[0, 0, t]),
                                  sem).start(priority=u % 2)
        return c

    lax.fori_loop(0, rows // ISSUE_UNROLL, start, 0)
    pltpu.make_async_copy(he_ref, _row_tile(hs_ref, 0, rows), sem).wait()


def _scatter_call(tinfo, pos, he, sorted_rows):
    t = he.shape[0] // ROW_PITCH
    return pl.pallas_call(
        _scatter_kernel,
        grid=(t // ROW_TILE,),
        in_specs=[pl.BlockSpec(memory_space=pltpu.SMEM),
                  pl.BlockSpec((1, 1, ROW_TILE), lambda i: (i, 0, 0), memory_space=pltpu.SMEM),
                  pl.BlockSpec((ROW_TILE * ROW_PITCH, LANES), lambda i: (i, 0))],
        out_specs=pl.BlockSpec(memory_space=pl.ANY),
        out_shape=jax.ShapeDtypeStruct((sorted_rows * ROW_PITCH, LANES), F32),
        scratch_shapes=[pltpu.VMEM((MOE_TILE * ROW_PITCH, LANES), F32), pltpu.SemaphoreType.DMA(()),
                        pltpu.SemaphoreType.DMA(())],
        compiler_params=_params("arbitrary"),
        name="row_scatter",
    )(tinfo, pos, he)


def _gather_kernel(pos_ref, x_ref, ys_ref, o_ref, buf, sem):
    rows = buf.shape[0] // ROW_PITCH

    def start(g, c):
        for u in range(ISSUE_UNROLL):
            t = g * ISSUE_UNROLL + u
            pltpu.make_async_copy(_row_tile(ys_ref, pos_ref[0, 0, t]), _row_tile(buf, t), sem).start(priority=u % 2)
        return c

    lax.fori_loop(0, rows // ISSUE_UNROLL, start, 0)
    pltpu.make_async_copy(_row_tile(ys_ref, 0, rows), buf, sem).wait()
    for s in range(SUBLANES):
        lanes = slice(s * LANES, (s + 1) * LANES)
        o_ref[:, lanes] = x_ref[:, lanes] + buf[_row_part(rows, s), :]


def _gather_call(pos, x, ys):
    t = x.shape[0]
    return pl.pallas_call(
        _gather_kernel,
        grid=(t // ROW_TILE,),
        in_specs=[pl.BlockSpec((1, 1, ROW_TILE), lambda i: (i, 0, 0), memory_space=pltpu.SMEM),
                  pl.BlockSpec((ROW_TILE, D_MODEL), lambda i: (i, 0)),
                  pl.BlockSpec(memory_space=pl.ANY)],
        out_specs=pl.BlockSpec((ROW_TILE, D_MODEL), lambda i: (i, 0)),
        out_shape=jax.ShapeDtypeStruct((t, D_MODEL), F32),
        scratch_shapes=[pltpu.VMEM((ROW_TILE * ROW_PITCH, LANES), F32), pltpu.SemaphoreType.DMA(())],
        compiler_params=_params("arbitrary"),
        name="row_gather_residual",
    )(pos, x, ys)


def _tile_group(j, tinfo):
    used = tinfo[LANES - 1]
    return tinfo[jnp.minimum(j, used - 1)] // PAIRS_PER_GROUP


def _moe_kernel(tinfo, hs_ref, wg32_ref, wu32_ref, wd32_ref, ys_ref, wg_ref, wu_ref, wd_ref):
    j = pl.program_id(0)
    used = tinfo[LANES - 1]

    @pl.when((j == 0) | ((j < used) & (_tile_group(j, tinfo) != _tile_group(jnp.maximum(j, 1) - 1, tinfo))))
    def _():
        wg_ref[...] = wg32_ref[...].astype(BF16)
        wu_ref[...] = wu32_ref[...].astype(BF16)
        wd_ref[...] = wd32_ref[...].astype(BF16)

    @pl.when(j < used)
    def _():
        pair = tinfo[j] % PAIRS_PER_GROUP
        e_lo = (pair >= 3).astype(I32) + (pair >= 5).astype(I32)
        e_hi = pair - (e_lo * 3 - (e_lo * (e_lo - 1)) // 2) + e_lo + 1
        xt = jnp.concatenate([hs_ref[_row_part(MOE_TILE, s), :].astype(BF16) for s in range(SUBLANES)], axis=1)

        scalars = hs_ref[_row_part(MOE_TILE, SUBLANES), :]

        def expert(e, gate):
            hg = _dot(xt, wg_ref[e])
            hu = _dot(xt, wu_ref[e])
            act = hg * (1.0 / (1.0 + jnp.exp(-hg))) * hu * gate
            return _dot(act.astype(BF16), wd_ref[e])

        y = expert(e_lo, scalars[:, 0:1]) + expert(e_hi, scalars[:, 1:2])
        for s in range(SUBLANES):
            ys_ref[_row_part(MOE_TILE, s), :] = y[:, s * LANES:(s + 1) * LANES]
        ys_ref[_row_part(MOE_TILE, SUBLANES), :] = jnp.zeros((MOE_TILE, LANES), F32)

    @pl.when(j >= used)
    def _():
        ys_ref[...] = jnp.zeros_like(ys_ref)


def _moe_call(tinfo, hs, wg, wu, wd, layer):
    sorted_rows = hs.shape[0] // ROW_PITCH
    row = lambda j, ti: (jnp.minimum(j, ti[LANES - 1] - 1), 0)
    grp = lambda j, ti: (layer, _tile_group(j, ti), 0, 0, 0)
    return pl.pallas_call(
        _moe_kernel,
        grid_spec=pltpu.PrefetchScalarGridSpec(
            num_scalar_prefetch=1,
            grid=(sorted_rows // MOE_TILE,),
            in_specs=[pl.BlockSpec((MOE_TILE * ROW_PITCH, LANES), row),
                      pl.BlockSpec((None, None, EXPERTS_PER_GROUP, D_MODEL, D_EXPERT), grp),
                      pl.BlockSpec((None, None, EXPERTS_PER_GROUP, D_MODEL, D_EXPERT), grp),
                      pl.BlockSpec((None, None, EXPERTS_PER_GROUP, D_EXPERT, D_MODEL), grp)],
            out_specs=pl.BlockSpec((MOE_TILE * ROW_PITCH, LANES), lambda j, ti: (j, 0)),
            scratch_shapes=[pltpu.VMEM((EXPERTS_PER_GROUP, D_MODEL, D_EXPERT), BF16),
                            pltpu.VMEM((EXPERTS_PER_GROUP, D_MODEL, D_EXPERT), BF16),
                            pltpu.VMEM((EXPERTS_PER_GROUP, D_EXPERT, D_MODEL), BF16)],
        ),
        out_shape=jax.ShapeDtypeStruct((sorted_rows * ROW_PITCH, LANES), F32),
        compiler_params=_params("arbitrary"),
        name="grouped_experts",
    )(tinfo, hs, wg, wu, wd)


def _rope_tables(seq):
    half = ROT_DIM // 2
    inv_freq = ROPE_THETA ** (-jnp.arange(0, ROT_DIM, 2, dtype=F32) / ROT_DIM)
    row = jnp.arange(seq)
    pos = (RES * (row % (seq // RES)) + row // (seq // RES)).astype(F32)
    ang = pos[:, None] * inv_freq[None, :]
    cos, sin = jnp.cos(ang), jnp.sin(ang)
    d = jnp.arange(LANES) % HEAD_DIM
    cos_l = jnp.where(d[None, :] < ROT_DIM, cos[:, d % half], 1.0)
    sin_l = sin[:, d % half]
    sa = jnp.where(d[None, :] < half, -sin_l, 0.0)
    sb = jnp.where((d[None, :] >= half) & (d[None, :] < ROT_DIM), sin_l, 0.0)
    return cos_l, sa, sb


def kernel(x, norm1_gain, w_in, q_norm_gain, k_norm_gain, w_pool, pool_scale, w_out, norm2_gain, w_group, b_group, w_router, b_router, w_gate, w_up, w_down):
    b, seq, d = x.shape
    depth = w_in.shape[0]
    t = b * seq
    n_per = seq // RES
    assert d == D_MODEL and seq % (RES * ATT_BLK) == 0 and t % ROW_TILE == 0
    sorted_rows = t + N_BUCKETS * MOE_TILE

    cos, sa, sb = _rope_tables(seq)
    lane_head = jnp.arange(MXU_WIDTH) // HEAD_DIM
    block_diag = (lane_head[:, None] == lane_head[None, :]).astype(BF16)

    w_pool_b = w_pool.astype(BF16)
    grouped = lambda w: w.reshape(depth, N_EXPERT_GROUPS, EXPERTS_PER_GROUP, *w.shape[2:])
    w_gate_g, w_up_g, w_down_g = grouped(w_gate), grouped(w_up), grouped(w_down)
    n_logits = N_EXPERT_GROUPS * (1 + EXPERTS_PER_GROUP)
    w_r = jnp.pad(jnp.concatenate([w_group, w_router], axis=-1), ((0, 0), (0, 0), (0, LANES - n_logits)))
    w_r_hi = w_r.astype(BF16)
    w_r_lo = (w_r - w_r_hi.astype(F32)).astype(BF16)
    w_r_split = jnp.concatenate([w_r_hi, w_r_lo], axis=-1)
    b_r = jnp.pad(jnp.concatenate([b_group, b_router], axis=-1), ((0, 0), (0, LANES - n_logits)))
    two_heads = lambda g: jnp.tile(g, (1, LANES // HEAD_DIM))

    xr = x.reshape(b, n_per, RES, d).transpose(0, 2, 1, 3).reshape(t, d)
    for l in range(depth):
        q, k, v, u = _in_call(xr, norm1_gain[l:l + 1], w_in, l, two_heads(q_norm_gain[l:l + 1]),
                              two_heads(k_norm_gain[l:l + 1]), cos, sa, sb, block_diag)
        att = _attn_call(q, k, v, seq)
        xr, he, meta = _out_call(att, u, xr, w_pool_b, pool_scale[l:l + 1], w_out, l, norm2_gain[l:l + 1],
                                 w_r_split[l], b_r[l:l + 1], seq)
        pos, tinfo = _sort_call(meta)
        hs = _scatter_call(tinfo, pos, he, sorted_rows)
        ys = _moe_call(tinfo.reshape(LANES), hs, w_gate_g, w_up_g, w_down_g, l)
        xr = _gather_call(pos, xr, ys)
    return xr.reshape(b, RES, n_per, d).transpose(0, 2, 1, 3).reshape(b, seq, d)
```

```python
import functools

import jax
import jax.numpy as jnp
from jax import lax
from jax.experimental import pallas as pl
from jax.experimental.pallas import tpu as pltpu

D_MODEL = 1024
N_HEADS = 8
HEAD_DIM = 64
ATT_WIDTH = N_HEADS * HEAD_DIM
POOL_GROUPS = 4
POOL_GROUP_DIM = 128
POOL_WIDTH = POOL_GROUPS * POOL_GROUP_DIM
POOL_WINDOWS = (2, 4, 8, 16)
IN_WIDTH = 3 * ATT_WIDTH + POOL_WIDTH
ROT_DIM = 16
ROPE_THETA = 500000.0
N_EXPERT_GROUPS = 4
EXPERTS_PER_GROUP = 4
D_EXPERT = 256
RMS_EPS = 1e-6
NEG_INF = -1e30
LOG2_E = 1.4426950408889634

LANES = 128
SUBLANES = 8
MXU_WIDTH = 256
RES = 16
ATT_BLK = 128
ATT_STEP_BLOCKS = 32
PAIRS_PER_GROUP = 6
N_BUCKETS = N_EXPERT_GROUPS * PAIRS_PER_GROUP
ROW_TILE = 512
MOE_TILE = 256
ISSUE_UNROLL = 8
ROW_PITCH = SUBLANES + 1
END_LANE = 96
VMEM_LIMIT = 56 * 1024 * 1024

F32 = jnp.float32
BF16 = jnp.bfloat16
I32 = jnp.int32


def _dot(a, b):
    return jnp.dot(a, b, preferred_element_type=F32)


def _dot_nt(a, b):
    return lax.dot_general(a, b, (((1,), (1,)), ((), ())), preferred_element_type=F32)


def _row_tile(ref, row, n=1):
    return ref.at[pl.ds(row * ROW_PITCH, n * ROW_PITCH), :]


def _row_part(n, s):
    return pl.ds(s, n, stride=ROW_PITCH)


def _params(*sem):
    return pltpu.CompilerParams(dimension_semantics=sem, vmem_limit_bytes=VMEM_LIMIT)


def _issue_row_gather(ys_ref, pos_ref, tile, buf, first, sem):
    def start(g, c):
        for u in range(ISSUE_UNROLL):
            t = g * ISSUE_UNROLL + u
            pltpu.make_async_copy(_row_tile(ys_ref, pos_ref[tile, 0, t]), _row_tile(buf, first + t),
                                  sem).start(priority=u % 2)
        return c

    lax.fori_loop(0, ROW_TILE // ISSUE_UNROLL, start, 0)


def _in_kernel(*refs, with_moe):
    if with_moe:
        pos_ref, ys_ref, refs = refs[0], refs[1], refs[2:]
        xo_ref, buf, sem = refs[-4], refs[-2], refs[-1]
        refs = refs[:-4] + (refs[-3],)
    x_ref, g1_ref, w_ref, qg_ref, kg_ref, cos_ref, sa_ref, sb_ref, bd_ref, q_ref, k_ref, v_ref, u_ref, wb_ref = refs
    i = pl.program_id(0)

    if with_moe:
        slot_rows = ROW_TILE
        @pl.when(i == 0)
        def _():
            _issue_row_gather(ys_ref, pos_ref, 0, buf, 0, sem.at[0])

        @pl.when(i + 1 < pl.num_programs(0))
        def _():
            _issue_row_gather(ys_ref, pos_ref, i + 1, buf, ((i + 1) % 2) * slot_rows, sem.at[(i + 1) % 2])

    @pl.when(i == 0)
    def _():
        wb_ref[...] = w_ref[...].astype(BF16)

    if with_moe:
        first = (i % 2) * slot_rows
        pltpu.make_async_copy(_row_tile(ys_ref, 0, slot_rows), _row_tile(buf, first, slot_rows), sem.at[i % 2]).wait()
        x = jnp.concatenate([x_ref[:, s * LANES:(s + 1) * LANES]
                             + buf[pl.ds(first * ROW_PITCH + s, slot_rows, stride=ROW_PITCH), :]
                             for s in range(SUBLANES)], axis=1)
        xo_ref[...] = x
    else:
        x = x_ref[...]
    ms = jnp.mean(x * x, axis=-1, keepdims=True)
    h = (x * lax.rsqrt(ms + RMS_EPS) * g1_ref[...]).astype(BF16)
    cos = cos_ref[...]
    sa = sa_ref[...]
    sb = sb_ref[...]
    bd = bd_ref[...]
    wide = bd.shape[0]

    def qk(col0, gain, out_ref, scale):
        z = _dot(h, wb_ref[:, col0:col0 + ATT_WIDTH])
        for w0 in range(0, ATT_WIDTH, wide):
            zw = z[:, w0:w0 + wide]
            zz = zw * zw
            hi = zz.astype(BF16)
            lo = (zz - hi.astype(F32)).astype(BF16)
            ssq = _dot(hi, bd) + _dot(lo, bd)
            yw = zw * lax.rsqrt(ssq * (1.0 / HEAD_DIM) + RMS_EPS)
            for c in range(wide // LANES):
                y = yw[:, c * LANES:(c + 1) * LANES] * gain
                rot = y * cos + pltpu.roll(y, LANES - ROT_DIM // 2, 1) * sa + pltpu.roll(y, ROT_DIM // 2, 1) * sb
                out_ref[:, w0 + c * LANES:w0 + (c + 1) * LANES] = (rot * scale).astype(BF16)

    qk(0, qg_ref[...], q_ref, HEAD_DIM ** -0.5 * LOG2_E)
    qk(ATT_WIDTH, kg_ref[...], k_ref, 1.0)
    v_ref[...] = _dot(h, wb_ref[:, 2 * ATT_WIDTH:3 * ATT_WIDTH]).astype(BF16)
    u_ref[...] = _dot(h, wb_ref[:, 3 * ATT_WIDTH:]).astype(BF16)


def _in_call(x, g1, w, layer, qg, kg, cos, sa, sb, bd, moe=None):
    t = x.shape[0]
    seq_tiles = cos.shape[0] // ROW_TILE
    row = lambda i: (i, 0)
    fix = lambda i: (0, 0)
    tab = lambda i: (i % seq_tiles, 0)
    out = jax.ShapeDtypeStruct((t, ATT_WIDTH), BF16)
    with_moe = moe is not None
    extra_in = [pl.BlockSpec(memory_space=pltpu.SMEM), pl.BlockSpec(memory_space=pl.ANY)] if with_moe else []
    extra_out = [pl.BlockSpec((ROW_TILE, D_MODEL), row)] if with_moe else []
    extra_shape = [jax.ShapeDtypeStruct((t, D_MODEL), F32)] if with_moe else []
    extra_scratch = [pltpu.VMEM((2 * ROW_TILE * ROW_PITCH, LANES), F32), pltpu.SemaphoreType.DMA((2,))] if with_moe else []
    return pl.pallas_call(
        functools.partial(_in_kernel, with_moe=with_moe),
        grid=(t // ROW_TILE,),
        in_specs=extra_in + [
            pl.BlockSpec((ROW_TILE, D_MODEL), row),
            pl.BlockSpec((1, D_MODEL), fix),
            pl.BlockSpec((None, D_MODEL, IN_WIDTH), lambda i: (layer, 0, 0)),
            pl.BlockSpec((1, LANES), fix),
            pl.BlockSpec((1, LANES), fix),
            pl.BlockSpec((ROW_TILE, LANES), tab),
            pl.BlockSpec((ROW_TILE, LANES), tab),
            pl.BlockSpec((ROW_TILE, LANES), tab),
            pl.BlockSpec(bd.shape, fix),
        ],
        out_specs=[pl.BlockSpec((ROW_TILE, ATT_WIDTH), row)] * 4 + extra_out,
        out_shape=[out] * 4 + extra_shape,
        scratch_shapes=[pltpu.VMEM((D_MODEL, IN_WIDTH), BF16)] + extra_scratch,
        compiler_params=_params("arbitrary"),
        name="in_proj",
    )(*(moe or ()), x, g1, w, qg, kg, cos, sa, sb, bd)


def _attn_bias(q_off, k_idx, with_prev):
    ok = (k_idx >= q_off) & (k_idx <= q_off + ATT_BLK)
    if not with_prev:
        ok = ok & (k_idx >= ATT_BLK)
    return jnp.where(ok, 0.0, NEG_INF).astype(F32)


def _attn_kernel(q_ref, k_ref, v_ref, o_ref, q32, k32, v32, m_s, l_s, acc_s, bias_s):
    n_per = q32.shape[1]
    pad = ATT_BLK
    zeros = jnp.zeros((pad, LANES), F32)
    for r in range(RES):
        rows = pl.ds(r * n_per, n_per)
        q32[r] = q_ref[rows, :].astype(F32)
        k32[r, pl.ds(0, pad), :] = zeros
        v32[r, pl.ds(0, pad), :] = zeros
        k32[r, pl.ds(pad, n_per), :] = k_ref[rows, :].astype(F32)
        v32[r, pl.ds(pad, n_per), :] = v_ref[rows, :].astype(F32)

    qi = lax.broadcasted_iota(I32, (2 * ATT_BLK, 2 * ATT_BLK), 0) & (ATT_BLK - 1)
    kc = lax.broadcasted_iota(I32, (2 * ATT_BLK, 2 * ATT_BLK), 1)
    offs = (
        (16 * (qi & 7) + (qi >> 3), 16 * (kc & 15) + (kc >> 4)),
        (4 * (qi & 31) + (qi >> 5), 4 * (kc & 63) + (kc >> 6)),
        (qi, kc),
    )
    for br, (qo, ko) in enumerate(offs):
        bias_s[br, 0] = _attn_bias(qo, ko, False)
        bias_s[br, 1] = _attn_bias(qo, ko, True)

    head_a = lax.broadcasted_iota(I32, (ATT_BLK, LANES), 1) < HEAD_DIM

    def block(qb, ks, vs, bias):
        qa = jnp.where(head_a, qb, 0.0)
        qq = jnp.concatenate([qa, qb - qa], axis=0).astype(BF16)
        s = _dot_nt(qq, ks.astype(BF16)) + bias
        m = jnp.max(s, axis=1, keepdims=True)
        p = jnp.exp2(s - m)
        l = jnp.sum(p, axis=1, keepdims=True)
        pv = _dot(p.astype(BF16), vs.astype(BF16))
        m2 = jnp.where(head_a, m[:ATT_BLK], m[ATT_BLK:])
        l2 = jnp.where(head_a, l[:ATT_BLK], l[ATT_BLK:])
        pv2 = jnp.where(head_a, pv[:ATT_BLK], pv[ATT_BLK:])
        return m2, l2, pv2

    def store(br, slab, rows, triple, shape=None):
        for ref, val in zip((m_s, l_s, acc_s), triple):
            ref[br, slab, rows, :] = val if shape is None else val.reshape(shape)

    slabs16 = ATT_STEP_BLOCKS // (n_per // ATT_BLK)

    def body16(rr, carry):
        for i in range(slabs16):
            r = slabs16 * rr + i
            for c in range(n_per // ATT_BLK):
                rows = pl.ds(c * ATT_BLK, ATT_BLK)
                keys = pl.ds(c * ATT_BLK, 2 * ATT_BLK)
                store(2, r, rows, block(q32[r, rows, :], k32[r, keys, :], v32[r, keys, :], bias_s[2, min(c, 1)]))
        return carry

    lax.fori_loop(0, RES // slabs16, body16, 0)

    sub = ATT_BLK // 4
    per4 = ATT_STEP_BLOCKS // 4

    def body4(cc, carry):
        for r4 in range(4):
            slabs = [r4 + 4 * m for m in range(4)]
            for c in range(per4):
                first = pl.multiple_of((cc * per4 + c) * sub, sub)
                rows = pl.ds(first, sub)
                keys = pl.ds(first + pad - sub, 2 * sub)
                cat = lambda ref, idx: jnp.concatenate([ref[s, idx, :] for s in slabs], axis=0)
                bias = bias_s[1, jnp.minimum(cc, 1)] if c == 0 else bias_s[1, 1]
                triple = block(cat(q32, rows), cat(k32, keys), cat(v32, keys), bias)
                for j, s in enumerate(slabs):
                    store(1, s, rows, [x[j * sub:(j + 1) * sub] for x in triple])
        return carry

    lax.fori_loop(0, n_per // (per4 * sub), body4, 0)

    sub1 = ATT_BLK // RES

    def body1(jj, carry):
        for g in range(ATT_STEP_BLOCKS):
            first = pl.multiple_of((jj * ATT_STEP_BLOCKS + g) * sub1, sub1)
            rows = pl.ds(first, sub1)
            keys = pl.ds(first + pad - sub1, 2 * sub1)
            bias = bias_s[0, jnp.minimum(jj, 1)] if g == 0 else bias_s[0, 1]
            triple = block(q32[:, rows, :].reshape(ATT_BLK, LANES), k32[:, keys, :].reshape(2 * ATT_BLK, LANES),
                           v32[:, keys, :].reshape(2 * ATT_BLK, LANES), bias)
            store(0, slice(None), rows, triple, (RES, sub1, LANES))
        return carry

    lax.fori_loop(0, n_per // (ATT_STEP_BLOCKS * sub1), body1, 0)

    for r in range(RES):
        ms = [m_s[br, r] for br in range(3)]
        top = jnp.maximum(jnp.maximum(ms[0], ms[1]), ms[2])
        ws = [jnp.exp2(m - top) for m in ms]
        num = sum(w * acc_s[br, r] for br, w in enumerate(ws))
        den = sum(w * l_s[br, r] for br, w in enumerate(ws))
        o_ref[pl.ds(r * n_per, n_per), :] = (num / den).astype(BF16)


def _attn_call(q, k, v, seq):
    t = q.shape[0]
    n_per = seq // RES
    spec = pl.BlockSpec((seq, LANES), lambda b, h: (b, h))
    staged = pltpu.VMEM((RES, n_per, LANES), F32)
    state = pltpu.VMEM((3, RES, n_per, LANES), F32)
    padded = pltpu.VMEM((RES, n_per + ATT_BLK, LANES), F32)
    return pl.pallas_call(
        _attn_kernel,
        grid=(t // seq, ATT_WIDTH // LANES),
        in_specs=[spec, spec, spec],
        out_specs=spec,
        out_shape=jax.ShapeDtypeStruct((t, ATT_WIDTH), BF16),
        scratch_shapes=[staged, padded, padded, state, state, state,
                        pltpu.VMEM((3, 2, 2 * ATT_BLK, 2 * ATT_BLK), F32)],
        compiler_params=_params("parallel", "parallel"),
        name="dilated_attn",
    )(q, k, v)


def _row_min_index(cond, lane_f):
    return jnp.min(jnp.where(cond, lane_f, float(LANES)), axis=1, keepdims=True)


def _out_kernel(att_ref, u_ref, uh_ref, x_ref, wp_ref, ps_ref, wo_ref, g2_ref, rw_ref, br_ref,
                xo_ref, he_ref, bk_ref, cnt_ref, wob_ref):
    i = pl.program_id(1)

    @pl.when((pl.program_id(0) == 0) & (i == 0))
    def _():
        wob_ref[...] = wo_ref[...].astype(BF16)

    nb = u_ref.shape[2]
    rows = RES * nb
    u = u_ref[0].astype(F32)
    halo = jnp.where(i > 0, uh_ref[0][:, -1:, :].astype(F32), 0.0)
    u_prev = jnp.concatenate([halo, u[:, :nb - 1, :]], axis=1)

    n_idx = lax.broadcasted_iota(I32, (RES, nb, POOL_GROUP_DIM), 1) + i * nb
    r_idx = lax.broadcasted_iota(I32, (RES, nb, POOL_GROUP_DIM), 0)
    p1 = (RES * n_idx + r_idx + 1).astype(F32)

    pools = []
    for g, w in enumerate(POOL_WINDOWS):
        lanes = slice(g * POOL_GROUP_DIM, (g + 1) * POOL_GROUP_DIM)
        ug = u[:, :, lanes]
        upg = u_prev[:, :, lanes]
        tot = ug
        for j in range(1, w):
            tot = tot + jnp.concatenate([upg[RES - j:], ug[:RES - j]], axis=0)
        rg = tot / jnp.minimum(p1, float(w)) - ug
        y = _dot(rg.reshape(rows, POOL_GROUP_DIM).astype(BF16), wp_ref[g])
        pools.append((y * ps_ref[:, lanes]).astype(BF16))
    mix = jnp.concatenate([att_ref[0].reshape(rows, ATT_WIDTH)] + pools, axis=1)
    x = x_ref[0].reshape(rows, D_MODEL) + _dot(mix, wob_ref[...])
    xo_ref[0] = x.reshape(RES, nb, D_MODEL)

    ms = jnp.mean(x * x, axis=-1, keepdims=True)
    h = x * lax.rsqrt(ms + RMS_EPS) * g2_ref[...]

    hh = h.astype(BF16)
    hl = (h - hh.astype(F32)).astype(BF16)
    by_hh = _dot(hh, rw_ref[...])
    logits = by_hh[:, :LANES] + (_dot(hl, rw_ref[:, :LANES]) + by_hh[:, LANES:]) + br_ref[...]
    lane = lax.broadcasted_iota(I32, (rows, LANES), 1)
    lane_f = lane.astype(F32)
    is_g = lane < N_EXPERT_GROUPS
    gl = jnp.where(is_g, logits, -jnp.inf)
    gm = jnp.max(gl, axis=1, keepdims=True)
    g_idx = _row_min_index(is_g & (gl == gm), lane_f)
    p_top = 1.0 / jnp.sum(jnp.where(is_g, jnp.exp(logits - gm), 0.0), axis=1, keepdims=True)
    e_lane = lane - N_EXPERT_GROUPS
    in_grp = (e_lane >= 0) & (e_lane < N_EXPERT_GROUPS * EXPERTS_PER_GROUP) & \
             ((e_lane >> 2).astype(F32) == g_idx)
    el = jnp.where(in_grp, logits, -jnp.inf)
    v1 = jnp.max(el, axis=1, keepdims=True)
    i1 = _row_min_index(in_grp & (el == v1), lane_f)
    rest = in_grp & (lane_f != i1)
    el2 = jnp.where(rest, logits, -jnp.inf)
    v2 = jnp.max(el2, axis=1, keepdims=True)
    i2 = _row_min_index(rest & (el2 == v2), lane_f)
    e21 = jnp.exp(v2 - v1)
    w1 = p_top / (1.0 + e21)
    w2 = p_top * e21 / (1.0 + e21)
    a1 = i1 - N_EXPERT_GROUPS - EXPERTS_PER_GROUP * g_idx
    a2 = i2 - N_EXPERT_GROUPS - EXPERTS_PER_GROUP * g_idx
    first_low = a1 < a2
    lo = jnp.where(first_low, a1, a2)
    hi = jnp.where(first_low, a2, a1)
    w_lo = jnp.where(first_low, w1, w2)
    w_hi = jnp.where(first_low, w2, w1)
    bucket = g_idx * PAIRS_PER_GROUP + lo * 3.0 - lo * (lo - 1.0) * 0.5 + hi - lo - 1.0
    meta = jnp.where(lane == 0, w_lo, jnp.where(lane == 1, w_hi, bucket))
    bk_ref[0] = meta.reshape(RES, nb, LANES)
    cnt_ref[0] = jnp.sum((lane_f == bucket).astype(F32), axis=0, keepdims=True)

    for s in range(ROW_PITCH):
        part = h[:, s * LANES:(s + 1) * LANES] if s < SUBLANES else meta
        for r in range(RES):
            he_ref[0, r, _row_part(nb, s), :] = part[r * nb:(r + 1) * nb]


def _out_call(att, u, x, wp, ps, wo, layer, g2, rw, br, seq):
    t = x.shape[0]
    b = t // seq
    n_per = seq // RES
    nb = ROW_TILE // RES
    halo_rows = 16
    v4 = lambda a: a.reshape(b, RES, n_per, a.shape[-1])
    tile = lambda bi, i: (bi, 0, i, 0)
    halo = lambda bi, i: (bi, 0, jnp.maximum(i * (nb // halo_rows) - 1, 0), 0)
    fix2 = lambda bi, i: (0, 0)
    steps = n_per // nb
    xo, he, bk, cnt = pl.pallas_call(
        _out_kernel,
        grid=(b, steps),
        in_specs=[
            pl.BlockSpec((1, RES, nb, ATT_WIDTH), tile),
            pl.BlockSpec((1, RES, nb, POOL_WIDTH), tile),
            pl.BlockSpec((1, RES, halo_rows, POOL_WIDTH), halo),
            pl.BlockSpec((1, RES, nb, D_MODEL), tile),
            pl.BlockSpec((None, POOL_GROUPS, POOL_GROUP_DIM, POOL_GROUP_DIM), lambda bi, i: (layer, 0, 0, 0)),
            pl.BlockSpec((1, POOL_WIDTH), fix2),
            pl.BlockSpec((None, D_MODEL, D_MODEL), lambda bi, i: (layer, 0, 0)),
            pl.BlockSpec((1, D_MODEL), fix2),
            pl.BlockSpec((D_MODEL, 2 * LANES), fix2),
            pl.BlockSpec((1, LANES), fix2),
        ],
        out_specs=[pl.BlockSpec((1, RES, nb, D_MODEL), tile),
                   pl.BlockSpec((1, RES, nb * ROW_PITCH, LANES), tile),
                   pl.BlockSpec((1, RES, nb, LANES), tile),
                   pl.BlockSpec((1, 1, LANES), lambda bi, i: (bi * steps + i, 0, 0))],
        out_shape=[jax.ShapeDtypeStruct((b, RES, n_per, D_MODEL), F32),
                   jax.ShapeDtypeStruct((b, RES, n_per * ROW_PITCH, LANES), F32),
                   jax.ShapeDtypeStruct((b, RES, n_per, LANES), F32),
                   jax.ShapeDtypeStruct((b * steps, 1, LANES), F32)],
        scratch_shapes=[pltpu.VMEM((D_MODEL, D_MODEL), BF16)],
        compiler_params=_params("arbitrary", "arbitrary"),
        name="out_proj_router",
    )(v4(att), v4(u), v4(u), v4(x), wp, ps, wo, g2, rw, br)
    return xo.reshape(t, D_MODEL), he.reshape(t * ROW_PITCH, LANES), bk.reshape(t, LANES), cnt


def _sort_kernel(meta_ref, counts_ref, pos_ref, tinfo_ref, cnt_s, off_s):
    i = pl.program_id(0)
    rows = meta_ref.shape[0]
    lane = lax.broadcasted_iota(I32, (rows, LANES), 1)
    onehot = lane.astype(F32) == meta_ref[:, 2:3]
    oh = onehot.astype(F32)
    tile_count = jnp.sum(oh, axis=0, keepdims=True)

    @pl.when(i == 0)
    def _():
        totals = jnp.sum(counts_ref[...], axis=0)
        tiles = jnp.floor((totals + (MOE_TILE - 1.0)) * (1.0 / MOE_TILE))
        tiles8 = jnp.broadcast_to(tiles, (SUBLANES, LANES)).astype(BF16)
        sq = (LANES, LANES)
        before = lax.broadcasted_iota(I32, sq, 0) < lax.broadcasted_iota(I32, sq, 1)
        start = _dot(tiles8, before.astype(BF16))
        off_s[...] = start[0:1] * float(MOE_TILE)
        cnt_s[...] = jnp.zeros_like(cnt_s)
        end = (start + tiles8.astype(F32)).astype(BF16)
        eye = (lax.broadcasted_iota(I32, sq, 0) == lax.broadcasted_iota(I32, sq, 1)).astype(BF16)
        end_col = _dot_nt(eye, end)[:, 0:1]
        tile_id = lax.broadcasted_iota(I32, sq, 1).astype(F32)
        tile_bucket = jnp.sum((end_col <= tile_id).astype(F32), axis=0, keepdims=True)
        total = jnp.max(end.astype(F32)[0:1], axis=1, keepdims=True)
        row_lane = lax.broadcasted_iota(I32, (1, LANES), 1)
        ends = pltpu.roll(end.astype(F32)[0:1], END_LANE, 1)
        is_end = (row_lane >= END_LANE) & (row_lane < END_LANE + N_BUCKETS)
        tinfo_ref[...] = jnp.where(row_lane == LANES - 1, total, jnp.where(is_end, ends, tile_bucket)).astype(I32)

    sq = (rows, rows)
    upto = lax.broadcasted_iota(I32, sq, 1) <= lax.broadcasted_iota(I32, sq, 0)
    prefix = _dot(upto.astype(BF16), oh.astype(BF16))
    posv = jnp.where(onehot, prefix - 1.0 + cnt_s[...] + off_s[...], 0.0)
    hi = jnp.floor(posv * (1.0 / 256.0))
    lo = posv - hi * 256.0
    ones = jnp.ones((SUBLANES, LANES), BF16)
    pos = _dot_nt(ones, hi.astype(BF16)) * 256.0 + _dot_nt(ones, lo.astype(BF16))
    pos_ref[0] = pos[0:1].astype(I32)
    cnt_s[...] += tile_count


def _sort_call(meta, counts):
    t = meta.shape[0]
    n_tiles = t // ROW_TILE
    return pl.pallas_call(
        _sort_kernel,
        grid=(n_tiles,),
        in_specs=[pl.BlockSpec((ROW_TILE, LANES), lambda i: (i, 0)),
                  pl.BlockSpec(counts.shape, lambda i: (0, 0, 0))],
        out_specs=[pl.BlockSpec((1, 1, ROW_TILE), lambda i: (i, 0, 0)),
                   pl.BlockSpec((1, LANES), lambda i: (0, 0))],
        out_shape=[jax.ShapeDtypeStruct((n_tiles, 1, ROW_TILE), I32),
                   jax.ShapeDtypeStruct((1, LANES), I32)],
        scratch_shapes=[pltpu.VMEM((1, LANES), F32), pltpu.VMEM((1, LANES), F32)],
        compiler_params=_params("arbitrary"),
        name="bucket_sort",
    )(meta, counts)


def _scatter_kernel(tinfo_ref, pos_ref, he_ref, hs_ref, zeros_ref, sem, zsem):
    rows = he_ref.shape[0] // ROW_PITCH

    @pl.when(pl.program_id(0) == 0)
    def _():
        zeros_ref[...] = jnp.zeros_like(zeros_ref)

        def tile_fill(j):
            return pltpu.make_async_copy(zeros_ref, _row_tile(hs_ref, j * MOE_TILE, MOE_TILE), zsem)

        def fill(b):
            end = tinfo_ref[0, END_LANE + b]
            begin = tinfo_ref[0, END_LANE + b - 1] if b else 0
            return end > begin, tile_fill(end - 1)

        unused = tile_fill
        tiles = hs_ref.shape[0] // (MOE_TILE * ROW_PITCH)
        used = tinfo_ref[0, LANES - 1]
        for b in range(N_BUCKETS):
            nonempty, copy = fill(b)
            pl.when(nonempty)(copy.start)
        lax.fori_loop(used, tiles, lambda j, c: (unused(j).start(), c)[1], 0)
        for b in range(N_BUCKETS):
            nonempty, copy = fill(b)
            pl.when(nonempty)(copy.wait)
        lax.fori_loop(used, tiles, lambda j, c: (unused(j).wait(), c)[1], 0)

    def start(g, c):
        for u in range(ISSUE_UNROLL):
            t = g * ISSUE_UNROLL + u
            pltpu.make_async_copy(_row_tile(he_ref, t), _row_tile(hs_ref, pos_ref[0, 0, t]),
                                  sem).start(priority=u % 2)
        return c

    lax.fori_loop(0, rows // ISSUE_UNROLL, start, 0)
    pltpu.make_async_copy(he_ref, _row_tile(hs_ref, 0, rows), sem).wait()


def _scatter_call(tinfo, pos, he, sorted_rows):
    t = he.shape[0] // ROW_PITCH
    return pl.pallas_call(
        _scatter_kernel,
        grid=(t // ROW_TILE,),
        in_specs=[pl.BlockSpec(memory_space=pltpu.SMEM),
                  pl.BlockSpec((1, 1, ROW_TILE), lambda i: (i, 0, 0), memory_space=pltpu.SMEM),
                  pl.BlockSpec((ROW_TILE * ROW_PITCH, LANES), lambda i: (i, 0))],
        out_specs=pl.BlockSpec(memory_space=pl.ANY),
        out_shape=jax.ShapeDtypeStruct((sorted_rows * ROW_PITCH, LANES), F32),
        scratch_shapes=[pltpu.VMEM((MOE_TILE * ROW_PITCH, LANES), F32), pltpu.SemaphoreType.DMA(()),
                        pltpu.SemaphoreType.DMA(())],
        compiler_params=_params("arbitrary"),
        name="row_scatter",
    )(tinfo, pos, he)


def _gather_kernel(pos_ref, x_ref, ys_ref, o_ref, buf, sem):
    rows = buf.shape[0] // ROW_PITCH
    _issue_row_gather(ys_ref, pos_ref, 0, buf, 0, sem)
    pltpu.make_async_copy(_row_tile(ys_ref, 0, rows), buf, sem).wait()
    for s in range(SUBLANES):
        lanes = slice(s * LANES, (s + 1) * LANES)
        o_ref[:, lanes] = x_ref[:, lanes] + buf[_row_part(rows, s), :]


def _gather_call(pos, x, ys):
    t = x.shape[0]
    return pl.pallas_call(
        _gather_kernel,
        grid=(t // ROW_TILE,),
        in_specs=[pl.BlockSpec((1, 1, ROW_TILE), lambda i: (i, 0, 0), memory_space=pltpu.SMEM),
                  pl.BlockSpec((ROW_TILE, D_MODEL), lambda i: (i, 0)),
                  pl.BlockSpec(memory_space=pl.ANY)],
        out_specs=pl.BlockSpec((ROW_TILE, D_MODEL), lambda i: (i, 0)),
        out_shape=jax.ShapeDtypeStruct((t, D_MODEL), F32),
        scratch_shapes=[pltpu.VMEM((ROW_TILE * ROW_PITCH, LANES), F32), pltpu.SemaphoreType.DMA(())],
        compiler_params=_params("arbitrary"),
        name="row_gather_residual",
    )(pos, x, ys)


def _tile_group(j, tinfo):
    used = tinfo[LANES - 1]
    return tinfo[jnp.minimum(j, used - 1)] // PAIRS_PER_GROUP


def _moe_kernel(tinfo, hs_ref, wg32_ref, wu32_ref, wd32_ref, ys_ref, wg_ref, wu_ref, wd_ref):
    j = pl.program_id(0)
    used = tinfo[LANES - 1]

    @pl.when((j == 0) | ((j < used) & (_tile_group(j, tinfo) != _tile_group(jnp.maximum(j, 1) - 1, tinfo))))
    def _():
        wg_ref[...] = wg32_ref[...].astype(BF16)
        wu_ref[...] = wu32_ref[...].astype(BF16)
        wd_ref[...] = wd32_ref[...].astype(BF16)

    @pl.when(j < used)
    def _():
        pair = tinfo[j] % PAIRS_PER_GROUP
        e_lo = (pair >= 3).astype(I32) + (pair >= 5).astype(I32)
        e_hi = pair - (e_lo * 3 - (e_lo * (e_lo - 1)) // 2) + e_lo + 1
        xt = jnp.concatenate([hs_ref[_row_part(MOE_TILE, s), :].astype(BF16) for s in range(SUBLANES)], axis=1)

        scalars = hs_ref[_row_part(MOE_TILE, SUBLANES), :]

        def expert(e, gate):
            hg = _dot(xt, wg_ref[e])
            hu = _dot(xt, wu_ref[e])
            act = hg * (1.0 / (1.0 + jnp.exp(-hg))) * hu * gate
            return _dot(act.astype(BF16), wd_ref[e])

        y = expert(e_lo, scalars[:, 0:1]) + expert(e_hi, scalars[:, 1:2])
        for s in range(SUBLANES):
            ys_ref[_row_part(MOE_TILE, s), :] = y[:, s * LANES:(s + 1) * LANES]
        ys_ref[_row_part(MOE_TILE, SUBLANES), :] = jnp.zeros((MOE_TILE, LANES), F32)

    @pl.when(j >= used)
    def _():
        ys_ref[...] = jnp.zeros_like(ys_ref)


def _moe_call(tinfo, hs, wg, wu, wd, layer):
    sorted_rows = hs.shape[0] // ROW_PITCH
    row = lambda j, ti: (jnp.minimum(j, ti[LANES - 1] - 1), 0)
    grp = lambda j, ti: (layer, _tile_group(j, ti), 0, 0, 0)
    return pl.pallas_call(
        _moe_kernel,
        grid_spec=pltpu.PrefetchScalarGridSpec(
            num_scalar_prefetch=1,
            grid=(sorted_rows // MOE_TILE,),
            in_specs=[pl.BlockSpec((MOE_TILE * ROW_PITCH, LANES), row),
                      pl.BlockSpec((None, None, EXPERTS_PER_GROUP, D_MODEL, D_EXPERT), grp),
                      pl.BlockSpec((None, None, EXPERTS_PER_GROUP, D_MODEL, D_EXPERT), grp),
                      pl.BlockSpec((None, None, EXPERTS_PER_GROUP, D_EXPERT, D_MODEL), grp)],
            out_specs=pl.BlockSpec((MOE_TILE * ROW_PITCH, LANES), lambda j, ti: (j, 0)),
            scratch_shapes=[pltpu.VMEM((EXPERTS_PER_GROUP, D_MODEL, D_EXPERT), BF16),
                            pltpu.VMEM((EXPERTS_PER_GROUP, D_MODEL, D_EXPERT), BF16),
                            pltpu.VMEM((EXPERTS_PER_GROUP, D_EXPERT, D_MODEL), BF16)],
        ),
        out_shape=jax.ShapeDtypeStruct((sorted_rows * ROW_PITCH, LANES), F32),
        compiler_params=_params("arbitrary"),
        name="grouped_experts",
    )(tinfo, hs, wg, wu, wd)


def _rope_tables(seq):
    half = ROT_DIM // 2
    inv_freq = ROPE_THETA ** (-jnp.arange(0, ROT_DIM, 2, dtype=F32) / ROT_DIM)
    row = jnp.arange(seq)
    pos = (RES * (row % (seq // RES)) + row // (seq // RES)).astype(F32)
    ang = pos[:, None] * inv_freq[None, :]
    cos, sin = jnp.cos(ang), jnp.sin(ang)
    d = jnp.arange(LANES) % HEAD_DIM
    cos_l = jnp.where(d[None, :] < ROT_DIM, cos[:, d % half], 1.0)
    sin_l = sin[:, d % half]
    sa = jnp.where(d[None, :] < half, -sin_l, 0.0)
    sb = jnp.where((d[None, :] >= half) & (d[None, :] < ROT_DIM), sin_l, 0.0)
    return cos_l, sa, sb


def kernel(x, norm1_gain, w_in, q_norm_gain, k_norm_gain, w_pool, pool_scale, w_out, norm2_gain, w_group, b_group, w_router, b_router, w_gate, w_up, w_down):
    b, seq, d = x.shape
    depth = w_in.shape[0]
    t = b * seq
    n_per = seq // RES
    assert d == D_MODEL and seq % (RES * ATT_BLK) == 0 and t % ROW_TILE == 0
    sorted_rows = t + N_BUCKETS * MOE_TILE

    cos, sa, sb = _rope_tables(seq)
    lane_head = jnp.arange(MXU_WIDTH) // HEAD_DIM
    block_diag = (lane_head[:, None] == lane_head[None, :]).astype(BF16)

    w_pool_b = w_pool.astype(BF16)
    grouped = lambda w: w.reshape(depth, N_EXPERT_GROUPS, EXPERTS_PER_GROUP, *w.shape[2:])
    w_gate_g, w_up_g, w_down_g = grouped(w_gate), grouped(w_up), grouped(w_down)
    n_logits = N_EXPERT_GROUPS * (1 + EXPERTS_PER_GROUP)
    w_r = jnp.pad(jnp.concatenate([w_group, w_router], axis=-1), ((0, 0), (0, 0), (0, LANES - n_logits)))
    w_r_hi = w_r.astype(BF16)
    w_r_lo = (w_r - w_r_hi.astype(F32)).astype(BF16)
    w_r_split = jnp.concatenate([w_r_hi, w_r_lo], axis=-1)
    b_r = jnp.pad(jnp.concatenate([b_group, b_router], axis=-1), ((0, 0), (0, LANES - n_logits)))
    two_heads = lambda g: jnp.tile(g, (1, LANES // HEAD_DIM))

    xr = x.reshape(b, n_per, RES, d).transpose(0, 2, 1, 3).reshape(t, d)
    moe = None
    for l in range(depth):
        q, k, v, u, *x_new = _in_call(xr, norm1_gain[l:l + 1], w_in, l, two_heads(q_norm_gain[l:l + 1]),
                                      two_heads(k_norm_gain[l:l + 1]), cos, sa, sb, block_diag, moe)
        xr = x_new[0] if x_new else xr
        att = _attn_call(q, k, v, seq)
        xr, he, meta, counts = _out_call(att, u, xr, w_pool_b, pool_scale[l:l + 1], w_out, l, norm2_gain[l:l + 1],
                                         w_r_split[l], b_r[l:l + 1], seq)
        pos, tinfo = _sort_call(meta, counts)
        hs = _scatter_call(tinfo, pos, he, sorted_rows)
        ys = _moe_call(tinfo.reshape(LANES), hs, w_gate_g, w_up_g, w_down_g, l)
        moe = (pos, ys)
    xr = _gather_call(pos, xr, ys)
    return xr.reshape(b, RES, n_per, d).transpose(0, 2, 1, 3).reshape(b, seq, d)
```

```python
import functools

import jax
import jax.numpy as jnp
from jax import lax
from jax.experimental import pallas as pl
from jax.experimental.pallas import tpu as pltpu

D_MODEL = 1024
N_HEADS = 8
HEAD_DIM = 64
ATT_WIDTH = N_HEADS * HEAD_DIM
POOL_GROUPS = 4
POOL_GROUP_DIM = 128
POOL_WIDTH = POOL_GROUPS * POOL_GROUP_DIM
POOL_WINDOWS = (2, 4, 8, 16)
IN_WIDTH = 3 * ATT_WIDTH + POOL_WIDTH
ROT_DIM = 16
ROPE_THETA = 500000.0
N_EXPERT_GROUPS = 4
EXPERTS_PER_GROUP = 4
D_EXPERT = 256
RMS_EPS = 1e-6
NEG_INF = -1e30
LOG2_E = 1.4426950408889634

LANES = 128
SUBLANES = 8
MXU_WIDTH = 256
RES = 16
ATT_BLK = 128
ATT_STEP_BLOCKS = 32
PAIRS_PER_GROUP = 6
N_BUCKETS = N_EXPERT_GROUPS * PAIRS_PER_GROUP
ROW_TILE = 512
MOE_TILE = 256
ISSUE_UNROLL = 8
ROW_PITCH = SUBLANES + 1
END_LANE = 96
VMEM_LIMIT = 56 * 1024 * 1024

F32 = jnp.float32
BF16 = jnp.bfloat16
I32 = jnp.int32


def _dot(a, b):
    return jnp.dot(a, b, preferred_element_type=F32)


def _dot_nt(a, b):
    return lax.dot_general(a, b, (((1,), (1,)), ((), ())), preferred_element_type=F32)


def _row_tile(ref, row, n=1):
    return ref.at[pl.ds(row * ROW_PITCH, n * ROW_PITCH), :]


def _row_part(n, s):
    return pl.ds(s, n, stride=ROW_PITCH)


def _params(*sem):
    return pltpu.CompilerParams(dimension_semantics=sem, vmem_limit_bytes=VMEM_LIMIT)


def _issue_row_gather(ys_ref, pos_ref, tile, buf, first, sem):
    def start(g, c):
        for u in range(ISSUE_UNROLL):
            t = g * ISSUE_UNROLL + u
            pltpu.make_async_copy(_row_tile(ys_ref, pos_ref[tile, 0, t]), _row_tile(buf, first + t),
                                  sem).start(priority=u % 2)
        return c

    lax.fori_loop(0, ROW_TILE // ISSUE_UNROLL, start, 0)


def _in_kernel(*refs, with_moe):
    if with_moe:
        pos_ref, ys_ref, refs = refs[0], refs[1], refs[2:]
        xo_ref, buf, sem = refs[-4], refs[-2], refs[-1]
        refs = refs[:-4] + (refs[-3],)
    x_ref, g1_ref, w_ref, qg_ref, kg_ref, cos_ref, sa_ref, sb_ref, bd_ref, q_ref, k_ref, v_ref, u_ref, wb_ref = refs
    i = pl.program_id(0)

    if with_moe:
        slot_rows = ROW_TILE
        @pl.when(i == 0)
        def _():
            _issue_row_gather(ys_ref, pos_ref, 0, buf, 0, sem.at[0])

        @pl.when(i + 1 < pl.num_programs(0))
        def _():
            _issue_row_gather(ys_ref, pos_ref, i + 1, buf, ((i + 1) % 2) * slot_rows, sem.at[(i + 1) % 2])

    @pl.when(i == 0)
    def _():
        wb_ref[...] = w_ref[...].astype(BF16)

    if with_moe:
        first = (i % 2) * slot_rows
        pltpu.make_async_copy(_row_tile(ys_ref, 0, slot_rows), _row_tile(buf, first, slot_rows), sem.at[i % 2]).wait()
        x = jnp.concatenate([x_ref[:, s * LANES:(s + 1) * LANES]
                             + buf[pl.ds(first * ROW_PITCH + s, slot_rows, stride=ROW_PITCH), :]
                             for s in range(SUBLANES)], axis=1)
        xo_ref[...] = x
    else:
        x = x_ref[...]
    ms = jnp.mean(x * x, axis=-1, keepdims=True)
    h = (x * lax.rsqrt(ms + RMS_EPS) * g1_ref[...]).astype(BF16)
    cos = cos_ref[...]
    sa = sa_ref[...]
    sb = sb_ref[...]
    bd = bd_ref[...]
    wide = bd.shape[0]

    def qk(col0, gain, out_ref, scale):
        z = _dot(h, wb_ref[:, col0:col0 + ATT_WIDTH])
        for w0 in range(0, ATT_WIDTH, wide):
            zw = z[:, w0:w0 + wide]
            zz = zw * zw
            hi = zz.astype(BF16)
            lo = (zz - hi.astype(F32)).astype(BF16)
            ssq = _dot(hi, bd) + _dot(lo, bd)
            yw = zw * lax.rsqrt(ssq * (1.0 / HEAD_DIM) + RMS_EPS)
            for c in range(wide // LANES):
                y = yw[:, c * LANES:(c + 1) * LANES] * gain
                rot = y * cos + pltpu.roll(y, LANES - ROT_DIM // 2, 1) * sa + pltpu.roll(y, ROT_DIM // 2, 1) * sb
                out_ref[:, w0 + c * LANES:w0 + (c + 1) * LANES] = (rot * scale).astype(BF16)

    qk(0, qg_ref[...], q_ref, HEAD_DIM ** -0.5 * LOG2_E)
    qk(ATT_WIDTH, kg_ref[...], k_ref, 1.0)
    v_ref[...] = _dot(h, wb_ref[:, 2 * ATT_WIDTH:3 * ATT_WIDTH]).astype(BF16)
    u_ref[...] = _dot(h, wb_ref[:, 3 * ATT_WIDTH:]).astype(BF16)


def _in_call(x, g1, w, layer, qg, kg, cos, sa, sb, bd, moe=None):
    t = x.shape[0]
    seq_tiles = cos.shape[0] // ROW_TILE
    row = lambda i: (i, 0)
    fix = lambda i: (0, 0)
    tab = lambda i: (i % seq_tiles, 0)
    out = jax.ShapeDtypeStruct((t, ATT_WIDTH), BF16)
    with_moe = moe is not None
    extra_in = [pl.BlockSpec(memory_space=pltpu.SMEM), pl.BlockSpec(memory_space=pl.ANY)] if with_moe else []
    extra_out = [pl.BlockSpec((ROW_TILE, D_MODEL), row)] if with_moe else []
    extra_shape = [jax.ShapeDtypeStruct((t, D_MODEL), F32)] if with_moe else []
    extra_scratch = [pltpu.VMEM((2 * ROW_TILE * ROW_PITCH, LANES), F32), pltpu.SemaphoreType.DMA((2,))] if with_moe else []
    return pl.pallas_call(
        functools.partial(_in_kernel, with_moe=with_moe),
        grid=(t // ROW_TILE,),
        in_specs=extra_in + [
            pl.BlockSpec((ROW_TILE, D_MODEL), row),
            pl.BlockSpec((1, D_MODEL), fix),
            pl.BlockSpec((None, D_MODEL, IN_WIDTH), lambda i: (layer, 0, 0)),
            pl.BlockSpec((1, LANES), fix),
            pl.BlockSpec((1, LANES), fix),
            pl.BlockSpec((ROW_TILE, LANES), tab),
            pl.BlockSpec((ROW_TILE, LANES), tab),
            pl.BlockSpec((ROW_TILE, LANES), tab),
            pl.BlockSpec(bd.shape, fix),
        ],
        out_specs=[pl.BlockSpec((ROW_TILE, ATT_WIDTH), row)] * 4 + extra_out,
        out_shape=[out] * 4 + extra_shape,
        scratch_shapes=[pltpu.VMEM((D_MODEL, IN_WIDTH), BF16)] + extra_scratch,
        compiler_params=_params("arbitrary"),
        name="in_proj",
    )(*(moe or ()), x, g1, w, qg, kg, cos, sa, sb, bd)


def _attn_bias(q_off, k_idx, with_prev):
    ok = (k_idx >= q_off) & (k_idx <= q_off + ATT_BLK)
    if not with_prev:
        ok = ok & (k_idx >= ATT_BLK)
    return jnp.where(ok, 0.0, NEG_INF).astype(F32)


def _attn_kernel(q_ref, k_ref, v_ref, o_ref, q32, k32, v32, m_s, l_s, acc_s, bias_s):
    n_per = q32.shape[1]
    pad = ATT_BLK
    zeros = jnp.zeros((pad, LANES), F32)
    for r in range(RES):
        rows = pl.ds(r * n_per, n_per)
        q32[r] = q_ref[rows, :].astype(F32)
        k32[r, pl.ds(0, pad), :] = zeros
        v32[r, pl.ds(0, pad), :] = zeros
        k32[r, pl.ds(pad, n_per), :] = k_ref[rows, :].astype(F32)
        v32[r, pl.ds(pad, n_per), :] = v_ref[rows, :].astype(F32)

    @pl.when((pl.program_id(0) == 0) & (pl.program_id(1) == 0))
    def _():
        qi = lax.broadcasted_iota(I32, (2 * ATT_BLK, 2 * ATT_BLK), 0) & (ATT_BLK - 1)
        kc = lax.broadcasted_iota(I32, (2 * ATT_BLK, 2 * ATT_BLK), 1)
        offs = (
            (16 * (qi & 7) + (qi >> 3), 16 * (kc & 15) + (kc >> 4)),
            (4 * (qi & 31) + (qi >> 5), 4 * (kc & 63) + (kc >> 6)),
            (qi, kc),
        )
        for br, (qo, ko) in enumerate(offs):
            bias_s[br, 0] = _attn_bias(qo, ko, False)
            bias_s[br, 1] = _attn_bias(qo, ko, True)

    head_a = lax.broadcasted_iota(I32, (ATT_BLK, LANES), 1) < HEAD_DIM

    def block(qb, ks, vs, bias):
        qa = jnp.where(head_a, qb, 0.0)
        qq = jnp.concatenate([qa, qb - qa], axis=0).astype(BF16)
        s = _dot_nt(qq, ks.astype(BF16)) + bias
        m = jnp.max(s, axis=1, keepdims=True)
        p = jnp.exp2(s - m)
        l = jnp.sum(p, axis=1, keepdims=True)
        pv = _dot(p.astype(BF16), vs.astype(BF16))
        m2 = jnp.where(head_a, m[:ATT_BLK], m[ATT_BLK:])
        l2 = jnp.where(head_a, l[:ATT_BLK], l[ATT_BLK:])
        pv2 = jnp.where(head_a, pv[:ATT_BLK], pv[ATT_BLK:])
        return m2, l2, pv2

    def store(br, slab, rows, triple, shape=None):
        for ref, val in zip((m_s, l_s, acc_s), triple):
            ref[br, slab, rows, :] = val if shape is None else val.reshape(shape)

    slabs16 = ATT_STEP_BLOCKS // (n_per // ATT_BLK)

    def body16(rr, carry):
        for i in range(slabs16):
            r = slabs16 * rr + i
            for c in range(n_per // ATT_BLK):
                rows = pl.ds(c * ATT_BLK, ATT_BLK)
                keys = pl.ds(c * ATT_BLK, 2 * ATT_BLK)
                store(2, r, rows, block(q32[r, rows, :], k32[r, keys, :], v32[r, keys, :], bias_s[2, min(c, 1)]))
        return carry

    lax.fori_loop(0, RES // slabs16, body16, 0)

    sub = ATT_BLK // 4
    per4 = ATT_STEP_BLOCKS // 4

    def body4(cc, carry):
        for r4 in range(4):
            slabs = [r4 + 4 * m for m in range(4)]
            for c in range(per4):
                first = pl.multiple_of((cc * per4 + c) * sub, sub)
                rows = pl.ds(first, sub)
                keys = pl.ds(first + pad - sub, 2 * sub)
                cat = lambda ref, idx: jnp.concatenate([ref[s, idx, :] for s in slabs], axis=0)
                bias = bias_s[1, jnp.minimum(cc, 1)] if c == 0 else bias_s[1, 1]
                triple = block(cat(q32, rows), cat(k32, keys), cat(v32, keys), bias)
                for j, s in enumerate(slabs):
                    store(1, s, rows, [x[j * sub:(j + 1) * sub] for x in triple])
        return carry

    lax.fori_loop(0, n_per // (per4 * sub), body4, 0)

    sub1 = ATT_BLK // RES

    def body1(jj, carry):
        for g in range(ATT_STEP_BLOCKS):
            first = pl.multiple_of((jj * ATT_STEP_BLOCKS + g) * sub1, sub1)
            rows = pl.ds(first, sub1)
            keys = pl.ds(first + pad - sub1, 2 * sub1)
            bias = bias_s[0, jnp.minimum(jj, 1)] if g == 0 else bias_s[0, 1]
            triple = block(q32[:, rows, :].reshape(ATT_BLK, LANES), k32[:, keys, :].reshape(2 * ATT_BLK, LANES),
                           v32[:, keys, :].reshape(2 * ATT_BLK, LANES), bias)
            store(0, slice(None), rows, triple, (RES, sub1, LANES))
        return carry

    lax.fori_loop(0, n_per // (ATT_STEP_BLOCKS * sub1), body1, 0)

    for r in range(RES):
        ms = [m_s[br, r] for br in range(3)]
        top = jnp.maximum(jnp.maximum(ms[0], ms[1]), ms[2])
        ws = [jnp.exp2(m - top) for m in ms]
        num = sum(w * acc_s[br, r] for br, w in enumerate(ws))
        den = sum(w * l_s[br, r] for br, w in enumerate(ws))
        o_ref[pl.ds(r * n_per, n_per), :] = (num / den).astype(BF16)


def _attn_call(q, k, v, seq):
    t = q.shape[0]
    n_per = seq // RES
    spec = pl.BlockSpec((seq, LANES), lambda b, h: (b, h))
    staged = pltpu.VMEM((RES, n_per, LANES), F32)
    state = pltpu.VMEM((3, RES, n_per, LANES), F32)
    padded = pltpu.VMEM((RES, n_per + ATT_BLK, LANES), F32)
    return pl.pallas_call(
        _attn_kernel,
        grid=(t // seq, ATT_WIDTH // LANES),
        in_specs=[spec, spec, spec],
        out_specs=spec,
        out_shape=jax.ShapeDtypeStruct((t, ATT_WIDTH), BF16),
        scratch_shapes=[staged, padded, padded, state, state, state,
                        pltpu.VMEM((3, 2, 2 * ATT_BLK, 2 * ATT_BLK), F32)],
        compiler_params=_params("arbitrary", "arbitrary"),
        name="dilated_attn",
    )(q, k, v)


def _row_min_index(cond, lane_f):
    return jnp.min(jnp.where(cond, lane_f, float(LANES)), axis=1, keepdims=True)


def _out_kernel(att_ref, u_ref, uh_ref, x_ref, wp_ref, ps_ref, wo_ref, g2_ref, rw_ref, br_ref,
                xo_ref, he_ref, bk_ref, cnt_ref, wob_ref):
    i = pl.program_id(1)

    @pl.when((pl.program_id(0) == 0) & (i == 0))
    def _():
        wob_ref[...] = wo_ref[...].astype(BF16)

    nb = u_ref.shape[2]
    rows = RES * nb
    u = u_ref[0].astype(F32)
    halo = jnp.where(i > 0, uh_ref[0][:, -1:, :].astype(F32), 0.0)
    u_prev = jnp.concatenate([halo, u[:, :nb - 1, :]], axis=1)

    n_idx = lax.broadcasted_iota(I32, (RES, nb, POOL_GROUP_DIM), 1) + i * nb
    r_idx = lax.broadcasted_iota(I32, (RES, nb, POOL_GROUP_DIM), 0)
    p1 = (RES * n_idx + r_idx + 1).astype(F32)

    pools = []
    for g, w in enumerate(POOL_WINDOWS):
        lanes = slice(g * POOL_GROUP_DIM, (g + 1) * POOL_GROUP_DIM)
        ug = u[:, :, lanes]
        upg = u_prev[:, :, lanes]
        tot = ug
        for j in range(1, w):
            tot = tot + jnp.concatenate([upg[RES - j:], ug[:RES - j]], axis=0)
        rg = tot / jnp.minimum(p1, float(w)) - ug
        y = _dot(rg.reshape(rows, POOL_GROUP_DIM).astype(BF16), wp_ref[g])
        pools.append((y * ps_ref[:, lanes]).astype(BF16))
    mix = jnp.concatenate([att_ref[0].reshape(rows, ATT_WIDTH)] + pools, axis=1)
    x = x_ref[0].reshape(rows, D_MODEL) + _dot(mix, wob_ref[...])
    xo_ref[0] = x.reshape(RES, nb, D_MODEL)

    ms = jnp.mean(x * x, axis=-1, keepdims=True)
    h = x * lax.rsqrt(ms + RMS_EPS) * g2_ref[...]

    hh = h.astype(BF16)
    hl = (h - hh.astype(F32)).astype(BF16)
    by_hh = _dot(hh, rw_ref[...])
    logits = by_hh[:, :LANES] + (_dot(hl, rw_ref[:, :LANES]) + by_hh[:, LANES:]) + br_ref[...]
    lane = lax.broadcasted_iota(I32, (rows, LANES), 1)
    lane_f = lane.astype(F32)
    is_g = lane < N_EXPERT_GROUPS
    gl = jnp.where(is_g, logits, -jnp.inf)
    gm = jnp.max(gl, axis=1, keepdims=True)
    g_idx = _row_min_index(is_g & (gl == gm), lane_f)
    p_top = 1.0 / jnp.sum(jnp.where(is_g, jnp.exp(logits - gm), 0.0), axis=1, keepdims=True)
    e_lane = lane - N_EXPERT_GROUPS
    in_grp = (e_lane >= 0) & (e_lane < N_EXPERT_GROUPS * EXPERTS_PER_GROUP) & \
             ((e_lane >> 2).astype(F32) == g_idx)
    el = jnp.where(in_grp, logits, -jnp.inf)
    v1 = jnp.max(el, axis=1, keepdims=True)
    i1 = _row_min_index(in_grp & (el == v1), lane_f)
    rest = in_grp & (lane_f != i1)
    el2 = jnp.where(rest, logits, -jnp.inf)
    v2 = jnp.max(el2, axis=1, keepdims=True)
    i2 = _row_min_index(rest & (el2 == v2), lane_f)
    e21 = jnp.exp(v2 - v1)
    w1 = p_top / (1.0 + e21)
    w2 = p_top * e21 / (1.0 + e21)
    a1 = i1 - N_EXPERT_GROUPS - EXPERTS_PER_GROUP * g_idx
    a2 = i2 - N_EXPERT_GROUPS - EXPERTS_PER_GROUP * g_idx
    first_low = a1 < a2
    lo = jnp.where(first_low, a1, a2)
    hi = jnp.where(first_low, a2, a1)
    w_lo = jnp.where(first_low, w1, w2)
    w_hi = jnp.where(first_low, w2, w1)
    bucket = g_idx * PAIRS_PER_GROUP + lo * 3.0 - lo * (lo - 1.0) * 0.5 + hi - lo - 1.0
    meta = jnp.where(lane == 0, w_lo, jnp.where(lane == 1, w_hi, bucket))
    bk_ref[0] = meta.reshape(RES, nb, LANES)
    cnt_ref[0] = jnp.sum((lane_f == bucket).astype(F32), axis=0, keepdims=True)

    for s in range(ROW_PITCH):
        part = h[:, s * LANES:(s + 1) * LANES] if s < SUBLANES else meta
        for r in range(RES):
            he_ref[0, r, _row_part(nb, s), :] = part[r * nb:(r + 1) * nb]


def _out_call(att, u, x, wp, ps, wo, layer, g2, rw, br, seq):
    t = x.shape[0]
    b = t // seq
    n_per = seq // RES
    nb = ROW_TILE // RES
    halo_rows = 16
    v4 = lambda a: a.reshape(b, RES, n_per, a.shape[-1])
    tile = lambda bi, i: (bi, 0, i, 0)
    halo = lambda bi, i: (bi, 0, jnp.maximum(i * (nb // halo_rows) - 1, 0), 0)
    fix2 = lambda bi, i: (0, 0)
    steps = n_per // nb
    xo, he, bk, cnt = pl.pallas_call(
        _out_kernel,
        grid=(b, steps),
        in_specs=[
            pl.BlockSpec((1, RES, nb, ATT_WIDTH), tile),
            pl.BlockSpec((1, RES, nb, POOL_WIDTH), tile),
            pl.BlockSpec((1, RES, halo_rows, POOL_WIDTH), halo),
            pl.BlockSpec((1, RES, nb, D_MODEL), tile),
            pl.BlockSpec((None, POOL_GROUPS, POOL_GROUP_DIM, POOL_GROUP_DIM), lambda bi, i: (layer, 0, 0, 0)),
            pl.BlockSpec((1, POOL_WIDTH), fix2),
            pl.BlockSpec((None, D_MODEL, D_MODEL), lambda bi, i: (layer, 0, 0)),
            pl.BlockSpec((1, D_MODEL), fix2),
            pl.BlockSpec((D_MODEL, 2 * LANES), fix2),
            pl.BlockSpec((1, LANES), fix2),
        ],
        out_specs=[pl.BlockSpec((1, RES, nb, D_MODEL), tile),
                   pl.BlockSpec((1, RES, nb * ROW_PITCH, LANES), tile),
                   pl.BlockSpec((1, RES, nb, LANES), tile),
                   pl.BlockSpec((1, 1, LANES), lambda bi, i: (bi * steps + i, 0, 0))],
        out_shape=[jax.ShapeDtypeStruct((b, RES, n_per, D_MODEL), F32),
                   jax.ShapeDtypeStruct((b, RES, n_per * ROW_PITCH, LANES), F32),
                   jax.ShapeDtypeStruct((b, RES, n_per, LANES), F32),
                   jax.ShapeDtypeStruct((b * steps, 1, LANES), F32)],
        scratch_shapes=[pltpu.VMEM((D_MODEL, D_MODEL), BF16)],
        compiler_params=_params("arbitrary", "arbitrary"),
        name="out_proj_router",
    )(v4(att), v4(u), v4(u), v4(x), wp, ps, wo, g2, rw, br)
    return xo.reshape(t, D_MODEL), he.reshape(t * ROW_PITCH, LANES), bk.reshape(t, LANES), cnt


def _sort_kernel(meta_ref, counts_ref, pos_ref, tinfo_ref, cnt_s, off_s):
    i = pl.program_id(0)
    rows = meta_ref.shape[0]
    lane = lax.broadcasted_iota(I32, (rows, LANES), 1)
    onehot = lane.astype(F32) == meta_ref[:, 2:3]
    oh = onehot.astype(F32)
    tile_count = jnp.sum(oh, axis=0, keepdims=True)

    @pl.when(i == 0)
    def _():
        totals = jnp.sum(counts_ref[...], axis=0)
        tiles = jnp.floor((totals + (MOE_TILE - 1.0)) * (1.0 / MOE_TILE))
        tiles8 = jnp.broadcast_to(tiles, (SUBLANES, LANES)).astype(BF16)
        sq = (LANES, LANES)
        before = lax.broadcasted_iota(I32, sq, 0) < lax.broadcasted_iota(I32, sq, 1)
        start = _dot(tiles8, before.astype(BF16))
        off_s[...] = start[0:1] * float(MOE_TILE)
        cnt_s[...] = jnp.zeros_like(cnt_s)
        end = (start + tiles8.astype(F32)).astype(BF16)
        eye = (lax.broadcasted_iota(I32, sq, 0) == lax.broadcasted_iota(I32, sq, 1)).astype(BF16)
        end_col = _dot_nt(eye, end)[:, 0:1]
        tile_id = lax.broadcasted_iota(I32, sq, 1).astype(F32)
        tile_bucket = jnp.sum((end_col <= tile_id).astype(F32), axis=0, keepdims=True)
        total = jnp.max(end.astype(F32)[0:1], axis=1, keepdims=True)
        row_lane = lax.broadcasted_iota(I32, (1, LANES), 1)
        ends = pltpu.roll(end.astype(F32)[0:1], END_LANE, 1)
        is_end = (row_lane >= END_LANE) & (row_lane < END_LANE + N_BUCKETS)
        tinfo_ref[...] = jnp.where(row_lane == LANES - 1, total, jnp.where(is_end, ends, tile_bucket)).astype(I32)

    sq = (rows, rows)
    upto = lax.broadcasted_iota(I32, sq, 1) <= lax.broadcasted_iota(I32, sq, 0)
    prefix = _dot(upto.astype(BF16), oh.astype(BF16))
    posv = jnp.where(onehot, prefix - 1.0 + cnt_s[...] + off_s[...], 0.0)
    hi = jnp.floor(posv * (1.0 / 256.0))
    lo = posv - hi * 256.0
    ones = jnp.ones((SUBLANES, LANES), BF16)
    pos = _dot_nt(ones, hi.astype(BF16)) * 256.0 + _dot_nt(ones, lo.astype(BF16))
    pos_ref[0] = pos[0:1].astype(I32)
    cnt_s[...] += tile_count


def _sort_call(meta, counts):
    t = meta.shape[0]
    n_tiles = t // ROW_TILE
    return pl.pallas_call(
        _sort_kernel,
        grid=(n_tiles,),
        in_specs=[pl.BlockSpec((ROW_TILE, LANES), lambda i: (i, 0)),
                  pl.BlockSpec(counts.shape, lambda i: (0, 0, 0))],
        out_specs=[pl.BlockSpec((1, 1, ROW_TILE), lambda i: (i, 0, 0)),
                   pl.BlockSpec((1, LANES), lambda i: (0, 0))],
        out_shape=[jax.ShapeDtypeStruct((n_tiles, 1, ROW_TILE), I32),
                   jax.ShapeDtypeStruct((1, LANES), I32)],
        scratch_shapes=[pltpu.VMEM((1, LANES), F32), pltpu.VMEM((1, LANES), F32)],
        compiler_params=_params("arbitrary"),
        name="bucket_sort",
    )(meta, counts)


def _scatter_kernel(tinfo_ref, pos_ref, he_ref, hs_ref, zeros_ref, sem, zsem):
    rows = he_ref.shape[0] // ROW_PITCH

    @pl.when(pl.program_id(0) == 0)
    def _():
        zeros_ref[...] = jnp.zeros_like(zeros_ref)

        def tile_fill(j):
            return pltpu.make_async_copy(zeros_ref, _row_tile(hs_ref, j * MOE_TILE, MOE_TILE), zsem)

        def fill(b):
            end = tinfo_ref[0, END_LANE + b]
            begin = tinfo_ref[0, END_LANE + b - 1] if b else 0
            return end > begin, tile_fill(end - 1)

        unused = tile_fill
        tiles = hs_ref.shape[0] // (MOE_TILE * ROW_PITCH)
        used = tinfo_ref[0, LANES - 1]
        for b in range(N_BUCKETS):
            nonempty, copy = fill(b)
            pl.when(nonempty)(copy.start)
        lax.fori_loop(used, tiles, lambda j, c: (unused(j).start(), c)[1], 0)
        for b in range(N_BUCKETS):
            nonempty, copy = fill(b)
            pl.when(nonempty)(copy.wait)
        lax.fori_loop(used, tiles, lambda j, c: (unused(j).wait(), c)[1], 0)

    def start(g, c):
        for u in range(ISSUE_UNROLL):
            t = g * ISSUE_UNROLL + u
            pltpu.make_async_copy(_row_tile(he_ref, t), _row_tile(hs_ref, pos_ref[0, 0, t]),
                                  sem).start(priority=u % 2)
        return c

    lax.fori_loop(0, rows // ISSUE_UNROLL, start, 0)
    pltpu.make_async_copy(he_ref, _row_tile(hs_ref, 0, rows), sem).wait()


def _scatter_call(tinfo, pos, he, sorted_rows):
    t = he.shape[0] // ROW_PITCH
    return pl.pallas_call(
        _scatter_kernel,
        grid=(t // ROW_TILE,),
        in_specs=[pl.BlockSpec(memory_space=pltpu.SMEM),
                  pl.BlockSpec((1, 1, ROW_TILE), lambda i: (i, 0, 0), memory_space=pltpu.SMEM),
                  pl.BlockSpec((ROW_TILE * ROW_PITCH, LANES), lambda i: (i, 0))],
        out_specs=pl.BlockSpec(memory_space=pl.ANY),
        out_shape=jax.ShapeDtypeStruct((sorted_rows * ROW_PITCH, LANES), F32),
        scratch_shapes=[pltpu.VMEM((MOE_TILE * ROW_PITCH, LANES), F32), pltpu.SemaphoreType.DMA(()),
                        pltpu.SemaphoreType.DMA(())],
        compiler_params=_params("arbitrary"),
        name="row_scatter",
    )(tinfo, pos, he)


def _gather_kernel(pos_ref, x_ref, ys_ref, o_ref, buf, sem):
    rows = buf.shape[0] // ROW_PITCH
    _issue_row_gather(ys_ref, pos_ref, 0, buf, 0, sem)
    pltpu.make_async_copy(_row_tile(ys_ref, 0, rows), buf, sem).wait()
    for s in range(SUBLANES):
        lanes = slice(s * LANES, (s + 1) * LANES)
        o_ref[:, lanes] = x_ref[:, lanes] + buf[_row_part(rows, s), :]


def _gather_call(pos, x, ys):
    t = x.shape[0]
    return pl.pallas_call(
        _gather_kernel,
        grid=(t // ROW_TILE,),
        in_specs=[pl.BlockSpec((1, 1, ROW_TILE), lambda i: (i, 0, 0), memory_space=pltpu.SMEM),
                  pl.BlockSpec((ROW_TILE, D_MODEL), lambda i: (i, 0)),
                  pl.BlockSpec(memory_space=pl.ANY)],
        out_specs=pl.BlockSpec((ROW_TILE, D_MODEL), lambda i: (i, 0)),
        out_shape=jax.ShapeDtypeStruct((t, D_MODEL), F32),
        scratch_shapes=[pltpu.VMEM((ROW_TILE * ROW_PITCH, LANES), F32), pltpu.SemaphoreType.DMA(())],
        compiler_params=_params("arbitrary"),
        name="row_gather_residual",
    )(pos, x, ys)


def _tile_group(j, tinfo):
    used = tinfo[LANES - 1]
    return tinfo[jnp.minimum(j, used - 1)] // PAIRS_PER_GROUP


def _moe_kernel(tinfo, hs_ref, wg32_ref, wu32_ref, wd32_ref, ys_ref, wg_ref, wu_ref, wd_ref):
    j = pl.program_id(0)
    used = tinfo[LANES - 1]

    @pl.when((j == 0) | ((j < used) & (_tile_group(j, tinfo) != _tile_group(jnp.maximum(j, 1) - 1, tinfo))))
    def _():
        wg_ref[...] = wg32_ref[...].astype(BF16)
        wu_ref[...] = wu32_ref[...].astype(BF16)
        wd_ref[...] = wd32_ref[...].astype(BF16)

    @pl.when(j < used)
    def _():
        pair = tinfo[j] % PAIRS_PER_GROUP
        e_lo = (pair >= 3).astype(I32) + (pair >= 5).astype(I32)
        e_hi = pair - (e_lo * 3 - (e_lo * (e_lo - 1)) // 2) + e_lo + 1
        xt = jnp.concatenate([hs_ref[_row_part(MOE_TILE, s), :].astype(BF16) for s in range(SUBLANES)], axis=1)

        scalars = hs_ref[_row_part(MOE_TILE, SUBLANES), :]

        def expert(e, gate):
            hg = _dot(xt, wg_ref[e])
            hu = _dot(xt, wu_ref[e])
            act = hg * (1.0 / (1.0 + jnp.exp(-hg))) * hu * gate
            return _dot(act.astype(BF16), wd_ref[e])

        y = expert(e_lo, scalars[:, 0:1]) + expert(e_hi, scalars[:, 1:2])
        for s in range(SUBLANES):
            ys_ref[_row_part(MOE_TILE, s), :] = y[:, s * LANES:(s + 1) * LANES]
        ys_ref[_row_part(MOE_TILE, SUBLANES), :] = jnp.zeros((MOE_TILE, LANES), F32)

    @pl.when(j >= used)
    def _():
        ys_ref[...] = jnp.zeros_like(ys_ref)


def _moe_call(tinfo, hs, wg, wu, wd, layer):
    sorted_rows = hs.shape[0] // ROW_PITCH
    row = lambda j, ti: (jnp.minimum(j, ti[LANES - 1] - 1), 0)
    grp = lambda j, ti: (layer, _tile_group(j, ti), 0, 0, 0)
    return pl.pallas_call(
        _moe_kernel,
        grid_spec=pltpu.PrefetchScalarGridSpec(
            num_scalar_prefetch=1,
            grid=(sorted_rows // MOE_TILE,),
            in_specs=[pl.BlockSpec((MOE_TILE * ROW_PITCH, LANES), row),
                      pl.BlockSpec((None, None, EXPERTS_PER_GROUP, D_MODEL, D_EXPERT), grp),
                      pl.BlockSpec((None, None, EXPERTS_PER_GROUP, D_MODEL, D_EXPERT), grp),
                      pl.BlockSpec((None, None, EXPERTS_PER_GROUP, D_EXPERT, D_MODEL), grp)],
            out_specs=pl.BlockSpec((MOE_TILE * ROW_PITCH, LANES), lambda j, ti: (j, 0)),
            scratch_shapes=[pltpu.VMEM((EXPERTS_PER_GROUP, D_MODEL, D_EXPERT), BF16),
                            pltpu.VMEM((EXPERTS_PER_GROUP, D_MODEL, D_EXPERT), BF16),
                            pltpu.VMEM((EXPERTS_PER_GROUP, D_EXPERT, D_MODEL), BF16)],
        ),
        out_shape=jax.ShapeDtypeStruct((sorted_rows * ROW_PITCH, LANES), F32),
        compiler_params=_params("arbitrary"),
        name="grouped_experts",
    )(tinfo, hs, wg, wu, wd)


def _rope_tables(seq):
    half = ROT_DIM // 2
    inv_freq = ROPE_THETA ** (-jnp.arange(0, ROT_DIM, 2, dtype=F32) / ROT_DIM)
    row = jnp.arange(seq)
    pos = (RES * (row % (seq // RES)) + row // (seq // RES)).astype(F32)
    ang = pos[:, None] * inv_freq[None, :]
    cos, sin = jnp.cos(ang), jnp.sin(ang)
    d = jnp.arange(LANES) % HEAD_DIM
    cos_l = jnp.where(d[None, :] < ROT_DIM, cos[:, d % half], 1.0)
    sin_l = sin[:, d % half]
    sa = jnp.where(d[None, :] < half, -sin_l, 0.0)
    sb = jnp.where((d[None, :] >= half) & (d[None, :] < ROT_DIM), sin_l, 0.0)
    return cos_l, sa, sb


def kernel(x, norm1_gain, w_in, q_norm_gain, k_norm_gain, w_pool, pool_scale, w_out, norm2_gain, w_group, b_group, w_router, b_router, w_gate, w_up, w_down):
    b, seq, d = x.shape
    depth = w_in.shape[0]
    t = b * seq
    n_per = seq // RES
    assert d == D_MODEL and seq % (RES * ATT_BLK) == 0 and t % ROW_TILE == 0
    sorted_rows = t + N_BUCKETS * MOE_TILE

    cos, sa, sb = _rope_tables(seq)
    lane_head = jnp.arange(MXU_WIDTH) // HEAD_DIM
    block_diag = (lane_head[:, None] == lane_head[None, :]).astype(BF16)

    w_pool_b = w_pool.astype(BF16)
    grouped = lambda w: w.reshape(depth, N_EXPERT_GROUPS, EXPERTS_PER_GROUP, *w.shape[2:])
    w_gate_g, w_up_g, w_down_g = grouped(w_gate), grouped(w_up), grouped(w_down)
    n_logits = N_EXPERT_GROUPS * (1 + EXPERTS_PER_GROUP)
    w_r = jnp.pad(jnp.concatenate([w_group, w_router], axis=-1), ((0, 0), (0, 0), (0, LANES - n_logits)))
    w_r_hi = w_r.astype(BF16)
    w_r_lo = (w_r - w_r_hi.astype(F32)).astype(BF16)
    w_r_split = jnp.concatenate([w_r_hi, w_r_lo], axis=-1)
    b_r = jnp.pad(jnp.concatenate([b_group, b_router], axis=-1), ((0, 0), (0, LANES - n_logits)))
    two_heads = lambda g: jnp.tile(g, (1, LANES // HEAD_DIM))

    xr = x.reshape(b, n_per, RES, d).transpose(0, 2, 1, 3).reshape(t, d)
    moe = None
    for l in range(depth):
        q, k, v, u, *x_new = _in_call(xr, norm1_gain[l:l + 1], w_in, l, two_heads(q_norm_gain[l:l + 1]),
                                      two_heads(k_norm_gain[l:l + 1]), cos, sa, sb, block_diag, moe)
        xr = x_new[0] if x_new else xr
        att = _attn_call(q, k, v, seq)
        xr, he, meta, counts = _out_call(att, u, xr, w_pool_b, pool_scale[l:l + 1], w_out, l, norm2_gain[l:l + 1],
                                         w_r_split[l], b_r[l:l + 1], seq)
        pos, tinfo = _sort_call(meta, counts)
        hs = _scatter_call(tinfo, pos, he, sorted_rows)
        ys = _moe_call(tinfo.reshape(LANES), hs, w_gate_g, w_up_g, w_down_g, l)
        moe = (pos, ys)
    xr = _gather_call(pos, xr, ys)
    return xr.reshape(b, RES, n_per, d).transpose(0, 2, 1, 3).reshape(b, seq, d)
```

```python
import functools

import jax
import jax.numpy as jnp
from jax import lax
from jax.experimental import pallas as pl
from jax.experimental.pallas import tpu as pltpu

D_MODEL = 1024
N_HEADS = 8
HEAD_DIM = 64
ATT_WIDTH = N_HEADS * HEAD_DIM
POOL_GROUPS = 4
POOL_GROUP_DIM = 128
POOL_WIDTH = POOL_GROUPS * POOL_GROUP_DIM
POOL_WINDOWS = (2, 4, 8, 16)
IN_WIDTH = 3 * ATT_WIDTH + POOL_WIDTH
ROT_DIM = 16
ROPE_THETA = 500000.0
N_EXPERT_GROUPS = 4
EXPERTS_PER_GROUP = 4
D_EXPERT = 256
RMS_EPS = 1e-6
NEG_INF = -1e30
LOG2_E = 1.4426950408889634

LANES = 128
SUBLANES = 8
MXU_WIDTH = 256
RES = 16
ATT_BLK = 128
ATT_STEP_BLOCKS = 32
PAIRS_PER_GROUP = 6
N_BUCKETS = N_EXPERT_GROUPS * PAIRS_PER_GROUP
ROW_TILE = 512
MOE_TILE = 256
ISSUE_UNROLL = 8
ROW_PITCH = SUBLANES + 1
END_LANE = 96
VMEM_LIMIT = 56 * 1024 * 1024

F32 = jnp.float32
BF16 = jnp.bfloat16
I32 = jnp.int32


def _dot(a, b):
    return jnp.dot(a, b, preferred_element_type=F32)


def _dot_nt(a, b):
    return lax.dot_general(a, b, (((1,), (1,)), ((), ())), preferred_element_type=F32)


def _row_tile(ref, row, n=1):
    return ref.at[pl.ds(row * ROW_PITCH, n * ROW_PITCH), :]


def _row_part(n, s):
    return pl.ds(s, n, stride=ROW_PITCH)


def _params(*sem):
    return pltpu.CompilerParams(dimension_semantics=sem, vmem_limit_bytes=VMEM_LIMIT)


def _issue_row_gather(ys_ref, pos_ref, tile, buf, first, sem):
    def start(g, c):
        for u in range(ISSUE_UNROLL):
            t = g * ISSUE_UNROLL + u
            pltpu.make_async_copy(_row_tile(ys_ref, pos_ref[tile, 0, t]), _row_tile(buf, first + t),
                                  sem).start(priority=u % 2)
        return c

    lax.fori_loop(0, ROW_TILE // ISSUE_UNROLL, start, 0)


def _in_kernel(*refs, with_moe):
    if with_moe:
        pos_ref, next_pos_ref, ys_ref, refs = refs[0], refs[1], refs[2], refs[3:]
        xo_ref, buf, sem = refs[-4], refs[-2], refs[-1]
        refs = refs[:-4] + (refs[-3],)
    x_ref, g1_ref, w_ref, qg_ref, kg_ref, cos_ref, sa_ref, sb_ref, bd_ref, q_ref, k_ref, v_ref, u_ref, wb_ref = refs
    i = pl.program_id(0)

    if with_moe:
        slot_rows = ROW_TILE

        def issue(tile_pos_ref, slot):
            _issue_row_gather(ys_ref, tile_pos_ref, 0, buf, slot * slot_rows, sem.at[slot])

        pl.when(i == 0)(lambda: issue(pos_ref, 0))
        has_next = i + 1 < pl.num_programs(0)
        pl.when(has_next & (i % 2 == 1))(lambda: issue(next_pos_ref, 0))
        pl.when(has_next & (i % 2 == 0))(lambda: issue(next_pos_ref, 1))

    @pl.when(i == 0)
    def _():
        wb_ref[...] = w_ref[...].astype(BF16)

    if with_moe:
        first = (i % 2) * slot_rows
        pltpu.make_async_copy(_row_tile(ys_ref, 0, slot_rows), _row_tile(buf, first, slot_rows), sem.at[i % 2]).wait()
        x = jnp.concatenate([x_ref[:, s * LANES:(s + 1) * LANES]
                             + buf[pl.ds(first * ROW_PITCH + s, slot_rows, stride=ROW_PITCH), :]
                             for s in range(SUBLANES)], axis=1)
        xo_ref[...] = x
    else:
        x = x_ref[...]
    ms = jnp.mean(x * x, axis=-1, keepdims=True)
    h = (x * lax.rsqrt(ms + RMS_EPS) * g1_ref[...]).astype(BF16)
    cos = cos_ref[...]
    sa = sa_ref[...]
    sb = sb_ref[...]
    bd = bd_ref[...]
    wide = bd.shape[0]

    def qk(col0, gain, out_ref, scale):
        z = _dot(h, wb_ref[:, col0:col0 + ATT_WIDTH])
        for w0 in range(0, ATT_WIDTH, wide):
            zw = z[:, w0:w0 + wide]
            zz = zw * zw
            hi = zz.astype(BF16)
            lo = (zz - hi.astype(F32)).astype(BF16)
            ssq = _dot(hi, bd) + _dot(lo, bd)
            yw = zw * lax.rsqrt(ssq * (1.0 / HEAD_DIM) + RMS_EPS)
            for c in range(wide // LANES):
                y = yw[:, c * LANES:(c + 1) * LANES] * gain
                rot = y * cos + pltpu.roll(y, LANES - ROT_DIM // 2, 1) * sa + pltpu.roll(y, ROT_DIM // 2, 1) * sb
                out_ref[:, w0 + c * LANES:w0 + (c + 1) * LANES] = (rot * scale).astype(BF16)

    qk(0, qg_ref[...], q_ref, HEAD_DIM ** -0.5 * LOG2_E)
    qk(ATT_WIDTH, kg_ref[...], k_ref, 1.0)
    v_ref[...] = _dot(h, wb_ref[:, 2 * ATT_WIDTH:3 * ATT_WIDTH]).astype(BF16)
    u_ref[...] = _dot(h, wb_ref[:, 3 * ATT_WIDTH:]).astype(BF16)


def _in_call(x, g1, w, layer, qg, kg, cos, sa, sb, bd, moe=None):
    t = x.shape[0]
    seq_tiles = cos.shape[0] // ROW_TILE
    row = lambda i: (i, 0)
    fix = lambda i: (0, 0)
    tab = lambda i: (i % seq_tiles, 0)
    out = jax.ShapeDtypeStruct((t, ATT_WIDTH), BF16)
    with_moe = moe is not None
    tiles = t // ROW_TILE
    pos_block = lambda index: pl.BlockSpec((1, 1, ROW_TILE), index, memory_space=pltpu.SMEM)
    extra_in = [pos_block(lambda i: (i, 0, 0)), pos_block(lambda i: (jnp.minimum(i + 1, tiles - 1), 0, 0)),
                pl.BlockSpec(memory_space=pl.ANY)] if with_moe else []
    extra_out = [pl.BlockSpec((ROW_TILE, D_MODEL), row)] if with_moe else []
    extra_shape = [jax.ShapeDtypeStruct((t, D_MODEL), F32)] if with_moe else []
    extra_scratch = [pltpu.VMEM((2 * ROW_TILE * ROW_PITCH, LANES), F32), pltpu.SemaphoreType.DMA((2,))] if with_moe else []
    return pl.pallas_call(
        functools.partial(_in_kernel, with_moe=with_moe),
        grid=(t // ROW_TILE,),
        in_specs=extra_in + [
            pl.BlockSpec((ROW_TILE, D_MODEL), row),
            pl.BlockSpec((1, D_MODEL), fix),
            pl.BlockSpec((None, D_MODEL, IN_WIDTH), lambda i: (layer, 0, 0)),
            pl.BlockSpec((1, LANES), fix),
            pl.BlockSpec((1, LANES), fix),
            pl.BlockSpec((ROW_TILE, LANES), tab),
            pl.BlockSpec((ROW_TILE, LANES), tab),
            pl.BlockSpec((ROW_TILE, LANES), tab),
            pl.BlockSpec(bd.shape, fix),
        ],
        out_specs=[pl.BlockSpec((ROW_TILE, ATT_WIDTH), row)] * 4 + extra_out,
        out_shape=[out] * 4 + extra_shape,
        scratch_shapes=[pltpu.VMEM((D_MODEL, IN_WIDTH), BF16)] + extra_scratch,
        compiler_params=_params("arbitrary"),
        name="in_proj",
    )(*((moe[0], moe[0], moe[1]) if with_moe else ()), x, g1, w, qg, kg, cos, sa, sb, bd)


def _attn_bias(q_off, k_idx, with_prev):
    ok = (k_idx >= q_off) & (k_idx <= q_off + ATT_BLK)
    if not with_prev:
        ok = ok & (k_idx >= ATT_BLK)
    return jnp.where(ok, 0.0, NEG_INF).astype(F32)


def _attn_kernel(q_ref, k_ref, v_ref, o_ref, q32, k32, v32, m_s, l_s, acc_s, bias_s):
    n_per = q32.shape[1]
    pad = ATT_BLK
    zeros = jnp.zeros((pad, LANES), F32)
    for r in range(RES):
        rows = pl.ds(r * n_per, n_per)
        q32[r] = q_ref[rows, :].astype(F32)
        k32[r, pl.ds(0, pad), :] = zeros
        v32[r, pl.ds(0, pad), :] = zeros
        k32[r, pl.ds(pad, n_per), :] = k_ref[rows, :].astype(F32)
        v32[r, pl.ds(pad, n_per), :] = v_ref[rows, :].astype(F32)

    qi = lax.broadcasted_iota(I32, (2 * ATT_BLK, 2 * ATT_BLK), 0) & (ATT_BLK - 1)
    kc = lax.broadcasted_iota(I32, (2 * ATT_BLK, 2 * ATT_BLK), 1)
    offs = (
        (16 * (qi & 7) + (qi >> 3), 16 * (kc & 15) + (kc >> 4)),
        (4 * (qi & 31) + (qi >> 5), 4 * (kc & 63) + (kc >> 6)),
        (qi, kc),
    )
    for br, (qo, ko) in enumerate(offs):
        bias_s[br, 0] = _attn_bias(qo, ko, False)
        bias_s[br, 1] = _attn_bias(qo, ko, True)

    head_a = lax.broadcasted_iota(I32, (ATT_BLK, LANES), 1) < HEAD_DIM

    def block(qb, ks, vs, bias):
        qa = jnp.where(head_a, qb, 0.0)
        qq = jnp.concatenate([qa, qb - qa], axis=0).astype(BF16)
        s = _dot_nt(qq, ks.astype(BF16)) + bias
        m = jnp.max(s, axis=1, keepdims=True)
        p = jnp.exp2(s - m)
        l = jnp.sum(p, axis=1, keepdims=True)
        pv = _dot(p.astype(BF16), vs.astype(BF16))
        m2 = jnp.where(head_a, m[:ATT_BLK], m[ATT_BLK:])
        l2 = jnp.where(head_a, l[:ATT_BLK], l[ATT_BLK:])
        pv2 = jnp.where(head_a, pv[:ATT_BLK], pv[ATT_BLK:])
        return m2, l2, pv2

    def store(br, slab, rows, triple, shape=None):
        for ref, val in zip((m_s, l_s, acc_s), triple):
            ref[br, slab, rows, :] = val if shape is None else val.reshape(shape)

    slabs16 = ATT_STEP_BLOCKS // (n_per // ATT_BLK)

    def body16(rr, carry):
        for i in range(slabs16):
            r = slabs16 * rr + i
            for c in range(n_per // ATT_BLK):
                rows = pl.ds(c * ATT_BLK, ATT_BLK)
                keys = pl.ds(c * ATT_BLK, 2 * ATT_BLK)
                store(2, r, rows, block(q32[r, rows, :], k32[r, keys, :], v32[r, keys, :], bias_s[2, min(c, 1)]))
        return carry

    lax.fori_loop(0, RES // slabs16, body16, 0)

    sub = ATT_BLK // 4
    per4 = ATT_STEP_BLOCKS // 4

    def body4(cc, carry):
        for r4 in range(4):
            slabs = [r4 + 4 * m for m in range(4)]
            for c in range(per4):
                first = pl.multiple_of((cc * per4 + c) * sub, sub)
                rows = pl.ds(first, sub)
                keys = pl.ds(first + pad - sub, 2 * sub)
                cat = lambda ref, idx: jnp.concatenate([ref[s, idx, :] for s in slabs], axis=0)
                bias = bias_s[1, jnp.minimum(cc, 1)] if c == 0 else bias_s[1, 1]
                triple = block(cat(q32, rows), cat(k32, keys), cat(v32, keys), bias)
                for j, s in enumerate(slabs):
                    store(1, s, rows, [x[j * sub:(j + 1) * sub] for x in triple])
        return carry

    lax.fori_loop(0, n_per // (per4 * sub), body4, 0)

    sub1 = ATT_BLK // RES

    def body1(jj, carry):
        for g in range(ATT_STEP_BLOCKS):
            first = pl.multiple_of((jj * ATT_STEP_BLOCKS + g) * sub1, sub1)
            rows = pl.ds(first, sub1)
            keys = pl.ds(first + pad - sub1, 2 * sub1)
            bias = bias_s[0, jnp.minimum(jj, 1)] if g == 0 else bias_s[0, 1]
            triple = block(q32[:, rows, :].reshape(ATT_BLK, LANES), k32[:, keys, :].reshape(2 * ATT_BLK, LANES),
                           v32[:, keys, :].reshape(2 * ATT_BLK, LANES), bias)
            store(0, slice(None), rows, triple, (RES, sub1, LANES))
        return carry

    lax.fori_loop(0, n_per // (ATT_STEP_BLOCKS * sub1), body1, 0)

    for r in range(RES):
        ms = [m_s[br, r] for br in range(3)]
        top = jnp.maximum(jnp.maximum(ms[0], ms[1]), ms[2])
        ws = [jnp.exp2(m - top) for m in ms]
        num = sum(w * acc_s[br, r] for br, w in enumerate(ws))
        den = sum(w * l_s[br, r] for br, w in enumerate(ws))
        o_ref[pl.ds(r * n_per, n_per), :] = (num / den).astype(BF16)


def _attn_call(q, k, v, seq):
    t = q.shape[0]
    n_per = seq // RES
    spec = pl.BlockSpec((seq, LANES), lambda b, h: (b, h))
    staged = pltpu.VMEM((RES, n_per, LANES), F32)
    state = pltpu.VMEM((3, RES, n_per, LANES), F32)
    padded = pltpu.VMEM((RES, n_per + ATT_BLK, LANES), F32)
    return pl.pallas_call(
        _attn_kernel,
        grid=(t // seq, ATT_WIDTH // LANES),
        in_specs=[spec, spec, spec],
        out_specs=spec,
        out_shape=jax.ShapeDtypeStruct((t, ATT_WIDTH), BF16),
        scratch_shapes=[staged, padded, padded, state, state, state,
                        pltpu.VMEM((3, 2, 2 * ATT_BLK, 2 * ATT_BLK), F32)],
        compiler_params=_params("parallel", "parallel"),
        name="dilated_attn",
    )(q, k, v)


def _row_min_index(cond, lane_f):
    return jnp.min(jnp.where(cond, lane_f, float(LANES)), axis=1, keepdims=True)


def _out_kernel(att_ref, u_ref, uh_ref, x_ref, wp_ref, ps_ref, wo_ref, g2_ref, rw_ref, br_ref,
                xo_ref, he_ref, bk_ref, cnt_ref, wob_ref):
    i = pl.program_id(1)

    @pl.when((pl.program_id(0) == 0) & (i == 0))
    def _():
        wob_ref[...] = wo_ref[...].astype(BF16)

    nb = u_ref.shape[2]
    rows = RES * nb
    u = u_ref[0].astype(F32)
    halo = jnp.where(i > 0, uh_ref[0][:, -1:, :].astype(F32), 0.0)
    u_prev = jnp.concatenate([halo, u[:, :nb - 1, :]], axis=1)

    n_idx = lax.broadcasted_iota(I32, (RES, nb, POOL_GROUP_DIM), 1) + i * nb
    r_idx = lax.broadcasted_iota(I32, (RES, nb, POOL_GROUP_DIM), 0)
    p1 = (RES * n_idx + r_idx + 1).astype(F32)

    pools = []
    for g, w in enumerate(POOL_WINDOWS):
        lanes = slice(g * POOL_GROUP_DIM, (g + 1) * POOL_GROUP_DIM)
        ug = u[:, :, lanes]
        upg = u_prev[:, :, lanes]
        tot = ug
        for j in range(1, w):
            tot = tot + jnp.concatenate([upg[RES - j:], ug[:RES - j]], axis=0)
        rg = tot / jnp.minimum(p1, float(w)) - ug
        y = _dot(rg.reshape(rows, POOL_GROUP_DIM).astype(BF16), wp_ref[g])
        pools.append((y * ps_ref[:, lanes]).astype(BF16))
    mix = jnp.concatenate([att_ref[0].reshape(rows, ATT_WIDTH)] + pools, axis=1)
    x = x_ref[0].reshape(rows, D_MODEL) + _dot(mix, wob_ref[...])
    xo_ref[0] = x.reshape(RES, nb, D_MODEL)

    ms = jnp.mean(x * x, axis=-1, keepdims=True)
    h = x * lax.rsqrt(ms + RMS_EPS) * g2_ref[...]

    hh = h.astype(BF16)
    hl = (h - hh.astype(F32)).astype(BF16)
    by_hh = _dot(hh, rw_ref[...])
    logits = by_hh[:, :LANES] + (_dot(hl, rw_ref[:, :LANES]) + by_hh[:, LANES:]) + br_ref[...]
    lane = lax.broadcasted_iota(I32, (rows, LANES), 1)
    lane_f = lane.astype(F32)
    is_g = lane < N_EXPERT_GROUPS
    gl = jnp.where(is_g, logits, -jnp.inf)
    gm = jnp.max(gl, axis=1, keepdims=True)
    g_idx = _row_min_index(is_g & (gl == gm), lane_f)
    p_top = 1.0 / jnp.sum(jnp.where(is_g, jnp.exp(logits - gm), 0.0), axis=1, keepdims=True)
    e_lane = lane - N_EXPERT_GROUPS
    in_grp = (e_lane >= 0) & (e_lane < N_EXPERT_GROUPS * EXPERTS_PER_GROUP) & \
             ((e_lane >> 2).astype(F32) == g_idx)
    el = jnp.where(in_grp, logits, -jnp.inf)
    v1 = jnp.max(el, axis=1, keepdims=True)
    i1 = _row_min_index(in_grp & (el == v1), lane_f)
    rest = in_grp & (lane_f != i1)
    el2 = jnp.where(rest, logits, -jnp.inf)
    v2 = jnp.max(el2, axis=1, keepdims=True)
    i2 = _row_min_index(rest & (el2 == v2), lane_f)
    e21 = jnp.exp(v2 - v1)
    w1 = p_top / (1.0 + e21)
    w2 = p_top * e21 / (1.0 + e21)
    a1 = i1 - N_EXPERT_GROUPS - EXPERTS_PER_GROUP * g_idx
    a2 = i2 - N_EXPERT_GROUPS - EXPERTS_PER_GROUP * g_idx
    first_low = a1 < a2
    lo = jnp.where(first_low, a1, a2)
    hi = jnp.where(first_low, a2, a1)
    w_lo = jnp.where(first_low, w1, w2)
    w_hi = jnp.where(first_low, w2, w1)
    bucket = g_idx * PAIRS_PER_GROUP + lo * 3.0 - lo * (lo - 1.0) * 0.5 + hi - lo - 1.0
    meta = jnp.where(lane == 0, w_lo, jnp.where(lane == 1, w_hi, bucket))
    bk_ref[0] = meta.reshape(RES, nb, LANES)
    cnt_ref[0] = jnp.sum((lane_f == bucket).astype(F32), axis=0, keepdims=True)

    for s in range(ROW_PITCH):
        part = h[:, s * LANES:(s + 1) * LANES] if s < SUBLANES else meta
        for r in range(RES):
            he_ref[0, r, _row_part(nb, s), :] = part[r * nb:(r + 1) * nb]


def _out_call(att, u, x, wp, ps, wo, layer, g2, rw, br, seq):
    t = x.shape[0]
    b = t // seq
    n_per = seq // RES
    nb = ROW_TILE // RES
    halo_rows = 16
    v4 = lambda a: a.reshape(b, RES, n_per, a.shape[-1])
    tile = lambda bi, i: (bi, 0, i, 0)
    halo = lambda bi, i: (bi, 0, jnp.maximum(i * (nb // halo_rows) - 1, 0), 0)
    fix2 = lambda bi, i: (0, 0)
    steps = n_per // nb
    xo, he, bk, cnt = pl.pallas_call(
        _out_kernel,
        grid=(b, steps),
        in_specs=[
            pl.BlockSpec((1, RES, nb, ATT_WIDTH), tile),
            pl.BlockSpec((1, RES, nb, POOL_WIDTH), tile),
            pl.BlockSpec((1, RES, halo_rows, POOL_WIDTH), halo),
            pl.BlockSpec((1, RES, nb, D_MODEL), tile),
            pl.BlockSpec((None, POOL_GROUPS, POOL_GROUP_DIM, POOL_GROUP_DIM), lambda bi, i: (layer, 0, 0, 0)),
            pl.BlockSpec((1, POOL_WIDTH), fix2),
            pl.BlockSpec((None, D_MODEL, D_MODEL), lambda bi, i: (layer, 0, 0)),
            pl.BlockSpec((1, D_MODEL), fix2),
            pl.BlockSpec((D_MODEL, 2 * LANES), fix2),
            pl.BlockSpec((1, LANES), fix2),
        ],
        out_specs=[pl.BlockSpec((1, RES, nb, D_MODEL), tile),
                   pl.BlockSpec((1, RES, nb * ROW_PITCH, LANES), tile),
                   pl.BlockSpec((1, RES, nb, LANES), tile),
                   pl.BlockSpec((1, 1, LANES), lambda bi, i: (bi * steps + i, 0, 0))],
        out_shape=[jax.ShapeDtypeStruct((b, RES, n_per, D_MODEL), F32),
                   jax.ShapeDtypeStruct((b, RES, n_per * ROW_PITCH, LANES), F32),
                   jax.ShapeDtypeStruct((b, RES, n_per, LANES), F32),
                   jax.ShapeDtypeStruct((b * steps, 1, LANES), F32)],
        scratch_shapes=[pltpu.VMEM((D_MODEL, D_MODEL), BF16)],
        compiler_params=_params("arbitrary", "arbitrary"),
        name="out_proj_router",
    )(v4(att), v4(u), v4(u), v4(x), wp, ps, wo, g2, rw, br)
    return xo.reshape(t, D_MODEL), he.reshape(t * ROW_PITCH, LANES), bk.reshape(t, LANES), cnt


def _sort_kernel(meta_ref, counts_ref, pos_ref, tinfo_ref, cnt_s, off_s):
    i = pl.program_id(0)
    rows = meta_ref.shape[0]
    lane = lax.broadcasted_iota(I32, (rows, LANES), 1)
    onehot = lane.astype(F32) == meta_ref[:, 2:3]
    oh = onehot.astype(F32)
    tile_count = jnp.sum(oh, axis=0, keepdims=True)

    @pl.when(i == 0)
    def _():
        totals = jnp.sum(counts_ref[...], axis=0)
        tiles = jnp.floor((totals + (MOE_TILE - 1.0)) * (1.0 / MOE_TILE))
        tiles8 = jnp.broadcast_to(tiles, (SUBLANES, LANES)).astype(BF16)
        sq = (LANES, LANES)
        before = lax.broadcasted_iota(I32, sq, 0) < lax.broadcasted_iota(I32, sq, 1)
        start = _dot(tiles8, before.astype(BF16))
        off_s[...] = start[0:1] * float(MOE_TILE)
        cnt_s[...] = jnp.zeros_like(cnt_s)
        end = (start + tiles8.astype(F32)).astype(BF16)
        eye = (lax.broadcasted_iota(I32, sq, 0) == lax.broadcasted_iota(I32, sq, 1)).astype(BF16)
        end_col = _dot_nt(eye, end)[:, 0:1]
        tile_id = lax.broadcasted_iota(I32, sq, 1).astype(F32)
        tile_bucket = jnp.sum((end_col <= tile_id).astype(F32), axis=0, keepdims=True)
        total = jnp.max(end.astype(F32)[0:1], axis=1, keepdims=True)
        row_lane = lax.broadcasted_iota(I32, (1, LANES), 1)
        ends = pltpu.roll(end.astype(F32)[0:1], END_LANE, 1)
        is_end = (row_lane >= END_LANE) & (row_lane < END_LANE + N_BUCKETS)
        tinfo_ref[...] = jnp.where(row_lane == LANES - 1, total, jnp.where(is_end, ends, tile_bucket)).astype(I32)

    sq = (rows, rows)
    upto = lax.broadcasted_iota(I32, sq, 1) <= lax.broadcasted_iota(I32, sq, 0)
    prefix = _dot(upto.astype(BF16), oh.astype(BF16))
    posv = jnp.where(onehot, prefix - 1.0 + cnt_s[...] + off_s[...], 0.0)
    hi = jnp.floor(posv * (1.0 / 256.0))
    lo = posv - hi * 256.0
    ones = jnp.ones((SUBLANES, LANES), BF16)
    pos = _dot_nt(ones, hi.astype(BF16)) * 256.0 + _dot_nt(ones, lo.astype(BF16))
    pos_ref[0] = pos[0:1].astype(I32)
    cnt_s[...] += tile_count


def _sort_call(meta, counts):
    t = meta.shape[0]
    n_tiles = t // ROW_TILE
    return pl.pallas_call(
        _sort_kernel,
        grid=(n_tiles,),
        in_specs=[pl.BlockSpec((ROW_TILE, LANES), lambda i: (i, 0)),
                  pl.BlockSpec(counts.shape, lambda i: (0, 0, 0))],
        out_specs=[pl.BlockSpec((1, 1, ROW_TILE), lambda i: (i, 0, 0)),
                   pl.BlockSpec((1, LANES), lambda i: (0, 0))],
        out_shape=[jax.ShapeDtypeStruct((n_tiles, 1, ROW_TILE), I32),
                   jax.ShapeDtypeStruct((1, LANES), I32)],
        scratch_shapes=[pltpu.VMEM((1, LANES), F32), pltpu.VMEM((1, LANES), F32)],
        compiler_params=_params("arbitrary"),
        name="bucket_sort",
    )(meta, counts)


def _scatter_kernel(tinfo_ref, pos_ref, he_ref, hs_ref, zeros_ref, sem, zsem):
    rows = he_ref.shape[0] // ROW_PITCH

    @pl.when(pl.program_id(0) == 0)
    def _():
        zeros_ref[...] = jnp.zeros_like(zeros_ref)

        def tile_fill(j):
            return pltpu.make_async_copy(zeros_ref, _row_tile(hs_ref, j * MOE_TILE, MOE_TILE), zsem)

        def fill(b):
            end = tinfo_ref[0, END_LANE + b]
            begin = tinfo_ref[0, END_LANE + b - 1] if b else 0
            return end > begin, tile_fill(end - 1)

        unused = tile_fill
        tiles = hs_ref.shape[0] // (MOE_TILE * ROW_PITCH)
        used = tinfo_ref[0, LANES - 1]
        for b in range(N_BUCKETS):
            nonempty, copy = fill(b)
            pl.when(nonempty)(copy.start)
        lax.fori_loop(used, tiles, lambda j, c: (unused(j).start(), c)[1], 0)
        for b in range(N_BUCKETS):
            nonempty, copy = fill(b)
            pl.when(nonempty)(copy.wait)
        lax.fori_loop(used, tiles, lambda j, c: (unused(j).wait(), c)[1], 0)

    def start(g, c):
        for u in range(ISSUE_UNROLL):
            t = g * ISSUE_UNROLL + u
            pltpu.make_async_copy(_row_tile(he_ref, t), _row_tile(hs_ref, pos_ref[0, 0, t]),
                                  sem).start(priority=u % 2)
        return c

    lax.fori_loop(0, rows // ISSUE_UNROLL, start, 0)
    pltpu.make_async_copy(he_ref, _row_tile(hs_ref, 0, rows), sem).wait()


def _scatter_call(tinfo, pos, he, sorted_rows):
    t = he.shape[0] // ROW_PITCH
    return pl.pallas_call(
        _scatter_kernel,
        grid=(t // ROW_TILE,),
        in_specs=[pl.BlockSpec(memory_space=pltpu.SMEM),
                  pl.BlockSpec((1, 1, ROW_TILE), lambda i: (i, 0, 0), memory_space=pltpu.SMEM),
                  pl.BlockSpec((ROW_TILE * ROW_PITCH, LANES), lambda i: (i, 0))],
        out_specs=pl.BlockSpec(memory_space=pl.ANY),
        out_shape=jax.ShapeDtypeStruct((sorted_rows * ROW_PITCH, LANES), F32),
        scratch_shapes=[pltpu.VMEM((MOE_TILE * ROW_PITCH, LANES), F32), pltpu.SemaphoreType.DMA(()),
                        pltpu.SemaphoreType.DMA(())],
        compiler_params=_params("arbitrary"),
        name="row_scatter",
    )(tinfo, pos, he)


def _gather_kernel(pos_ref, x_ref, ys_ref, o_ref, buf, sem):
    rows = buf.shape[0] // ROW_PITCH
    _issue_row_gather(ys_ref, pos_ref, 0, buf, 0, sem)
    pltpu.make_async_copy(_row_tile(ys_ref, 0, rows), buf, sem).wait()
    for s in range(SUBLANES):
        lanes = slice(s * LANES, (s + 1) * LANES)
        o_ref[:, lanes] = x_ref[:, lanes] + buf[_row_part(rows, s), :]


def _gather_call(pos, x, ys):
    t = x.shape[0]
    return pl.pallas_call(
        _gather_kernel,
        grid=(t // ROW_TILE,),
        in_specs=[pl.BlockSpec((1, 1, ROW_TILE), lambda i: (i, 0, 0), memory_space=pltpu.SMEM),
                  pl.BlockSpec((ROW_TILE, D_MODEL), lambda i: (i, 0)),
                  pl.BlockSpec(memory_space=pl.ANY)],
        out_specs=pl.BlockSpec((ROW_TILE, D_MODEL), lambda i: (i, 0)),
        out_shape=jax.ShapeDtypeStruct((t, D_MODEL), F32),
        scratch_shapes=[pltpu.VMEM((ROW_TILE * ROW_PITCH, LANES), F32), pltpu.SemaphoreType.DMA(())],
        compiler_params=_params("arbitrary"),
        name="row_gather_residual",
    )(pos, x, ys)


def _tile_group(j, tinfo):
    used = tinfo[LANES - 1]
    return tinfo[jnp.minimum(j, used - 1)] // PAIRS_PER_GROUP


def _moe_kernel(tinfo, hs_ref, wg32_ref, wu32_ref, wd32_ref, ys_ref, wg_ref, wu_ref, wd_ref):
    j = pl.program_id(0)
    used = tinfo[LANES - 1]

    @pl.when((j == 0) | ((j < used) & (_tile_group(j, tinfo) != _tile_group(jnp.maximum(j, 1) - 1, tinfo))))
    def _():
        wg_ref[...] = wg32_ref[...].astype(BF16)
        wu_ref[...] = wu32_ref[...].astype(BF16)
        wd_ref[...] = wd32_ref[...].astype(BF16)

    @pl.when(j < used)
    def _():
        pair = tinfo[j] % PAIRS_PER_GROUP
        e_lo = (pair >= 3).astype(I32) + (pair >= 5).astype(I32)
        e_hi = pair - (e_lo * 3 - (e_lo * (e_lo - 1)) // 2) + e_lo + 1
        xt = jnp.concatenate([hs_ref[_row_part(MOE_TILE, s), :].astype(BF16) for s in range(SUBLANES)], axis=1)

        scalars = hs_ref[_row_part(MOE_TILE, SUBLANES), :]

        def expert(e, gate):
            hg = _dot(xt, wg_ref[e])
            hu = _dot(xt, wu_ref[e])
            act = hg * (1.0 / (1.0 + jnp.exp(-hg))) * hu * gate
            return _dot(act.astype(BF16), wd_ref[e])

        y = expert(e_lo, scalars[:, 0:1]) + expert(e_hi, scalars[:, 1:2])
        for s in range(SUBLANES):
            ys_ref[_row_part(MOE_TILE, s), :] = y[:, s * LANES:(s + 1) * LANES]
        ys_ref[_row_part(MOE_TILE, SUBLANES), :] = jnp.zeros((MOE_TILE, LANES), F32)

    @pl.when(j >= used)
    def _():
        ys_ref[...] = jnp.zeros_like(ys_ref)


def _moe_call(tinfo, hs, wg, wu, wd, layer):
    sorted_rows = hs.shape[0] // ROW_PITCH
    row = lambda j, ti: (jnp.minimum(j, ti[LANES - 1] - 1), 0)
    grp = lambda j, ti: (layer, _tile_group(j, ti), 0, 0, 0)
    return pl.pallas_call(
        _moe_kernel,
        grid_spec=pltpu.PrefetchScalarGridSpec(
            num_scalar_prefetch=1,
            grid=(sorted_rows // MOE_TILE,),
            in_specs=[pl.BlockSpec((MOE_TILE * ROW_PITCH, LANES), row),
                      pl.BlockSpec((None, None, EXPERTS_PER_GROUP, D_MODEL, D_EXPERT), grp),
                      pl.BlockSpec((None, None, EXPERTS_PER_GROUP, D_MODEL, D_EXPERT), grp),
                      pl.BlockSpec((None, None, EXPERTS_PER_GROUP, D_EXPERT, D_MODEL), grp)],
            out_specs=pl.BlockSpec((MOE_TILE * ROW_PITCH, LANES), lambda j, ti: (j, 0)),
            scratch_shapes=[pltpu.VMEM((EXPERTS_PER_GROUP, D_MODEL, D_EXPERT), BF16),
                            pltpu.VMEM((EXPERTS_PER_GROUP, D_MODEL, D_EXPERT), BF16),
                            pltpu.VMEM((EXPERTS_PER_GROUP, D_EXPERT, D_MODEL), BF16)],
        ),
        out_shape=jax.ShapeDtypeStruct((sorted_rows * ROW_PITCH, LANES), F32),
        compiler_params=_params("arbitrary"),
        name="grouped_experts",
    )(tinfo, hs, wg, wu, wd)


def _rope_tables(seq):
    half = ROT_DIM // 2
    inv_freq = ROPE_THETA ** (-jnp.arange(0, ROT_DIM, 2, dtype=F32) / ROT_DIM)
    row = jnp.arange(seq)
    pos = (RES * (row % (seq // RES)) + row // (seq // RES)).astype(F32)
    ang = pos[:, None] * inv_freq[None, :]
    cos, sin = jnp.cos(ang), jnp.sin(ang)
    d = jnp.arange(LANES) % HEAD_DIM
    cos_l = jnp.where(d[None, :] < ROT_DIM, cos[:, d % half], 1.0)
    sin_l = sin[:, d % half]
    sa = jnp.where(d[None, :] < half, -sin_l, 0.0)
    sb = jnp.where((d[None, :] >= half) & (d[None, :] < ROT_DIM), sin_l, 0.0)
    return cos_l, sa, sb


def kernel(x, norm1_gain, w_in, q_norm_gain, k_norm_gain, w_pool, pool_scale, w_out, norm2_gain, w_group, b_group, w_router, b_router, w_gate, w_up, w_down):
    b, seq, d = x.shape
    depth = w_in.shape[0]
    t = b * seq
    n_per = seq // RES
    assert d == D_MODEL and seq % (RES * ATT_BLK) == 0 and t % ROW_TILE == 0
    sorted_rows = t + N_BUCKETS * MOE_TILE

    cos, sa, sb = _rope_tables(seq)
    lane_head = jnp.arange(MXU_WIDTH) // HEAD_DIM
    block_diag = (lane_head[:, None] == lane_head[None, :]).astype(BF16)

    w_pool_b = w_pool.astype(BF16)
    grouped = lambda w: w.reshape(depth, N_EXPERT_GROUPS, EXPERTS_PER_GROUP, *w.shape[2:])
    w_gate_g, w_up_g, w_down_g = grouped(w_gate), grouped(w_up), grouped(w_down)
    n_logits = N_EXPERT_GROUPS * (1 + EXPERTS_PER_GROUP)
    w_r = jnp.pad(jnp.concatenate([w_group, w_router], axis=-1), ((0, 0), (0, 0), (0, LANES - n_logits)))
    w_r_hi = w_r.astype(BF16)
    w_r_lo = (w_r - w_r_hi.astype(F32)).astype(BF16)
    w_r_split = jnp.concatenate([w_r_hi, w_r_lo], axis=-1)
    b_r = jnp.pad(jnp.concatenate([b_group, b_router], axis=-1), ((0, 0), (0, LANES - n_logits)))
    two_heads = lambda g: jnp.tile(g, (1, LANES // HEAD_DIM))

    xr = x.reshape(b, n_per, RES, d).transpose(0, 2, 1, 3).reshape(t, d)
    moe = None
    for l in range(depth):
        q, k, v, u, *x_new = _in_call(xr, norm1_gain[l:l + 1], w_in, l, two_heads(q_norm_gain[l:l + 1]),
                                      two_heads(k_norm_gain[l:l + 1]), cos, sa, sb, block_diag, moe)
        xr = x_new[0] if x_new else xr
        att = _attn_call(q, k, v, seq)
        xr, he, meta, counts = _out_call(att, u, xr, w_pool_b, pool_scale[l:l + 1], w_out, l, norm2_gain[l:l + 1],
                                         w_r_split[l], b_r[l:l + 1], seq)
        pos, tinfo = _sort_call(meta, counts)
        hs = _scatter_call(tinfo, pos, he, sorted_rows)
        ys = _moe_call(tinfo.reshape(LANES), hs, w_gate_g, w_up_g, w_down_g, l)
        moe = (pos, ys)
    xr = _gather_call(pos, xr, ys)
    return xr.reshape(b, RES, n_per, d).transpose(0, 2, 1, 3).reshape(b, seq, d)
```

```python
import functools

import jax
import jax.numpy as jnp
from jax import lax
from jax.experimental import pallas as pl
from jax.experimental.pallas import tpu as pltpu

D_MODEL = 1024
N_HEADS = 8
HEAD_DIM = 64
ATT_WIDTH = N_HEADS * HEAD_DIM
POOL_GROUPS = 4
POOL_GROUP_DIM = 128
POOL_WIDTH = POOL_GROUPS * POOL_GROUP_DIM
POOL_WINDOWS = (2, 4, 8, 16)
IN_WIDTH = 3 * ATT_WIDTH + POOL_WIDTH
ROT_DIM = 16
ROPE_THETA = 500000.0
N_EXPERT_GROUPS = 4
EXPERTS_PER_GROUP = 4
D_EXPERT = 256
RMS_EPS = 1e-6
NEG_INF = -1e30
LOG2_E = 1.4426950408889634

LANES = 128
SUBLANES = 8
MXU_WIDTH = 256
RES = 16
ATT_BLK = 128
ATT_STEP_BLOCKS = 32
PAIRS_PER_GROUP = 6
N_BUCKETS = N_EXPERT_GROUPS * PAIRS_PER_GROUP
ROW_TILE = 512
MOE_TILE = 256
ISSUE_UNROLL = 8
ROW_PITCH = SUBLANES + 1
END_LANE = 96
VMEM_LIMIT = 56 * 1024 * 1024

F32 = jnp.float32
BF16 = jnp.bfloat16
I32 = jnp.int32


def _dot(a, b):
    return jnp.dot(a, b, preferred_element_type=F32)


def _dot_nt(a, b):
    return lax.dot_general(a, b, (((1,), (1,)), ((), ())), preferred_element_type=F32)


def _row_tile(ref, row, n=1):
    return ref.at[pl.ds(row * ROW_PITCH, n * ROW_PITCH), :]


def _row_part(n, s):
    return pl.ds(s, n, stride=ROW_PITCH)


def _params(*sem):
    return pltpu.CompilerParams(dimension_semantics=sem, vmem_limit_bytes=VMEM_LIMIT)


def _issue_row_gather(ys_ref, pos_ref, tile, buf, first, sem):
    def start(g, c):
        for u in range(ISSUE_UNROLL):
            t = g * ISSUE_UNROLL + u
            pltpu.make_async_copy(_row_tile(ys_ref, pos_ref[tile, 0, t]), _row_tile(buf, first + t),
                                  sem).start(priority=u % 2)
        return c

    lax.fori_loop(0, ROW_TILE // ISSUE_UNROLL, start, 0)


def _in_kernel(*refs, with_moe):
    if with_moe:
        pos_ref, next_pos_ref, ys_ref, refs = refs[0], refs[1], refs[2], refs[3:]
        xo_ref, buf, sem = refs[-4], refs[-2], refs[-1]
        refs = refs[:-4] + (refs[-3],)
    x_ref, g1_ref, w_ref, qg_ref, kg_ref, cos_ref, sa_ref, sb_ref, bd_ref, q_ref, k_ref, v_ref, u_ref, wb_ref = refs
    i = pl.program_id(0)

    if with_moe:
        slot_rows = ROW_TILE

        def issue(tile_pos_ref, slot):
            _issue_row_gather(ys_ref, tile_pos_ref, 0, buf, slot * slot_rows, sem.at[slot])

        pl.when(i == 0)(lambda: issue(pos_ref, 0))
        has_next = i + 1 < pl.num_programs(0)
        pl.when(has_next & (i % 2 == 1))(lambda: issue(next_pos_ref, 0))
        pl.when(has_next & (i % 2 == 0))(lambda: issue(next_pos_ref, 1))

    @pl.when(i == 0)
    def _():
        wb_ref[...] = w_ref[...].astype(BF16)

    if with_moe:
        first = (i % 2) * slot_rows
        pltpu.make_async_copy(_row_tile(ys_ref, 0, slot_rows), _row_tile(buf, first, slot_rows), sem.at[i % 2]).wait()
        x = jnp.concatenate([x_ref[:, s * LANES:(s + 1) * LANES]
                             + buf[pl.ds(first * ROW_PITCH + s, slot_rows, stride=ROW_PITCH), :]
                             for s in range(SUBLANES)], axis=1)
        xo_ref[...] = x
    else:
        x = x_ref[...]
    ms = jnp.mean(x * x, axis=-1, keepdims=True)
    h = (x * lax.rsqrt(ms + RMS_EPS) * g1_ref[...]).astype(BF16)
    cos = cos_ref[...]
    sa = sa_ref[...]
    sb = sb_ref[...]
    bd = bd_ref[...]
    wide = bd.shape[0]

    def qk(col0, gain, out_ref, scale):
        z = _dot(h, wb_ref[:, col0:col0 + ATT_WIDTH])
        for w0 in range(0, ATT_WIDTH, wide):
            zw = z[:, w0:w0 + wide]
            zz = zw * zw
            hi = zz.astype(BF16)
            lo = (zz - hi.astype(F32)).astype(BF16)
            ssq = _dot(hi, bd) + _dot(lo, bd)
            yw = zw * lax.rsqrt(ssq * (1.0 / HEAD_DIM) + RMS_EPS)
            for c in range(wide // LANES):
                y = yw[:, c * LANES:(c + 1) * LANES] * gain
                rot = y * cos + pltpu.roll(y, LANES - ROT_DIM // 2, 1) * sa + pltpu.roll(y, ROT_DIM // 2, 1) * sb
                out_ref[:, w0 + c * LANES:w0 + (c + 1) * LANES] = (rot * scale).astype(BF16)

    qk(0, qg_ref[...], q_ref, HEAD_DIM ** -0.5 * LOG2_E)
    qk(ATT_WIDTH, kg_ref[...], k_ref, 1.0)
    v_ref[...] = _dot(h, wb_ref[:, 2 * ATT_WIDTH:3 * ATT_WIDTH]).astype(BF16)
    u_ref[...] = _dot(h, wb_ref[:, 3 * ATT_WIDTH:]).astype(BF16)


def _in_call(x, g1, w, layer, qg, kg, cos, sa, sb, bd, moe=None):
    t = x.shape[0]
    seq_tiles = cos.shape[0] // ROW_TILE
    row = lambda i: (i, 0)
    fix = lambda i: (0, 0)
    tab = lambda i: (i % seq_tiles, 0)
    out = jax.ShapeDtypeStruct((t, ATT_WIDTH), BF16)
    with_moe = moe is not None
    tiles = t // ROW_TILE
    pos_block = lambda index: pl.BlockSpec((1, 1, ROW_TILE), index, memory_space=pltpu.SMEM)
    extra_in = [pos_block(lambda i: (i, 0, 0)), pos_block(lambda i: (jnp.minimum(i + 1, tiles - 1), 0, 0)),
                pl.BlockSpec(memory_space=pl.ANY)] if with_moe else []
    extra_out = [pl.BlockSpec((ROW_TILE, D_MODEL), row)] if with_moe else []
    extra_shape = [jax.ShapeDtypeStruct((t, D_MODEL), F32)] if with_moe else []
    extra_scratch = [pltpu.VMEM((2 * ROW_TILE * ROW_PITCH, LANES), F32), pltpu.SemaphoreType.DMA((2,))] if with_moe else []
    return pl.pallas_call(
        functools.partial(_in_kernel, with_moe=with_moe),
        grid=(t // ROW_TILE,),
        in_specs=extra_in + [
            pl.BlockSpec((ROW_TILE, D_MODEL), row),
            pl.BlockSpec((1, D_MODEL), fix),
            pl.BlockSpec((None, D_MODEL, IN_WIDTH), lambda i: (layer, 0, 0)),
            pl.BlockSpec((1, LANES), fix),
            pl.BlockSpec((1, LANES), fix),
            pl.BlockSpec((ROW_TILE, LANES), tab),
            pl.BlockSpec((ROW_TILE, LANES), tab),
            pl.BlockSpec((ROW_TILE, LANES), tab),
            pl.BlockSpec(bd.shape, fix),
        ],
        out_specs=[pl.BlockSpec((ROW_TILE, ATT_WIDTH), row)] * 4 + extra_out,
        out_shape=[out] * 4 + extra_shape,
        scratch_shapes=[pltpu.VMEM((D_MODEL, IN_WIDTH), BF16)] + extra_scratch,
        compiler_params=_params("arbitrary"),
        name="in_proj",
    )(*((moe[0], moe[0], moe[1]) if with_moe else ()), x, g1, w, qg, kg, cos, sa, sb, bd)


def _attn_bias(q_off, k_idx, with_prev):
    ok = (k_idx >= q_off) & (k_idx <= q_off + ATT_BLK)
    if not with_prev:
        ok = ok & (k_idx >= ATT_BLK)
    return jnp.where(ok, 0.0, NEG_INF).astype(F32)


def _attn_kernel(q_ref, k_ref, v_ref, o_ref, q32, k32, v32, m_s, l_s, acc_s, bias_s):
    n_per = q32.shape[1]
    pad = ATT_BLK
    zeros = jnp.zeros((pad, LANES), F32)
    for r in range(RES):
        rows = pl.ds(r * n_per, n_per)
        q32[r] = q_ref[rows, :].astype(F32)
        k32[r, pl.ds(0, pad), :] = zeros
        v32[r, pl.ds(0, pad), :] = zeros
        k32[r, pl.ds(pad, n_per), :] = k_ref[rows, :].astype(F32)
        v32[r, pl.ds(pad, n_per), :] = v_ref[rows, :].astype(F32)

    qi = lax.broadcasted_iota(I32, (2 * ATT_BLK, 2 * ATT_BLK), 0) & (ATT_BLK - 1)
    kc = lax.broadcasted_iota(I32, (2 * ATT_BLK, 2 * ATT_BLK), 1)
    offs = (
        (16 * (qi & 7) + (qi >> 3), 16 * (kc & 15) + (kc >> 4)),
        (4 * (qi & 31) + (qi >> 5), 4 * (kc & 63) + (kc >> 6)),
        (qi, kc),
    )
    for br, (qo, ko) in enumerate(offs):
        bias_s[br, 0] = _attn_bias(qo, ko, False)
        bias_s[br, 1] = _attn_bias(qo, ko, True)

    head_a = lax.broadcasted_iota(I32, (ATT_BLK, LANES), 1) < HEAD_DIM

    def block(qb, ks, vs, bias):
        qa = jnp.where(head_a, qb, 0.0)
        qq = jnp.concatenate([qa, qb - qa], axis=0).astype(BF16)
        s = _dot_nt(qq, ks.astype(BF16)) + bias
        m = jnp.max(s, axis=1, keepdims=True)
        p = jnp.exp2(s - m)
        l = jnp.sum(p, axis=1, keepdims=True)
        pv = _dot(p.astype(BF16), vs.astype(BF16))
        m2 = jnp.where(head_a, m[:ATT_BLK], m[ATT_BLK:])
        l2 = jnp.where(head_a, l[:ATT_BLK], l[ATT_BLK:])
        pv2 = jnp.where(head_a, pv[:ATT_BLK], pv[ATT_BLK:])
        return m2, l2, pv2

    def store(br, slab, rows, triple, shape=None):
        for ref, val in zip((m_s, l_s, acc_s), triple):
            ref[br, slab, rows, :] = val if shape is None else val.reshape(shape)

    slabs16 = ATT_STEP_BLOCKS // (n_per // ATT_BLK)

    def body16(rr, carry):
        for i in range(slabs16):
            r = slabs16 * rr + i
            for c in range(n_per // ATT_BLK):
                rows = pl.ds(c * ATT_BLK, ATT_BLK)
                keys = pl.ds(c * ATT_BLK, 2 * ATT_BLK)
                store(2, r, rows, block(q32[r, rows, :], k32[r, keys, :], v32[r, keys, :], bias_s[2, min(c, 1)]))
        return carry

    lax.fori_loop(0, RES // slabs16, body16, 0)

    sub = ATT_BLK // 4
    per4 = ATT_STEP_BLOCKS // 4

    def body4(cc, carry):
        for r4 in range(4):
            slabs = [r4 + 4 * m for m in range(4)]
            for c in range(per4):
                first = pl.multiple_of((cc * per4 + c) * sub, sub)
                rows = pl.ds(first, sub)
                keys = pl.ds(first + pad - sub, 2 * sub)
                cat = lambda ref, idx: jnp.concatenate([ref[s, idx, :] for s in slabs], axis=0)
                bias = bias_s[1, jnp.minimum(cc, 1)] if c == 0 else bias_s[1, 1]
                triple = block(cat(q32, rows), cat(k32, keys), cat(v32, keys), bias)
                for j, s in enumerate(slabs):
                    store(1, s, rows, [x[j * sub:(j + 1) * sub] for x in triple])
        return carry

    lax.fori_loop(0, n_per // (per4 * sub), body4, 0)

    sub1 = ATT_BLK // RES

    def body1(jj, carry):
        for g in range(ATT_STEP_BLOCKS):
            first = pl.multiple_of((jj * ATT_STEP_BLOCKS + g) * sub1, sub1)
            rows = pl.ds(first, sub1)
            keys = pl.ds(first + pad - sub1, 2 * sub1)
            bias = bias_s[0, jnp.minimum(jj, 1)] if g == 0 else bias_s[0, 1]
            triple = block(q32[:, rows, :].reshape(ATT_BLK, LANES), k32[:, keys, :].reshape(2 * ATT_BLK, LANES),
                           v32[:, keys, :].reshape(2 * ATT_BLK, LANES), bias)
            store(0, slice(None), rows, triple, (RES, sub1, LANES))
        return carry

    lax.fori_loop(0, n_per // (ATT_STEP_BLOCKS * sub1), body1, 0)

    for r in range(RES):
        ms = [m_s[br, r] for br in range(3)]
        top = jnp.maximum(jnp.maximum(ms[0], ms[1]), ms[2])
        ws = [jnp.exp2(m - top) for m in ms]
        num = sum(w * acc_s[br, r] for br, w in enumerate(ws))
        den = sum(w * l_s[br, r] for br, w in enumerate(ws))
        o_ref[pl.ds(r * n_per, n_per), :] = (num / den).astype(BF16)


def _attn_call(q, k, v, seq):
    t = q.shape[0]
    n_per = seq // RES
    spec = pl.BlockSpec((seq, LANES), lambda b, h: (b, h))
    staged = pltpu.VMEM((RES, n_per, LANES), F32)
    state = pltpu.VMEM((3, RES, n_per, LANES), F32)
    padded = pltpu.VMEM((RES, n_per + ATT_BLK, LANES), F32)
    return pl.pallas_call(
        _attn_kernel,
        grid=(t // seq, ATT_WIDTH // LANES),
        in_specs=[spec, spec, spec],
        out_specs=spec,
        out_shape=jax.ShapeDtypeStruct((t, ATT_WIDTH), BF16),
        scratch_shapes=[staged, padded, padded, state, state, state,
                        pltpu.VMEM((3, 2, 2 * ATT_BLK, 2 * ATT_BLK), F32)],
        compiler_params=_params("parallel", "parallel"),
        name="dilated_attn",
    )(q, k, v)


def _row_min_index(cond, lane_f):
    return jnp.min(jnp.where(cond, lane_f, float(LANES)), axis=1, keepdims=True)


def _out_kernel(att_ref, u_ref, uh_ref, x_ref, wp_ref, ps_ref, wo_ref, g2_ref, rw_ref, br_ref,
                xo_ref, he_ref, bk_ref, cnt_ref, wob_ref):
    i = pl.program_id(1)

    @pl.when((pl.program_id(0) == 0) & (i == 0))
    def _():
        wob_ref[...] = wo_ref[...].astype(BF16)

    nb = u_ref.shape[2]
    rows = RES * nb
    u = u_ref[0].astype(F32)
    halo = jnp.where(i > 0, uh_ref[0][:, -1:, :].astype(F32), 0.0)
    u_prev = jnp.concatenate([halo, u[:, :nb - 1, :]], axis=1)

    n_idx = lax.broadcasted_iota(I32, (RES, nb, POOL_GROUP_DIM), 1) + i * nb
    r_idx = lax.broadcasted_iota(I32, (RES, nb, POOL_GROUP_DIM), 0)
    p1 = (RES * n_idx + r_idx + 1).astype(F32)

    pools = []
    for g, w in enumerate(POOL_WINDOWS):
        lanes = slice(g * POOL_GROUP_DIM, (g + 1) * POOL_GROUP_DIM)
        ug = u[:, :, lanes]
        upg = u_prev[:, :, lanes]
        tot = ug
        for j in range(1, w):
            tot = tot + jnp.concatenate([upg[RES - j:], ug[:RES - j]], axis=0)
        rg = tot / jnp.minimum(p1, float(w)) - ug
        y = _dot(rg.reshape(rows, POOL_GROUP_DIM).astype(BF16), wp_ref[g])
        pools.append((y * ps_ref[:, lanes]).astype(BF16))
    mix = jnp.concatenate([att_ref[0].reshape(rows, ATT_WIDTH)] + pools, axis=1)
    x = x_ref[0].reshape(rows, D_MODEL) + _dot(mix, wob_ref[...])
    xo_ref[0] = x.reshape(RES, nb, D_MODEL)

    ms = jnp.mean(x * x, axis=-1, keepdims=True)
    h = x * lax.rsqrt(ms + RMS_EPS) * g2_ref[...]

    hh = h.astype(BF16)
    hl = (h - hh.astype(F32)).astype(BF16)
    by_hh = _dot(hh, rw_ref[...])
    logits = by_hh[:, :LANES] + (_dot(hl, rw_ref[:, :LANES]) + by_hh[:, LANES:]) + br_ref[...]
    lane = lax.broadcasted_iota(I32, (rows, LANES), 1)
    lane_f = lane.astype(F32)
    is_g = lane < N_EXPERT_GROUPS
    gl = jnp.where(is_g, logits, -jnp.inf)
    gm = jnp.max(gl, axis=1, keepdims=True)
    g_idx = _row_min_index(is_g & (gl == gm), lane_f)
    p_top = 1.0 / jnp.sum(jnp.where(is_g, jnp.exp(logits - gm), 0.0), axis=1, keepdims=True)
    e_lane = lane - N_EXPERT_GROUPS
    in_grp = (e_lane >= 0) & (e_lane < N_EXPERT_GROUPS * EXPERTS_PER_GROUP) & \
             ((e_lane >> 2).astype(F32) == g_idx)
    el = jnp.where(in_grp, logits, -jnp.inf)
    v1 = jnp.max(el, axis=1, keepdims=True)
    i1 = _row_min_index(in_grp & (el == v1), lane_f)
    rest = in_grp & (lane_f != i1)
    el2 = jnp.where(rest, logits, -jnp.inf)
    v2 = jnp.max(el2, axis=1, keepdims=True)
    i2 = _row_min_index(rest & (el2 == v2), lane_f)
    e21 = jnp.exp(v2 - v1)
    w1 = p_top / (1.0 + e21)
    w2 = p_top * e21 / (1.0 + e21)
    a1 = i1 - N_EXPERT_GROUPS - EXPERTS_PER_GROUP * g_idx
    a2 = i2 - N_EXPERT_GROUPS - EXPERTS_PER_GROUP * g_idx
    first_low = a1 < a2
    lo = jnp.where(first_low, a1, a2)
    hi = jnp.where(first_low, a2, a1)
    w_lo = jnp.where(first_low, w1, w2)
    w_hi = jnp.where(first_low, w2, w1)
    bucket = g_idx * PAIRS_PER_GROUP + lo * 3.0 - lo * (lo - 1.0) * 0.5 + hi - lo - 1.0
    meta = jnp.where(lane == 0, w_lo, jnp.where(lane == 1, w_hi, bucket))
    bk_ref[0] = meta.reshape(RES, nb, LANES)
    cnt_ref[0] = jnp.sum((lane_f == bucket).astype(F32), axis=0, keepdims=True)

    for s in range(ROW_PITCH):
        part = h[:, s * LANES:(s + 1) * LANES] if s < SUBLANES else meta
        for r in range(RES):
            he_ref[0, r, _row_part(nb, s), :] = part[r * nb:(r + 1) * nb]


def _out_call(att, u, x, wp, ps, wo, layer, g2, rw, br, seq):
    t = x.shape[0]
    b = t // seq
    n_per = seq // RES
    nb = ROW_TILE // RES
    halo_rows = 16
    v4 = lambda a: a.reshape(b, RES, n_per, a.shape[-1])
    tile = lambda bi, i: (bi, 0, i, 0)
    halo = lambda bi, i: (bi, 0, jnp.maximum(i * (nb // halo_rows) - 1, 0), 0)
    fix2 = lambda bi, i: (0, 0)
    steps = n_per // nb
    xo, he, bk, cnt = pl.pallas_call(
        _out_kernel,
        grid=(b, steps),
        in_specs=[
            pl.BlockSpec((1, RES, nb, ATT_WIDTH), tile),
            pl.BlockSpec((1, RES, nb, POOL_WIDTH), tile),
            pl.BlockSpec((1, RES, halo_rows, POOL_WIDTH), halo),
            pl.BlockSpec((1, RES, nb, D_MODEL), tile),
            pl.BlockSpec((None, POOL_GROUPS, POOL_GROUP_DIM, POOL_GROUP_DIM), lambda bi, i: (layer, 0, 0, 0)),
            pl.BlockSpec((1, POOL_WIDTH), fix2),
            pl.BlockSpec((None, D_MODEL, D_MODEL), lambda bi, i: (layer, 0, 0)),
            pl.BlockSpec((1, D_MODEL), fix2),
            pl.BlockSpec((D_MODEL, 2 * LANES), fix2),
            pl.BlockSpec((1, LANES), fix2),
        ],
        out_specs=[pl.BlockSpec((1, RES, nb, D_MODEL), tile),
                   pl.BlockSpec((1, RES, nb * ROW_PITCH, LANES), tile),
                   pl.BlockSpec((1, RES, nb, LANES), tile),
                   pl.BlockSpec((1, 1, LANES), lambda bi, i: (bi * steps + i, 0, 0))],
        out_shape=[jax.ShapeDtypeStruct((b, RES, n_per, D_MODEL), F32),
                   jax.ShapeDtypeStruct((b, RES, n_per * ROW_PITCH, LANES), F32),
                   jax.ShapeDtypeStruct((b, RES, n_per, LANES), F32),
                   jax.ShapeDtypeStruct((b * steps, 1, LANES), F32)],
        scratch_shapes=[pltpu.VMEM((D_MODEL, D_MODEL), BF16)],
        compiler_params=_params("arbitrary", "arbitrary"),
        name="out_proj_router",
    )(v4(att), v4(u), v4(u), v4(x), wp, ps, wo, g2, rw, br)
    return xo.reshape(t, D_MODEL), he.reshape(t * ROW_PITCH, LANES), bk.reshape(t, LANES), cnt


def _sort_kernel(meta_ref, counts_ref, pos_ref, tinfo_ref, cnt_s, off_s):
    i = pl.program_id(0)
    rows = meta_ref.shape[0]
    lane = lax.broadcasted_iota(I32, (rows, LANES), 1)
    onehot = lane.astype(F32) == meta_ref[:, 2:3]
    oh = onehot.astype(F32)
    tile_count = jnp.sum(oh, axis=0, keepdims=True)

    @pl.when(i == 0)
    def _():
        totals = jnp.sum(counts_ref[...], axis=0)
        tiles = jnp.floor((totals + (MOE_TILE - 1.0)) * (1.0 / MOE_TILE))
        tiles8 = jnp.broadcast_to(tiles, (SUBLANES, LANES)).astype(BF16)
        sq = (LANES, LANES)
        before = lax.broadcasted_iota(I32, sq, 0) < lax.broadcasted_iota(I32, sq, 1)
        start = _dot(tiles8, before.astype(BF16))
        off_s[...] = start[0:1] * float(MOE_TILE)
        cnt_s[...] = jnp.zeros_like(cnt_s)
        end = (start + tiles8.astype(F32)).astype(BF16)
        eye = (lax.broadcasted_iota(I32, sq, 0) == lax.broadcasted_iota(I32, sq, 1)).astype(BF16)
        end_col = _dot_nt(eye, end)[:, 0:1]
        tile_id = lax.broadcasted_iota(I32, sq, 1).astype(F32)
        tile_bucket = jnp.sum((end_col <= tile_id).astype(F32), axis=0, keepdims=True)
        total = jnp.max(end.astype(F32)[0:1], axis=1, keepdims=True)
        row_lane = lax.broadcasted_iota(I32, (1, LANES), 1)
        ends = pltpu.roll(end.astype(F32)[0:1], END_LANE, 1)
        is_end = (row_lane >= END_LANE) & (row_lane < END_LANE + N_BUCKETS)
        tinfo_ref[...] = jnp.where(row_lane == LANES - 1, total, jnp.where(is_end, ends, tile_bucket)).astype(I32)

    sq = (rows, rows)
    upto = lax.broadcasted_iota(I32, sq, 1) <= lax.broadcasted_iota(I32, sq, 0)
    prefix = _dot(upto.astype(BF16), oh.astype(BF16))
    posv = jnp.where(onehot, prefix - 1.0 + cnt_s[...] + off_s[...], 0.0)
    hi = jnp.floor(posv * (1.0 / 256.0))
    lo = posv - hi * 256.0
    ones = jnp.ones((SUBLANES, LANES), BF16)
    pos = _dot_nt(ones, hi.astype(BF16)) * 256.0 + _dot_nt(ones, lo.astype(BF16))
    pos_ref[0] = pos[0:1].astype(I32)
    cnt_s[...] += tile_count


def _sort_call(meta, counts):
    t = meta.shape[0]
    n_tiles = t // ROW_TILE
    return pl.pallas_call(
        _sort_kernel,
        grid=(n_tiles,),
        in_specs=[pl.BlockSpec((ROW_TILE, LANES), lambda i: (i, 0)),
                  pl.BlockSpec(counts.shape, lambda i: (0, 0, 0))],
        out_specs=[pl.BlockSpec((1, 1, ROW_TILE), lambda i: (i, 0, 0)),
                   pl.BlockSpec((1, LANES), lambda i: (0, 0))],
        out_shape=[jax.ShapeDtypeStruct((n_tiles, 1, ROW_TILE), I32),
                   jax.ShapeDtypeStruct((1, LANES), I32)],
        scratch_shapes=[pltpu.VMEM((1, LANES), F32), pltpu.VMEM((1, LANES), F32)],
        compiler_params=_params("arbitrary"),
        name="bucket_sort",
    )(meta, counts)


def _scatter_kernel(tinfo_ref, pos_ref, he_ref, hs_ref, zeros_ref, sem, zsem):
    rows = he_ref.shape[0] // ROW_PITCH

    @pl.when(pl.program_id(0) == 0)
    def _():
        zeros_ref[...] = jnp.zeros_like(zeros_ref)

        def tile_fill(j):
            return pltpu.make_async_copy(zeros_ref, _row_tile(hs_ref, j * MOE_TILE, MOE_TILE), zsem)

        def fill(b):
            end = tinfo_ref[0, END_LANE + b]
            begin = tinfo_ref[0, END_LANE + b - 1] if b else 0
            return end > begin, tile_fill(end - 1)

        unused = tile_fill
        tiles = hs_ref.shape[0] // (MOE_TILE * ROW_PITCH)
        used = tinfo_ref[0, LANES - 1]
        for b in range(N_BUCKETS):
            nonempty, copy = fill(b)
            pl.when(nonempty)(copy.start)
        lax.fori_loop(used, tiles, lambda j, c: (unused(j).start(), c)[1], 0)
        for b in range(N_BUCKETS):
            nonempty, copy = fill(b)
            pl.when(nonempty)(copy.wait)
        lax.fori_loop(used, tiles, lambda j, c: (unused(j).wait(), c)[1], 0)

    def start(g, c):
        for u in range(ISSUE_UNROLL):
            t = g * ISSUE_UNROLL + u
            pltpu.make_async_copy(_row_tile(he_ref, t), _row_tile(hs_ref, pos_ref[0, 0, t]),
                                  sem).start(priority=u % 2)
        return c

    lax.fori_loop(0, rows // ISSUE_UNROLL, start, 0)
    pltpu.make_async_copy(he_ref, _row_tile(hs_ref, 0, rows), sem).wait()


def _scatter_call(tinfo, pos, he, sorted_rows):
    t = he.shape[0] // ROW_PITCH
    return pl.pallas_call(
        _scatter_kernel,
        grid=(t // ROW_TILE,),
        in_specs=[pl.BlockSpec(memory_space=pltpu.SMEM),
                  pl.BlockSpec((1, 1, ROW_TILE), lambda i: (i, 0, 0), memory_space=pltpu.SMEM),
                  pl.BlockSpec((ROW_TILE * ROW_PITCH, LANES), lambda i: (i, 0))],
        out_specs=pl.BlockSpec(memory_space=pl.ANY),
        out_shape=jax.ShapeDtypeStruct((sorted_rows * ROW_PITCH, LANES), F32),
        scratch_shapes=[pltpu.VMEM((MOE_TILE * ROW_PITCH, LANES), F32), pltpu.SemaphoreType.DMA(()),
                        pltpu.SemaphoreType.DMA(())],
        compiler_params=_params("arbitrary"),
        name="row_scatter",
    )(tinfo, pos, he)


def _gather_kernel(pos_ref, x_ref, ys_ref, o_ref, buf, sem):
    rows = buf.shape[0] // ROW_PITCH
    _issue_row_gather(ys_ref, pos_ref, 0, buf, 0, sem)
    pltpu.make_async_copy(_row_tile(ys_ref, 0, rows), buf, sem).wait()
    for s in range(SUBLANES):
        lanes = slice(s * LANES, (s + 1) * LANES)
        o_ref[:, lanes] = x_ref[:, lanes] + buf[_row_part(rows, s), :]


def _gather_call(pos, x, ys):
    t = x.shape[0]
    return pl.pallas_call(
        _gather_kernel,
        grid=(t // ROW_TILE,),
        in_specs=[pl.BlockSpec((1, 1, ROW_TILE), lambda i: (i, 0, 0), memory_space=pltpu.SMEM),
                  pl.BlockSpec((ROW_TILE, D_MODEL), lambda i: (i, 0)),
                  pl.BlockSpec(memory_space=pl.ANY)],
        out_specs=pl.BlockSpec((ROW_TILE, D_MODEL), lambda i: (i, 0)),
        out_shape=jax.ShapeDtypeStruct((t, D_MODEL), F32),
        scratch_shapes=[pltpu.VMEM((ROW_TILE * ROW_PITCH, LANES), F32), pltpu.SemaphoreType.DMA(())],
        compiler_params=_params("arbitrary"),
        name="row_gather_residual",
    )(pos, x, ys)


def _tile_group(j, tinfo):
    used = tinfo[LANES - 1]
    return tinfo[jnp.minimum(j, used - 1)] // PAIRS_PER_GROUP


def _moe_kernel(tinfo, hs_ref, wg32_ref, wu32_ref, wd32_ref, ys_ref, wg_ref, wu_ref, wd_ref):
    j = pl.program_id(0)
    used = tinfo[LANES - 1]

    @pl.when((j == 0) | ((j < used) & (_tile_group(j, tinfo) != _tile_group(jnp.maximum(j, 1) - 1, tinfo))))
    def _():
        wg_ref[...] = wg32_ref[...].astype(BF16)
        wu_ref[...] = wu32_ref[...].astype(BF16)
        wd_ref[...] = wd32_ref[...].astype(BF16)

    @pl.when(j < used)
    def _():
        pair = tinfo[j] % PAIRS_PER_GROUP
        e_lo = (pair >= 3).astype(I32) + (pair >= 5).astype(I32)
        e_hi = pair - (e_lo * 3 - (e_lo * (e_lo - 1)) // 2) + e_lo + 1
        xt = jnp.concatenate([hs_ref[_row_part(MOE_TILE, s), :].astype(BF16) for s in range(SUBLANES)], axis=1)

        scalars = hs_ref[_row_part(MOE_TILE, SUBLANES), :]

        def expert(e, gate):
            hg = _dot(xt, wg_ref[e])
            hu = _dot(xt, wu_ref[e])
            act = hg * (1.0 / (1.0 + jnp.exp(-hg))) * hu * gate
            return _dot(act.astype(BF16), wd_ref[e])

        y = expert(e_lo, scalars[:, 0:1]) + expert(e_hi, scalars[:, 1:2])
        for s in range(SUBLANES):
            ys_ref[_row_part(MOE_TILE, s), :] = y[:, s * LANES:(s + 1) * LANES]
        ys_ref[_row_part(MOE_TILE, SUBLANES), :] = jnp.zeros((MOE_TILE, LANES), F32)

    @pl.when(j >= used)
    def _():
        ys_ref[...] = jnp.zeros_like(ys_ref)


def _moe_call(tinfo, hs, wg, wu, wd, layer):
    sorted_rows = hs.shape[0] // ROW_PITCH
    row = lambda j, ti: (jnp.minimum(j, ti[LANES - 1] - 1), 0)
    grp = lambda j, ti: (layer, _tile_group(j, ti), 0, 0, 0)
    return pl.pallas_call(
        _moe_kernel,
        grid_spec=pltpu.PrefetchScalarGridSpec(
            num_scalar_prefetch=1,
            grid=(sorted_rows // MOE_TILE,),
            in_specs=[pl.BlockSpec((MOE_TILE * ROW_PITCH, LANES), row),
                      pl.BlockSpec((None, None, EXPERTS_PER_GROUP, D_MODEL, D_EXPERT), grp),
                      pl.BlockSpec((None, None, EXPERTS_PER_GROUP, D_MODEL, D_EXPERT), grp),
                      pl.BlockSpec((None, None, EXPERTS_PER_GROUP, D_EXPERT, D_MODEL), grp)],
            out_specs=pl.BlockSpec((MOE_TILE * ROW_PITCH, LANES), lambda j, ti: (j, 0)),
            scratch_shapes=[pltpu.VMEM((EXPERTS_PER_GROUP, D_MODEL, D_EXPERT), BF16),
                            pltpu.VMEM((EXPERTS_PER_GROUP, D_MODEL, D_EXPERT), BF16),
                            pltpu.VMEM((EXPERTS_PER_GROUP, D_EXPERT, D_MODEL), BF16)],
        ),
        out_shape=jax.ShapeDtypeStruct((sorted_rows * ROW_PITCH, LANES), F32),
        compiler_params=_params("arbitrary"),
        name="grouped_experts",
    )(tinfo, hs, wg, wu, wd)


def _rope_tables(seq):
    half = ROT_DIM // 2
    inv_freq = ROPE_THETA ** (-jnp.arange(0, ROT_DIM, 2, dtype=F32) / ROT_DIM)
    row = jnp.arange(seq)
    pos = (RES * (row % (seq // RES)) + row // (seq // RES)).astype(F32)
    ang = pos[:, None] * inv_freq[None, :]
    cos, sin = jnp.cos(ang), jnp.sin(ang)
    d = jnp.arange(LANES) % HEAD_DIM
    cos_l = jnp.where(d[None, :] < ROT_DIM, cos[:, d % half], 1.0)
    sin_l = sin[:, d % half]
    sa = jnp.where(d[None, :] < half, -sin_l, 0.0)
    sb = jnp.where((d[None, :] >= half) & (d[None, :] < ROT_DIM), sin_l, 0.0)
    return cos_l, sa, sb


def kernel(x, norm1_gain, w_in, q_norm_gain, k_norm_gain, w_pool, pool_scale, w_out, norm2_gain, w_group, b_group, w_router, b_router, w_gate, w_up, w_down):
    b, seq, d = x.shape
    depth = w_in.shape[0]
    t = b * seq
    n_per = seq // RES
    assert d == D_MODEL and seq % (RES * ATT_BLK) == 0 and t % ROW_TILE == 0
    sorted_rows = t + N_BUCKETS * MOE_TILE

    cos, sa, sb = _rope_tables(seq)
    lane_head = jnp.arange(MXU_WIDTH) // HEAD_DIM
    block_diag = (lane_head[:, None] == lane_head[None, :]).astype(BF16)

    w_pool_b = w_pool.astype(BF16)
    grouped = lambda w: w.reshape(depth, N_EXPERT_GROUPS, EXPERTS_PER_GROUP, *w.shape[2:])
    w_gate_g, w_up_g, w_down_g = grouped(w_gate), grouped(w_up), grouped(w_down)
    n_logits = N_EXPERT_GROUPS * (1 + EXPERTS_PER_GROUP)
    w_r = jnp.pad(jnp.concatenate([w_group, w_router], axis=-1), ((0, 0), (0, 0), (0, LANES - n_logits)))
    w_r_hi = w_r.astype(BF16)
    w_r_lo = (w_r - w_r_hi.astype(F32)).astype(BF16)
    w_r_split = jnp.concatenate([w_r_hi, w_r_lo], axis=-1)
    b_r = jnp.pad(jnp.concatenate([b_group, b_router], axis=-1), ((0, 0), (0, LANES - n_logits)))
    two_heads = lambda g: jnp.tile(g, (1, LANES // HEAD_DIM))

    xr = x.reshape(b, n_per, RES, d).transpose(0, 2, 1, 3).reshape(t, d)
    moe = None
    for l in range(depth):
        q, k, v, u, *x_new = _in_call(xr, norm1_gain[l:l + 1], w_in, l, two_heads(q_norm_gain[l:l + 1]),
                                      two_heads(k_norm_gain[l:l + 1]), cos, sa, sb, block_diag, moe)
        xr = x_new[0] if x_new else xr
        att = _attn_call(q, k, v, seq)
        xr, he, meta, counts = _out_call(att, u, xr, w_pool_b, pool_scale[l:l + 1], w_out, l, norm2_gain[l:l + 1],
                                         w_r_split[l], b_r[l:l + 1], seq)
        pos, tinfo = _sort_call(meta, counts)
        hs = _scatter_call(tinfo, pos, he, sorted_rows)
        ys = _moe_call(tinfo.reshape(LANES), hs, w_gate_g, w_up_g, w_down_g, l)
        moe = (pos, ys)
    natural = lambda a: a.reshape(b, RES, n_per, -1).transpose(0, 2, 1, 3)
    out = _gather_call(natural(pos).reshape(pos.shape), natural(xr).reshape(t, d), ys)
    return out.reshape(b, seq, d)
```

```python
import functools

import jax
import jax.numpy as jnp
from jax import lax
from jax.experimental import pallas as pl
from jax.experimental.pallas import tpu as pltpu

D_MODEL = 1024
N_HEADS = 8
HEAD_DIM = 64
ATT_WIDTH = N_HEADS * HEAD_DIM
POOL_GROUPS = 4
POOL_GROUP_DIM = 128
POOL_WIDTH = POOL_GROUPS * POOL_GROUP_DIM
POOL_WINDOWS = (2, 4, 8, 16)
IN_WIDTH = 3 * ATT_WIDTH + POOL_WIDTH
ROT_DIM = 16
ROPE_THETA = 500000.0
N_EXPERT_GROUPS = 4
EXPERTS_PER_GROUP = 4
D_EXPERT = 256
RMS_EPS = 1e-6
NEG_INF = -1e30
LOG2_E = 1.4426950408889634

LANES = 128
SUBLANES = 8
MXU_WIDTH = 256
RES = 16
ATT_BLK = 128
ATT_STEP_BLOCKS = 32
PAIRS_PER_GROUP = 6
N_BUCKETS = N_EXPERT_GROUPS * PAIRS_PER_GROUP
ROW_TILE = 512
MOE_TILE = 256
ISSUE_UNROLL = 8
ROW_PITCH = SUBLANES + 1
END_LANE = 96
VMEM_LIMIT = 56 * 1024 * 1024

F32 = jnp.float32
BF16 = jnp.bfloat16
I32 = jnp.int32


def _dot(a, b):
    return jnp.dot(a, b, preferred_element_type=F32)


def _dot_nt(a, b):
    return lax.dot_general(a, b, (((1,), (1,)), ((), ())), preferred_element_type=F32)


def _row_tile(ref, row, n=1):
    return ref.at[pl.ds(row * ROW_PITCH, n * ROW_PITCH), :]


def _row_part(n, s):
    return pl.ds(s, n, stride=ROW_PITCH)


def _params(*sem):
    return pltpu.CompilerParams(dimension_semantics=sem, vmem_limit_bytes=VMEM_LIMIT)


def _issue_row_gather(ys_ref, pos_ref, tile, buf, first, sem):
    def start(g, c):
        for u in range(ISSUE_UNROLL):
            t = g * ISSUE_UNROLL + u
            pltpu.make_async_copy(_row_tile(ys_ref, pos_ref[tile, 0, t]), _row_tile(buf, first + t),
                                  sem).start(priority=u % 2)
        return c

    lax.fori_loop(0, ROW_TILE // ISSUE_UNROLL, start, 0)


def _in_kernel(*refs, with_moe):
    if with_moe:
        pos_ref, next_pos_ref, ys_ref, refs = refs[0], refs[1], refs[2], refs[3:]
        xo_ref, buf, sem = refs[-4], refs[-2], refs[-1]
        refs = refs[:-4] + (refs[-3],)
    x_ref, g1_ref, w_ref, qg_ref, kg_ref, cos_ref, sa_ref, sb_ref, bd_ref, q_ref, k_ref, v_ref, u_ref, wb_ref = refs
    i = pl.program_id(0)

    if with_moe:
        slot_rows = ROW_TILE

        def issue(tile_pos_ref, slot):
            _issue_row_gather(ys_ref, tile_pos_ref, 0, buf, slot * slot_rows, sem.at[slot])

        pl.when(i == 0)(lambda: issue(pos_ref, 0))
        has_next = i + 1 < pl.num_programs(0)
        pl.when(has_next & (i % 2 == 1))(lambda: issue(next_pos_ref, 0))
        pl.when(has_next & (i % 2 == 0))(lambda: issue(next_pos_ref, 1))

    @pl.when(i == 0)
    def _():
        wb_ref[...] = w_ref[...].astype(BF16)

    if with_moe:
        first = (i % 2) * slot_rows
        pltpu.make_async_copy(_row_tile(ys_ref, 0, slot_rows), _row_tile(buf, first, slot_rows), sem.at[i % 2]).wait()
        x = jnp.concatenate([x_ref[:, s * LANES:(s + 1) * LANES]
                             + buf[pl.ds(first * ROW_PITCH + s, slot_rows, stride=ROW_PITCH), :]
                             for s in range(SUBLANES)], axis=1)
        xo_ref[...] = x
    else:
        x = x_ref[...]
    ms = jnp.mean(x * x, axis=-1, keepdims=True)
    h = (x * lax.rsqrt(ms + RMS_EPS) * g1_ref[...]).astype(BF16)
    cos = cos_ref[...]
    sa = sa_ref[...]
    sb = sb_ref[...]
    bd = bd_ref[...]
    wide = bd.shape[0]

    def qk(col0, gain, out_ref, scale):
        z = _dot(h, wb_ref[:, col0:col0 + ATT_WIDTH])
        for w0 in range(0, ATT_WIDTH, wide):
            zw = z[:, w0:w0 + wide]
            ssq = _dot((zw * zw).astype(BF16), bd)
            yw = zw * lax.rsqrt(ssq * (1.0 / HEAD_DIM) + RMS_EPS)
            for c in range(wide // LANES):
                y = yw[:, c * LANES:(c + 1) * LANES] * gain
                rot = y * cos + pltpu.roll(y, LANES - ROT_DIM // 2, 1) * sa + pltpu.roll(y, ROT_DIM // 2, 1) * sb
                out_ref[:, w0 + c * LANES:w0 + (c + 1) * LANES] = (rot * scale).astype(BF16)

    qk(0, qg_ref[...], q_ref, HEAD_DIM ** -0.5 * LOG2_E)
    qk(ATT_WIDTH, kg_ref[...], k_ref, 1.0)
    v_ref[...] = _dot(h, wb_ref[:, 2 * ATT_WIDTH:3 * ATT_WIDTH]).astype(BF16)
    u_ref[...] = _dot(h, wb_ref[:, 3 * ATT_WIDTH:]).astype(BF16)


def _in_call(x, g1, w, layer, qg, kg, cos, sa, sb, bd, moe=None):
    t = x.shape[0]
    seq_tiles = cos.shape[0] // ROW_TILE
    row = lambda i: (i, 0)
    fix = lambda i: (0, 0)
    tab = lambda i: (i % seq_tiles, 0)
    out = jax.ShapeDtypeStruct((t, ATT_WIDTH), BF16)
    with_moe = moe is not None
    tiles = t // ROW_TILE
    pos_block = lambda index: pl.BlockSpec((1, 1, ROW_TILE), index, memory_space=pltpu.SMEM)
    extra_in = [pos_block(lambda i: (i, 0, 0)), pos_block(lambda i: (jnp.minimum(i + 1, tiles - 1), 0, 0)),
                pl.BlockSpec(memory_space=pl.ANY)] if with_moe else []
    extra_out = [pl.BlockSpec((ROW_TILE, D_MODEL), row)] if with_moe else []
    extra_shape = [jax.ShapeDtypeStruct((t, D_MODEL), F32)] if with_moe else []
    extra_scratch = [pltpu.VMEM((2 * ROW_TILE * ROW_PITCH, LANES), F32), pltpu.SemaphoreType.DMA((2,))] if with_moe else []
    return pl.pallas_call(
        functools.partial(_in_kernel, with_moe=with_moe),
        grid=(t // ROW_TILE,),
        in_specs=extra_in + [
            pl.BlockSpec((ROW_TILE, D_MODEL), row),
            pl.BlockSpec((1, D_MODEL), fix),
            pl.BlockSpec((None, D_MODEL, IN_WIDTH), lambda i: (layer, 0, 0)),
            pl.BlockSpec((1, LANES), fix),
            pl.BlockSpec((1, LANES), fix),
            pl.BlockSpec((ROW_TILE, LANES), tab),
            pl.BlockSpec((ROW_TILE, LANES), tab),
            pl.BlockSpec((ROW_TILE, LANES), tab),
            pl.BlockSpec(bd.shape, fix),
        ],
        out_specs=[pl.BlockSpec((ROW_TILE, ATT_WIDTH), row)] * 4 + extra_out,
        out_shape=[out] * 4 + extra_shape,
        scratch_shapes=[pltpu.VMEM((D_MODEL, IN_WIDTH), BF16)] + extra_scratch,
        compiler_params=_params("arbitrary"),
        name="in_proj",
    )(*((moe[0], moe[0], moe[1]) if with_moe else ()), x, g1, w, qg, kg, cos, sa, sb, bd)


def _attn_bias(q_off, k_idx, with_prev):
    ok = (k_idx >= q_off) & (k_idx <= q_off + ATT_BLK)
    if not with_prev:
        ok = ok & (k_idx >= ATT_BLK)
    return jnp.where(ok, 0.0, NEG_INF).astype(F32)


def _attn_kernel(q_ref, k_ref, v_ref, o_ref, q32, k32, v32, m_s, l_s, acc_s, bias_s):
    n_per = q32.shape[1]
    pad = ATT_BLK
    zeros = jnp.zeros((pad, LANES), F32)
    for r in range(RES):
        rows = pl.ds(r * n_per, n_per)
        q32[r] = q_ref[rows, :].astype(F32)
        k32[r, pl.ds(0, pad), :] = zeros
        v32[r, pl.ds(0, pad), :] = zeros
        k32[r, pl.ds(pad, n_per), :] = k_ref[rows, :].astype(F32)
        v32[r, pl.ds(pad, n_per), :] = v_ref[rows, :].astype(F32)

    qi = lax.broadcasted_iota(I32, (2 * ATT_BLK, 2 * ATT_BLK), 0) & (ATT_BLK - 1)
    kc = lax.broadcasted_iota(I32, (2 * ATT_BLK, 2 * ATT_BLK), 1)
    offs = (
        (16 * (qi & 7) + (qi >> 3), 16 * (kc & 15) + (kc >> 4)),
        (4 * (qi & 31) + (qi >> 5), 4 * (kc & 63) + (kc >> 6)),
        (qi, kc),
    )
    for br, (qo, ko) in enumerate(offs):
        bias_s[br, 0] = _attn_bias(qo, ko, False)
        bias_s[br, 1] = _attn_bias(qo, ko, True)

    head_a = lax.broadcasted_iota(I32, (ATT_BLK, LANES), 1) < HEAD_DIM

    def block(qb, ks, vs, bias):
        qa = jnp.where(head_a, qb, 0.0)
        qq = jnp.concatenate([qa, qb - qa], axis=0).astype(BF16)
        s = _dot_nt(qq, ks.astype(BF16)) + bias
        m = jnp.max(s, axis=1, keepdims=True)
        p = jnp.exp2(s - m)
        l = jnp.sum(p, axis=1, keepdims=True)
        pv = _dot(p.astype(BF16), vs.astype(BF16))
        m2 = jnp.where(head_a, m[:ATT_BLK], m[ATT_BLK:])
        l2 = jnp.where(head_a, l[:ATT_BLK], l[ATT_BLK:])
        pv2 = jnp.where(head_a, pv[:ATT_BLK], pv[ATT_BLK:])
        return m2, l2, pv2

    def store(br, slab, rows, triple, shape=None):
        for ref, val in zip((m_s, l_s, acc_s), triple):
            ref[br, slab, rows, :] = val if shape is None else val.reshape(shape)

    slabs16 = ATT_STEP_BLOCKS // (n_per // ATT_BLK)

    def body16(rr, carry):
        for i in range(slabs16):
            r = slabs16 * rr + i
            for c in range(n_per // ATT_BLK):
                rows = pl.ds(c * ATT_BLK, ATT_BLK)
                keys = pl.ds(c * ATT_BLK, 2 * ATT_BLK)
                store(2, r, rows, block(q32[r, rows, :], k32[r, keys, :], v32[r, keys, :], bias_s[2, min(c, 1)]))
        return carry

    lax.fori_loop(0, RES // slabs16, body16, 0)

    sub = ATT_BLK // 4
    per4 = ATT_STEP_BLOCKS // 4

    def body4(cc, carry):
        for r4 in range(4):
            slabs = [r4 + 4 * m for m in range(4)]
            for c in range(per4):
                first = pl.multiple_of((cc * per4 + c) * sub, sub)
                rows = pl.ds(first, sub)
                keys = pl.ds(first + pad - sub, 2 * sub)
                cat = lambda ref, idx: jnp.concatenate([ref[s, idx, :] for s in slabs], axis=0)
                bias = bias_s[1, jnp.minimum(cc, 1)] if c == 0 else bias_s[1, 1]
                triple = block(cat(q32, rows), cat(k32, keys), cat(v32, keys), bias)
                for j, s in enumerate(slabs):
                    store(1, s, rows, [x[j * sub:(j + 1) * sub] for x in triple])
        return carry

    lax.fori_loop(0, n_per // (per4 * sub), body4, 0)

    sub1 = ATT_BLK // RES

    def body1(jj, carry):
        for g in range(ATT_STEP_BLOCKS):
            first = pl.multiple_of((jj * ATT_STEP_BLOCKS + g) * sub1, sub1)
            rows = pl.ds(first, sub1)
            keys = pl.ds(first + pad - sub1, 2 * sub1)
            bias = bias_s[0, jnp.minimum(jj, 1)] if g == 0 else bias_s[0, 1]
            triple = block(q32[:, rows, :].reshape(ATT_BLK, LANES), k32[:, keys, :].reshape(2 * ATT_BLK, LANES),
                           v32[:, keys, :].reshape(2 * ATT_BLK, LANES), bias)
            store(0, slice(None), rows, triple, (RES, sub1, LANES))
        return carry

    lax.fori_loop(0, n_per // (ATT_STEP_BLOCKS * sub1), body1, 0)

    for r in range(RES):
        ms = [m_s[br, r] for br in range(3)]
        top = jnp.maximum(jnp.maximum(ms[0], ms[1]), ms[2])
        ws = [jnp.exp2(m - top) for m in ms]
        num = sum(w * acc_s[br, r] for br, w in enumerate(ws))
        den = sum(w * l_s[br, r] for br, w in enumerate(ws))
        o_ref[pl.ds(r * n_per, n_per), :] = (num / den).astype(BF16)


def _attn_call(q, k, v, seq):
    t = q.shape[0]
    n_per = seq // RES
    spec = pl.BlockSpec((seq, LANES), lambda b, h: (b, h))
    staged = pltpu.VMEM((RES, n_per, LANES), F32)
    state = pltpu.VMEM((3, RES, n_per, LANES), F32)
    padded = pltpu.VMEM((RES, n_per + ATT_BLK, LANES), F32)
    return pl.pallas_call(
        _attn_kernel,
        grid=(t // seq, ATT_WIDTH // LANES),
        in_specs=[spec, spec, spec],
        out_specs=spec,
        out_shape=jax.ShapeDtypeStruct((t, ATT_WIDTH), BF16),
        scratch_shapes=[staged, padded, padded, state, state, state,
                        pltpu.VMEM((3, 2, 2 * ATT_BLK, 2 * ATT_BLK), F32)],
        compiler_params=_params("parallel", "parallel"),
        name="dilated_attn",
    )(q, k, v)


def _row_min_index(cond, lane_f):
    return jnp.min(jnp.where(cond, lane_f, float(LANES)), axis=1, keepdims=True)


def _out_kernel(att_ref, u_ref, uh_ref, x_ref, wp_ref, ps_ref, wo_ref, g2_ref, rw_ref, br_ref,
                xo_ref, he_ref, bk_ref, cnt_ref, wob_ref):
    i = pl.program_id(1)

    @pl.when((pl.program_id(0) == 0) & (i == 0))
    def _():
        wob_ref[...] = wo_ref[...].astype(BF16)

    nb = u_ref.shape[2]
    rows = RES * nb
    u = u_ref[0].astype(F32)
    halo = jnp.where(i > 0, uh_ref[0][:, -1:, :].astype(F32), 0.0)
    u_prev = jnp.concatenate([halo, u[:, :nb - 1, :]], axis=1)

    n_idx = lax.broadcasted_iota(I32, (RES, nb, POOL_GROUP_DIM), 1) + i * nb
    r_idx = lax.broadcasted_iota(I32, (RES, nb, POOL_GROUP_DIM), 0)
    p1 = (RES * n_idx + r_idx + 1).astype(F32)

    pools = []
    for g, w in enumerate(POOL_WINDOWS):
        lanes = slice(g * POOL_GROUP_DIM, (g + 1) * POOL_GROUP_DIM)
        ug = u[:, :, lanes]
        upg = u_prev[:, :, lanes]
        tot = ug
        for j in range(1, w):
            tot = tot + jnp.concatenate([upg[RES - j:], ug[:RES - j]], axis=0)
        rg = tot / jnp.minimum(p1, float(w)) - ug
        y = _dot(rg.reshape(rows, POOL_GROUP_DIM).astype(BF16), wp_ref[g])
        pools.append((y * ps_ref[:, lanes]).astype(BF16))
    mix = jnp.concatenate([att_ref[0].reshape(rows, ATT_WIDTH)] + pools, axis=1)
    x = x_ref[0].reshape(rows, D_MODEL) + _dot(mix, wob_ref[...])
    xo_ref[0] = x.reshape(RES, nb, D_MODEL)

    ms = jnp.mean(x * x, axis=-1, keepdims=True)
    h = x * lax.rsqrt(ms + RMS_EPS) * g2_ref[...]

    hh = h.astype(BF16)
    hl = (h - hh.astype(F32)).astype(BF16)
    by_hh = _dot(hh, rw_ref[...])
    logits = by_hh[:, :LANES] + (_dot(hl, rw_ref[:, :LANES]) + by_hh[:, LANES:]) + br_ref[...]
    lane = lax.broadcasted_iota(I32, (rows, LANES), 1)
    lane_f = lane.astype(F32)
    is_g = lane < N_EXPERT_GROUPS
    gl = jnp.where(is_g, logits, -jnp.inf)
    gm = jnp.max(gl, axis=1, keepdims=True)
    g_idx = _row_min_index(is_g & (gl == gm), lane_f)
    p_top = 1.0 / jnp.sum(jnp.where(is_g, jnp.exp(logits - gm), 0.0), axis=1, keepdims=True)
    e_lane = lane - N_EXPERT_GROUPS
    in_grp = (e_lane >= 0) & (e_lane < N_EXPERT_GROUPS * EXPERTS_PER_GROUP) & \
             ((e_lane >> 2).astype(F32) == g_idx)
    el = jnp.where(in_grp, logits, -jnp.inf)
    v1 = jnp.max(el, axis=1, keepdims=True)
    i1 = _row_min_index(in_grp & (el == v1), lane_f)
    rest = in_grp & (lane_f != i1)
    el2 = jnp.where(rest, logits, -jnp.inf)
    v2 = jnp.max(el2, axis=1, keepdims=True)
    i2 = _row_min_index(rest & (el2 == v2), lane_f)
    e21 = jnp.exp(v2 - v1)
    w1 = p_top / (1.0 + e21)
    w2 = p_top * e21 / (1.0 + e21)
    a1 = i1 - N_EXPERT_GROUPS - EXPERTS_PER_GROUP * g_idx
    a2 = i2 - N_EXPERT_GROUPS - EXPERTS_PER_GROUP * g_idx
    first_low = a1 < a2
    lo = jnp.where(first_low, a1, a2)
    hi = jnp.where(first_low, a2, a1)
    w_lo = jnp.where(first_low, w1, w2)
    w_hi = jnp.where(first_low, w2, w1)
    bucket = g_idx * PAIRS_PER_GROUP + lo * 3.0 - lo * (lo - 1.0) * 0.5 + hi - lo - 1.0
    meta = jnp.where(lane == 0, w_lo, jnp.where(lane == 1, w_hi, bucket))
    bk_ref[0] = meta.reshape(RES, nb, LANES)
    cnt_ref[0] = jnp.sum((lane_f == bucket).astype(F32), axis=0, keepdims=True)

    for s in range(ROW_PITCH):
        part = h[:, s * LANES:(s + 1) * LANES] if s < SUBLANES else meta
        for r in range(RES):
            he_ref[0, r, _row_part(nb, s), :] = part[r * nb:(r + 1) * nb]


def _out_call(att, u, x, wp, ps, wo, layer, g2, rw, br, seq):
    t = x.shape[0]
    b = t // seq
    n_per = seq // RES
    nb = ROW_TILE // RES
    halo_rows = 16
    v4 = lambda a: a.reshape(b, RES, n_per, a.shape[-1])
    tile = lambda bi, i: (bi, 0, i, 0)
    halo = lambda bi, i: (bi, 0, jnp.maximum(i * (nb // halo_rows) - 1, 0), 0)
    fix2 = lambda bi, i: (0, 0)
    steps = n_per // nb
    xo, he, bk, cnt = pl.pallas_call(
        _out_kernel,
        grid=(b, steps),
        in_specs=[
            pl.BlockSpec((1, RES, nb, ATT_WIDTH), tile),
            pl.BlockSpec((1, RES, nb, POOL_WIDTH), tile),
            pl.BlockSpec((1, RES, halo_rows, POOL_WIDTH), halo),
            pl.BlockSpec((1, RES, nb, D_MODEL), tile),
            pl.BlockSpec((None, POOL_GROUPS, POOL_GROUP_DIM, POOL_GROUP_DIM), lambda bi, i: (layer, 0, 0, 0)),
            pl.BlockSpec((1, POOL_WIDTH), fix2),
            pl.BlockSpec((None, D_MODEL, D_MODEL), lambda bi, i: (layer, 0, 0)),
            pl.BlockSpec((1, D_MODEL), fix2),
            pl.BlockSpec((D_MODEL, 2 * LANES), fix2),
            pl.BlockSpec((1, LANES), fix2),
        ],
        out_specs=[pl.BlockSpec((1, RES, nb, D_MODEL), tile),
                   pl.BlockSpec((1, RES, nb * ROW_PITCH, LANES), tile),
                   pl.BlockSpec((1, RES, nb, LANES), tile),
                   pl.BlockSpec((1, 1, LANES), lambda bi, i: (bi * steps + i, 0, 0))],
        out_shape=[jax.ShapeDtypeStruct((b, RES, n_per, D_MODEL), F32),
                   jax.ShapeDtypeStruct((b, RES, n_per * ROW_PITCH, LANES), F32),
                   jax.ShapeDtypeStruct((b, RES, n_per, LANES), F32),
                   jax.ShapeDtypeStruct((b * steps, 1, LANES), F32)],
        scratch_shapes=[pltpu.VMEM((D_MODEL, D_MODEL), BF16)],
        compiler_params=_params("arbitrary", "arbitrary"),
        name="out_proj_router",
    )(v4(att), v4(u), v4(u), v4(x), wp, ps, wo, g2, rw, br)
    return xo.reshape(t, D_MODEL), he.reshape(t * ROW_PITCH, LANES), bk.reshape(t, LANES), cnt


def _sort_kernel(meta_ref, counts_ref, pos_ref, tinfo_ref, cnt_s, off_s):
    i = pl.program_id(0)
    rows = meta_ref.shape[0]
    lane = lax.broadcasted_iota(I32, (rows, LANES), 1)
    onehot = lane.astype(F32) == meta_ref[:, 2:3]
    oh = onehot.astype(F32)
    tile_count = jnp.sum(oh, axis=0, keepdims=True)

    @pl.when(i == 0)
    def _():
        totals = jnp.sum(counts_ref[...], axis=0)
        tiles = jnp.floor((totals + (MOE_TILE - 1.0)) * (1.0 / MOE_TILE))
        tiles8 = jnp.broadcast_to(tiles, (SUBLANES, LANES)).astype(BF16)
        sq = (LANES, LANES)
        before = lax.broadcasted_iota(I32, sq, 0) < lax.broadcasted_iota(I32, sq, 1)
        start = _dot(tiles8, before.astype(BF16))
        off_s[...] = start[0:1] * float(MOE_TILE)
        cnt_s[...] = jnp.zeros_like(cnt_s)
        end = (start + tiles8.astype(F32)).astype(BF16)
        eye = (lax.broadcasted_iota(I32, sq, 0) == lax.broadcasted_iota(I32, sq, 1)).astype(BF16)
        end_col = _dot_nt(eye, end)[:, 0:1]
        tile_id = lax.broadcasted_iota(I32, sq, 1).astype(F32)
        tile_bucket = jnp.sum((end_col <= tile_id).astype(F32), axis=0, keepdims=True)
        total = jnp.max(end.astype(F32)[0:1], axis=1, keepdims=True)
        row_lane = lax.broadcasted_iota(I32, (1, LANES), 1)
        ends = pltpu.roll(end.astype(F32)[0:1], END_LANE, 1)
        is_end = (row_lane >= END_LANE) & (row_lane < END_LANE + N_BUCKETS)
        tinfo_ref[...] = jnp.where(row_lane == LANES - 1, total, jnp.where(is_end, ends, tile_bucket)).astype(I32)

    sq = (rows, rows)
    upto = lax.broadcasted_iota(I32, sq, 1) <= lax.broadcasted_iota(I32, sq, 0)
    prefix = _dot(upto.astype(BF16), oh.astype(BF16))
    posv = jnp.where(onehot, prefix - 1.0 + cnt_s[...] + off_s[...], 0.0)
    hi = jnp.floor(posv * (1.0 / 256.0))
    lo = posv - hi * 256.0
    ones = jnp.ones((SUBLANES, LANES), BF16)
    pos = _dot_nt(ones, hi.astype(BF16)) * 256.0 + _dot_nt(ones, lo.astype(BF16))
    pos_ref[0] = pos[0:1].astype(I32)
    cnt_s[...] += tile_count


def _sort_call(meta, counts):
    t = meta.shape[0]
    n_tiles = t // ROW_TILE
    return pl.pallas_call(
        _sort_kernel,
        grid=(n_tiles,),
        in_specs=[pl.BlockSpec((ROW_TILE, LANES), lambda i: (i, 0)),
                  pl.BlockSpec(counts.shape, lambda i: (0, 0, 0))],
        out_specs=[pl.BlockSpec((1, 1, ROW_TILE), lambda i: (i, 0, 0)),
                   pl.BlockSpec((1, LANES), lambda i: (0, 0))],
        out_shape=[jax.ShapeDtypeStruct((n_tiles, 1, ROW_TILE), I32),
                   jax.ShapeDtypeStruct((1, LANES), I32)],
        scratch_shapes=[pltpu.VMEM((1, LANES), F32), pltpu.VMEM((1, LANES), F32)],
        compiler_params=_params("arbitrary"),
        name="bucket_sort",
    )(meta, counts)


def _scatter_kernel(tinfo_ref, pos_ref, he_ref, hs_ref, zeros_ref, sem, zsem):
    rows = he_ref.shape[0] // ROW_PITCH

    @pl.when(pl.program_id(0) == 0)
    def _():
        zeros_ref[...] = jnp.zeros_like(zeros_ref)

        def tile_fill(j):
            return pltpu.make_async_copy(zeros_ref, _row_tile(hs_ref, j * MOE_TILE, MOE_TILE), zsem)

        def fill(b):
            end = tinfo_ref[0, END_LANE + b]
            begin = tinfo_ref[0, END_LANE + b - 1] if b else 0
            return end > begin, tile_fill(end - 1)

        unused = tile_fill
        tiles = hs_ref.shape[0] // (MOE_TILE * ROW_PITCH)
        used = tinfo_ref[0, LANES - 1]
        for b in range(N_BUCKETS):
            nonempty, copy = fill(b)
            pl.when(nonempty)(copy.start)
        lax.fori_loop(used, tiles, lambda j, c: (unused(j).start(), c)[1], 0)
        for b in range(N_BUCKETS):
            nonempty, copy = fill(b)
            pl.when(nonempty)(copy.wait)
        lax.fori_loop(used, tiles, lambda j, c: (unused(j).wait(), c)[1], 0)

    def start(g, c):
        for u in range(ISSUE_UNROLL):
            t = g * ISSUE_UNROLL + u
            pltpu.make_async_copy(_row_tile(he_ref, t), _row_tile(hs_ref, pos_ref[0, 0, t]),
                                  sem).start(priority=u % 2)
        return c

    lax.fori_loop(0, rows // ISSUE_UNROLL, start, 0)
    pltpu.make_async_copy(he_ref, _row_tile(hs_ref, 0, rows), sem).wait()


def _scatter_call(tinfo, pos, he, sorted_rows):
    t = he.shape[0] // ROW_PITCH
    return pl.pallas_call(
        _scatter_kernel,
        grid=(t // ROW_TILE,),
        in_specs=[pl.BlockSpec(memory_space=pltpu.SMEM),
                  pl.BlockSpec((1, 1, ROW_TILE), lambda i: (i, 0, 0), memory_space=pltpu.SMEM),
                  pl.BlockSpec((ROW_TILE * ROW_PITCH, LANES), lambda i: (i, 0))],
        out_specs=pl.BlockSpec(memory_space=pl.ANY),
        out_shape=jax.ShapeDtypeStruct((sorted_rows * ROW_PITCH, LANES), F32),
        scratch_shapes=[pltpu.VMEM((MOE_TILE * ROW_PITCH, LANES), F32), pltpu.SemaphoreType.DMA(()),
                        pltpu.SemaphoreType.DMA(())],
        compiler_params=_params("arbitrary"),
        name="row_scatter",
    )(tinfo, pos, he)


def _gather_kernel(pos_ref, x_ref, ys_ref, o_ref, buf, sem):
    rows = buf.shape[0] // ROW_PITCH
    _issue_row_gather(ys_ref, pos_ref, 0, buf, 0, sem)
    pltpu.make_async_copy(_row_tile(ys_ref, 0, rows), buf, sem).wait()
    for s in range(SUBLANES):
        lanes = slice(s * LANES, (s + 1) * LANES)
        o_ref[:, lanes] = x_ref[:, lanes] + buf[_row_part(rows, s), :]


def _gather_call(pos, x, ys):
    t = x.shape[0]
    return pl.pallas_call(
        _gather_kernel,
        grid=(t // ROW_TILE,),
        in_specs=[pl.BlockSpec((1, 1, ROW_TILE), lambda i: (i, 0, 0), memory_space=pltpu.SMEM),
                  pl.BlockSpec((ROW_TILE, D_MODEL), lambda i: (i, 0)),
                  pl.BlockSpec(memory_space=pl.ANY)],
        out_specs=pl.BlockSpec((ROW_TILE, D_MODEL), lambda i: (i, 0)),
        out_shape=jax.ShapeDtypeStruct((t, D_MODEL), F32),
        scratch_shapes=[pltpu.VMEM((ROW_TILE * ROW_PITCH, LANES), F32), pltpu.SemaphoreType.DMA(())],
        compiler_params=_params("arbitrary"),
        name="row_gather_residual",
    )(pos, x, ys)


def _tile_group(j, tinfo):
    used = tinfo[LANES - 1]
    return tinfo[jnp.minimum(j, used - 1)] // PAIRS_PER_GROUP


def _moe_kernel(tinfo, hs_ref, wg32_ref, wu32_ref, wd32_ref, ys_ref, wg_ref, wu_ref, wd_ref):
    j = pl.program_id(0)
    used = tinfo[LANES - 1]

    @pl.when((j == 0) | ((j < used) & (_tile_group(j, tinfo) != _tile_group(jnp.maximum(j, 1) - 1, tinfo))))
    def _():
        wg_ref[...] = wg32_ref[...].astype(BF16)
        wu_ref[...] = wu32_ref[...].astype(BF16)
        wd_ref[...] = wd32_ref[...].astype(BF16)

    @pl.when(j < used)
    def _():
        pair = tinfo[j] % PAIRS_PER_GROUP
        e_lo = (pair >= 3).astype(I32) + (pair >= 5).astype(I32)
        e_hi = pair - (e_lo * 3 - (e_lo * (e_lo - 1)) // 2) + e_lo + 1
        xt = jnp.concatenate([hs_ref[_row_part(MOE_TILE, s), :].astype(BF16) for s in range(SUBLANES)], axis=1)

        scalars = hs_ref[_row_part(MOE_TILE, SUBLANES), :]

        def expert(e, gate):
            hg = _dot(xt, wg_ref[e])
            hu = _dot(xt, wu_ref[e])
            act = hg * (1.0 / (1.0 + jnp.exp(-hg))) * hu * gate
            return _dot(act.astype(BF16), wd_ref[e])

        y = expert(e_lo, scalars[:, 0:1]) + expert(e_hi, scalars[:, 1:2])
        for s in range(SUBLANES):
            ys_ref[_row_part(MOE_TILE, s), :] = y[:, s * LANES:(s + 1) * LANES]
        ys_ref[_row_part(MOE_TILE, SUBLANES), :] = jnp.zeros((MOE_TILE, LANES), F32)

    @pl.when(j >= used)
    def _():
        ys_ref[...] = jnp.zeros_like(ys_ref)


def _moe_call(tinfo, hs, wg, wu, wd, layer):
    sorted_rows = hs.shape[0] // ROW_PITCH
    row = lambda j, ti: (jnp.minimum(j, ti[LANES - 1] - 1), 0)
    grp = lambda j, ti: (layer, _tile_group(j, ti), 0, 0, 0)
    return pl.pallas_call(
        _moe_kernel,
        grid_spec=pltpu.PrefetchScalarGridSpec(
            num_scalar_prefetch=1,
            grid=(sorted_rows // MOE_TILE,),
            in_specs=[pl.BlockSpec((MOE_TILE * ROW_PITCH, LANES), row),
                      pl.BlockSpec((None, None, EXPERTS_PER_GROUP, D_MODEL, D_EXPERT), grp),
                      pl.BlockSpec((None, None, EXPERTS_PER_GROUP, D_MODEL, D_EXPERT), grp),
                      pl.BlockSpec((None, None, EXPERTS_PER_GROUP, D_EXPERT, D_MODEL), grp)],
            out_specs=pl.BlockSpec((MOE_TILE * ROW_PITCH, LANES), lambda j, ti: (j, 0)),
            scratch_shapes=[pltpu.VMEM((EXPERTS_PER_GROUP, D_MODEL, D_EXPERT), BF16),
                            pltpu.VMEM((EXPERTS_PER_GROUP, D_MODEL, D_EXPERT), BF16),
                            pltpu.VMEM((EXPERTS_PER_GROUP, D_EXPERT, D_MODEL), BF16)],
        ),
        out_shape=jax.ShapeDtypeStruct((sorted_rows * ROW_PITCH, LANES), F32),
        compiler_params=_params("arbitrary"),
        name="grouped_experts",
    )(tinfo, hs, wg, wu, wd)


def _rope_tables(seq):
    half = ROT_DIM // 2
    inv_freq = ROPE_THETA ** (-jnp.arange(0, ROT_DIM, 2, dtype=F32) / ROT_DIM)
    row = jnp.arange(seq)
    pos = (RES * (row % (seq // RES)) + row // (seq // RES)).astype(F32)
    ang = pos[:, None] * inv_freq[None, :]
    cos, sin = jnp.cos(ang), jnp.sin(ang)
    d = jnp.arange(LANES) % HEAD_DIM
    cos_l = jnp.where(d[None, :] < ROT_DIM, cos[:, d % half], 1.0)
    sin_l = sin[:, d % half]
    sa = jnp.where(d[None, :] < half, -sin_l, 0.0)
    sb = jnp.where((d[None, :] >= half) & (d[None, :] < ROT_DIM), sin_l, 0.0)
    return cos_l, sa, sb


def kernel(x, norm1_gain, w_in, q_norm_gain, k_norm_gain, w_pool, pool_scale, w_out, norm2_gain, w_group, b_group, w_router, b_router, w_gate, w_up, w_down):
    b, seq, d = x.shape
    depth = w_in.shape[0]
    t = b * seq
    n_per = seq // RES
    assert d == D_MODEL and seq % (RES * ATT_BLK) == 0 and t % ROW_TILE == 0
    sorted_rows = t + N_BUCKETS * MOE_TILE

    cos, sa, sb = _rope_tables(seq)
    lane_head = jnp.arange(MXU_WIDTH) // HEAD_DIM
    block_diag = (lane_head[:, None] == lane_head[None, :]).astype(BF16)

    w_pool_b = w_pool.astype(BF16)
    grouped = lambda w: w.reshape(depth, N_EXPERT_GROUPS, EXPERTS_PER_GROUP, *w.shape[2:])
    w_gate_g, w_up_g, w_down_g = grouped(w_gate), grouped(w_up), grouped(w_down)
    n_logits = N_EXPERT_GROUPS * (1 + EXPERTS_PER_GROUP)
    w_r = jnp.pad(jnp.concatenate([w_group, w_router], axis=-1), ((0, 0), (0, 0), (0, LANES - n_logits)))
    w_r_hi = w_r.astype(BF16)
    w_r_lo = (w_r - w_r_hi.astype(F32)).astype(BF16)
    w_r_split = jnp.concatenate([w_r_hi, w_r_lo], axis=-1)
    b_r = jnp.pad(jnp.concatenate([b_group, b_router], axis=-1), ((0, 0), (0, LANES - n_logits)))
    two_heads = lambda g: jnp.tile(g, (1, LANES // HEAD_DIM))

    xr = x.reshape(b, n_per, RES, d).transpose(0, 2, 1, 3).reshape(t, d)
    moe = None
    for l in range(depth):
        q, k, v, u, *x_new = _in_call(xr, norm1_gain[l:l + 1], w_in, l, two_heads(q_norm_gain[l:l + 1]),
                                      two_heads(k_norm_gain[l:l + 1]), cos, sa, sb, block_diag, moe)
        xr = x_new[0] if x_new else xr
        att = _attn_call(q, k, v, seq)
        xr, he, meta, counts = _out_call(att, u, xr, w_pool_b, pool_scale[l:l + 1], w_out, l, norm2_gain[l:l + 1],
                                         w_r_split[l], b_r[l:l + 1], seq)
        pos, tinfo = _sort_call(meta, counts)
        hs = _scatter_call(tinfo, pos, he, sorted_rows)
        ys = _moe_call(tinfo.reshape(LANES), hs, w_gate_g, w_up_g, w_down_g, l)
        moe = (pos, ys)
    natural = lambda a: a.reshape(b, RES, n_per, -1).transpose(0, 2, 1, 3)
    out = _gather_call(natural(pos).reshape(pos.shape), natural(xr).reshape(t, d), ys)
    return out.reshape(b, seq, d)
```

```python
import functools

import jax
import jax.numpy as jnp
from jax import lax
from jax.experimental import pallas as pl
from jax.experimental.pallas import tpu as pltpu

D_MODEL = 1024
N_HEADS = 8
HEAD_DIM = 64
ATT_WIDTH = N_HEADS * HEAD_DIM
POOL_GROUPS = 4
POOL_GROUP_DIM = 128
POOL_WIDTH = POOL_GROUPS * POOL_GROUP_DIM
POOL_WINDOWS = (2, 4, 8, 16)
IN_WIDTH = 3 * ATT_WIDTH + POOL_WIDTH
ROT_DIM = 16
ROPE_THETA = 500000.0
N_EXPERT_GROUPS = 4
EXPERTS_PER_GROUP = 4
D_EXPERT = 256
RMS_EPS = 1e-6
NEG_INF = -1e30
LOG2_E = 1.4426950408889634

LANES = 128
SUBLANES = 8
MXU_WIDTH = 256
RES = 16
ATT_BLK = 128
ATT_STEP_BLOCKS = 32
PAIRS_PER_GROUP = 6
N_BUCKETS = N_EXPERT_GROUPS * PAIRS_PER_GROUP
ROW_TILE = 512
MOE_TILE = 256
ISSUE_UNROLL = 8
ROW_PITCH = SUBLANES + 1
END_LANE = 96
VMEM_LIMIT = 56 * 1024 * 1024

F32 = jnp.float32
BF16 = jnp.bfloat16
I32 = jnp.int32


def _dot(a, b):
    return jnp.dot(a, b, preferred_element_type=F32)


def _dot_nt(a, b):
    return lax.dot_general(a, b, (((1,), (1,)), ((), ())), preferred_element_type=F32)


def _row_tile(ref, row, n=1):
    return ref.at[pl.ds(row * ROW_PITCH, n * ROW_PITCH), :]


def _row_part(n, s):
    return pl.ds(s, n, stride=ROW_PITCH)


def _params(*sem):
    return pltpu.CompilerParams(dimension_semantics=sem, vmem_limit_bytes=VMEM_LIMIT)


def _issue_row_gather(ys_ref, pos_ref, tile, buf, first, sem):
    def start(g, c):
        for u in range(ISSUE_UNROLL):
            t = g * ISSUE_UNROLL + u
            pltpu.make_async_copy(_row_tile(ys_ref, pos_ref[tile, 0, t]), _row_tile(buf, first + t),
                                  sem).start(priority=1)
        return c

    lax.fori_loop(0, ROW_TILE // ISSUE_UNROLL, start, 0)


def _in_kernel(*refs, with_moe):
    if with_moe:
        pos_ref, next_pos_ref, ys_ref, refs = refs[0], refs[1], refs[2], refs[3:]
        xo_ref, buf, sem = refs[-4], refs[-2], refs[-1]
        refs = refs[:-4] + (refs[-3],)
    x_ref, g1_ref, w_ref, qg_ref, kg_ref, cos_ref, sa_ref, sb_ref, bd_ref, q_ref, k_ref, v_ref, u_ref, wb_ref = refs
    i = pl.program_id(0)

    if with_moe:
        slot_rows = ROW_TILE

        def issue(tile_pos_ref, slot):
            _issue_row_gather(ys_ref, tile_pos_ref, 0, buf, slot * slot_rows, sem.at[slot])

        pl.when(i == 0)(lambda: issue(pos_ref, 0))
        has_next = i + 1 < pl.num_programs(0)
        pl.when(has_next & (i % 2 == 1))(lambda: issue(next_pos_ref, 0))
        pl.when(has_next & (i % 2 == 0))(lambda: issue(next_pos_ref, 1))

    @pl.when(i == 0)
    def _():
        wb_ref[...] = w_ref[...].astype(BF16)

    if with_moe:
        first = (i % 2) * slot_rows
        pltpu.make_async_copy(_row_tile(ys_ref, 0, slot_rows), _row_tile(buf, first, slot_rows), sem.at[i % 2]).wait()
        x = jnp.concatenate([x_ref[:, s * LANES:(s + 1) * LANES]
                             + buf[pl.ds(first * ROW_PITCH + s, slot_rows, stride=ROW_PITCH), :]
                             for s in range(SUBLANES)], axis=1)
        xo_ref[...] = x
    else:
        x = x_ref[...]
    ms = jnp.mean(x * x, axis=-1, keepdims=True)
    h = (x * lax.rsqrt(ms + RMS_EPS) * g1_ref[...]).astype(BF16)
    cos = cos_ref[...]
    sa = sa_ref[...]
    sb = sb_ref[...]
    bd = bd_ref[...]
    wide = bd.shape[0]

    def qk(col0, gain, out_ref, scale):
        z = _dot(h, wb_ref[:, col0:col0 + ATT_WIDTH])
        for w0 in range(0, ATT_WIDTH, wide):
            zw = z[:, w0:w0 + wide]
            ssq = _dot((zw * zw).astype(BF16), bd)
            yw = zw * lax.rsqrt(ssq * (1.0 / HEAD_DIM) + RMS_EPS)
            for c in range(wide // LANES):
                y = yw[:, c * LANES:(c + 1) * LANES] * gain
                rot = y * cos + pltpu.roll(y, LANES - ROT_DIM // 2, 1) * sa + pltpu.roll(y, ROT_DIM // 2, 1) * sb
                out_ref[:, w0 + c * LANES:w0 + (c + 1) * LANES] = (rot * scale).astype(BF16)

    qk(0, qg_ref[...], q_ref, HEAD_DIM ** -0.5 * LOG2_E)
    qk(ATT_WIDTH, kg_ref[...], k_ref, 1.0)
    v_ref[...] = _dot(h, wb_ref[:, 2 * ATT_WIDTH:3 * ATT_WIDTH]).astype(BF16)
    u_ref[...] = _dot(h, wb_ref[:, 3 * ATT_WIDTH:]).astype(BF16)


def _in_call(x, g1, w, layer, qg, kg, cos, sa, sb, bd, moe=None):
    t = x.shape[0]
    seq_tiles = cos.shape[0] // ROW_TILE
    row = lambda i: (i, 0)
    fix = lambda i: (0, 0)
    tab = lambda i: (i % seq_tiles, 0)
    out = jax.ShapeDtypeStruct((t, ATT_WIDTH), BF16)
    with_moe = moe is not None
    tiles = t // ROW_TILE
    pos_block = lambda index: pl.BlockSpec((1, 1, ROW_TILE), index, memory_space=pltpu.SMEM)
    extra_in = [pos_block(lambda i: (i, 0, 0)), pos_block(lambda i: (jnp.minimum(i + 1, tiles - 1), 0, 0)),
                pl.BlockSpec(memory_space=pl.ANY)] if with_moe else []
    extra_out = [pl.BlockSpec((ROW_TILE, D_MODEL), row)] if with_moe else []
    extra_shape = [jax.ShapeDtypeStruct((t, D_MODEL), F32)] if with_moe else []
    extra_scratch = [pltpu.VMEM((2 * ROW_TILE * ROW_PITCH, LANES), F32), pltpu.SemaphoreType.DMA((2,))] if with_moe else []
    return pl.pallas_call(
        functools.partial(_in_kernel, with_moe=with_moe),
        grid=(t // ROW_TILE,),
        in_specs=extra_in + [
            pl.BlockSpec((ROW_TILE, D_MODEL), row),
            pl.BlockSpec((1, D_MODEL), fix),
            pl.BlockSpec((None, D_MODEL, IN_WIDTH), lambda i: (layer, 0, 0)),
            pl.BlockSpec((1, LANES), fix),
            pl.BlockSpec((1, LANES), fix),
            pl.BlockSpec((ROW_TILE, LANES), tab),
            pl.BlockSpec((ROW_TILE, LANES), tab),
            pl.BlockSpec((ROW_TILE, LANES), tab),
            pl.BlockSpec(bd.shape, fix),
        ],
        out_specs=[pl.BlockSpec((ROW_TILE, ATT_WIDTH), row)] * 4 + extra_out,
        out_shape=[out] * 4 + extra_shape,
        scratch_shapes=[pltpu.VMEM((D_MODEL, IN_WIDTH), BF16)] + extra_scratch,
        compiler_params=_params("arbitrary"),
        name="in_proj",
    )(*((moe[0], moe[0], moe[1]) if with_moe else ()), x, g1, w, qg, kg, cos, sa, sb, bd)


def _attn_bias(q_off, k_idx, with_prev):
    ok = (k_idx >= q_off) & (k_idx <= q_off + ATT_BLK)
    if not with_prev:
        ok = ok & (k_idx >= ATT_BLK)
    return jnp.where(ok, 0.0, NEG_INF).astype(F32)


def _attn_kernel(q_ref, k_ref, v_ref, o_ref, q32, k32, v32, m_s, l_s, acc_s, bias_s):
    n_per = q32.shape[1]
    pad = ATT_BLK
    zeros = jnp.zeros((pad, LANES), F32)
    for r in range(RES):
        rows = pl.ds(r * n_per, n_per)
        q32[r] = q_ref[rows, :].astype(F32)
        k32[r, pl.ds(0, pad), :] = zeros
        v32[r, pl.ds(0, pad), :] = zeros
        k32[r, pl.ds(pad, n_per), :] = k_ref[rows, :].astype(F32)
        v32[r, pl.ds(pad, n_per), :] = v_ref[rows, :].astype(F32)

    qi = lax.broadcasted_iota(I32, (2 * ATT_BLK, 2 * ATT_BLK), 0) & (ATT_BLK - 1)
    kc = lax.broadcasted_iota(I32, (2 * ATT_BLK, 2 * ATT_BLK), 1)
    offs = (
        (16 * (qi & 7) + (qi >> 3), 16 * (kc & 15) + (kc >> 4)),
        (4 * (qi & 31) + (qi >> 5), 4 * (kc & 63) + (kc >> 6)),
        (qi, kc),
    )
    for br, (qo, ko) in enumerate(offs):
        bias_s[br, 0] = _attn_bias(qo, ko, False)
        bias_s[br, 1] = _attn_bias(qo, ko, True)

    head_a = lax.broadcasted_iota(I32, (ATT_BLK, LANES), 1) < HEAD_DIM

    def block(qb, ks, vs, bias):
        qa = jnp.where(head_a, qb, 0.0)
        qq = jnp.concatenate([qa, qb - qa], axis=0).astype(BF16)
        s = _dot_nt(qq, ks.astype(BF16)) + bias
        m = jnp.max(s, axis=1, keepdims=True)
        p = jnp.exp2(s - m)
        l = jnp.sum(p, axis=1, keepdims=True)
        pv = _dot(p.astype(BF16), vs.astype(BF16))
        m2 = jnp.where(head_a, m[:ATT_BLK], m[ATT_BLK:])
        l2 = jnp.where(head_a, l[:ATT_BLK], l[ATT_BLK:])
        pv2 = jnp.where(head_a, pv[:ATT_BLK], pv[ATT_BLK:])
        return m2, l2, pv2

    def store(br, slab, rows, triple, shape=None):
        for ref, val in zip((m_s, l_s, acc_s), triple):
            ref[br, slab, rows, :] = val if shape is None else val.reshape(shape)

    slabs16 = ATT_STEP_BLOCKS // (n_per // ATT_BLK)

    def body16(rr, carry):
        for i in range(slabs16):
            r = slabs16 * rr + i
            for c in range(n_per // ATT_BLK):
                rows = pl.ds(c * ATT_BLK, ATT_BLK)
                keys = pl.ds(c * ATT_BLK, 2 * ATT_BLK)
                store(2, r, rows, block(q32[r, rows, :], k32[r, keys, :], v32[r, keys, :], bias_s[2, min(c, 1)]))
        return carry

    lax.fori_loop(0, RES // slabs16, body16, 0)

    sub = ATT_BLK // 4
    per4 = ATT_STEP_BLOCKS // 4

    def body4(cc, carry):
        for r4 in range(4):
            slabs = [r4 + 4 * m for m in range(4)]
            for c in range(per4):
                first = pl.multiple_of((cc * per4 + c) * sub, sub)
                rows = pl.ds(first, sub)
                keys = pl.ds(first + pad - sub, 2 * sub)
                cat = lambda ref, idx: jnp.concatenate([ref[s, idx, :] for s in slabs], axis=0)
                bias = bias_s[1, jnp.minimum(cc, 1)] if c == 0 else bias_s[1, 1]
                triple = block(cat(q32, rows), cat(k32, keys), cat(v32, keys), bias)
                for j, s in enumerate(slabs):
                    store(1, s, rows, [x[j * sub:(j + 1) * sub] for x in triple])
        return carry

    lax.fori_loop(0, n_per // (per4 * sub), body4, 0)

    sub1 = ATT_BLK // RES

    def body1(jj, carry):
        for g in range(ATT_STEP_BLOCKS):
            first = pl.multiple_of((jj * ATT_STEP_BLOCKS + g) * sub1, sub1)
            rows = pl.ds(first, sub1)
            keys = pl.ds(first + pad - sub1, 2 * sub1)
            bias = bias_s[0, jnp.minimum(jj, 1)] if g == 0 else bias_s[0, 1]
            triple = block(q32[:, rows, :].reshape(ATT_BLK, LANES), k32[:, keys, :].reshape(2 * ATT_BLK, LANES),
                           v32[:, keys, :].reshape(2 * ATT_BLK, LANES), bias)
            store(0, slice(None), rows, triple, (RES, sub1, LANES))
        return carry

    lax.fori_loop(0, n_per // (ATT_STEP_BLOCKS * sub1), body1, 0)

    for r in range(RES):
        ms = [m_s[br, r] for br in range(3)]
        top = jnp.maximum(jnp.maximum(ms[0], ms[1]), ms[2])
        ws = [jnp.exp2(m - top) for m in ms]
        num = sum(w * acc_s[br, r] for br, w in enumerate(ws))
        den = sum(w * l_s[br, r] for br, w in enumerate(ws))
        o_ref[pl.ds(r * n_per, n_per), :] = (num / den).astype(BF16)


def _attn_call(q, k, v, seq):
    t = q.shape[0]
    n_per = seq // RES
    spec = pl.BlockSpec((seq, LANES), lambda b, h: (b, h))
    staged = pltpu.VMEM((RES, n_per, LANES), F32)
    state = pltpu.VMEM((3, RES, n_per, LANES), F32)
    padded = pltpu.VMEM((RES, n_per + ATT_BLK, LANES), F32)
    return pl.pallas_call(
        _attn_kernel,
        grid=(t // seq, ATT_WIDTH // LANES),
        in_specs=[spec, spec, spec],
        out_specs=spec,
        out_shape=jax.ShapeDtypeStruct((t, ATT_WIDTH), BF16),
        scratch_shapes=[staged, padded, padded, state, state, state,
                        pltpu.VMEM((3, 2, 2 * ATT_BLK, 2 * ATT_BLK), F32)],
        compiler_params=_params("parallel", "parallel"),
        name="dilated_attn",
    )(q, k, v)


def _row_min_index(cond, lane_f):
    return jnp.min(jnp.where(cond, lane_f, float(LANES)), axis=1, keepdims=True)


def _out_kernel(att_ref, u_ref, uh_ref, x_ref, wp_ref, ps_ref, wo_ref, g2_ref, rw_ref, br_ref,
                xo_ref, he_ref, bk_ref, cnt_ref, wob_ref):
    i = pl.program_id(1)

    @pl.when((pl.program_id(0) == 0) & (i == 0))
    def _():
        wob_ref[...] = wo_ref[...].astype(BF16)

    nb = u_ref.shape[2]
    rows = RES * nb
    u = u_ref[0].astype(F32)
    halo = jnp.where(i > 0, uh_ref[0][:, -1:, :].astype(F32), 0.0)
    u_prev = jnp.concatenate([halo, u[:, :nb - 1, :]], axis=1)

    n_idx = lax.broadcasted_iota(I32, (RES, nb, POOL_GROUP_DIM), 1) + i * nb
    r_idx = lax.broadcasted_iota(I32, (RES, nb, POOL_GROUP_DIM), 0)
    p1 = (RES * n_idx + r_idx + 1).astype(F32)

    pools = []
    for g, w in enumerate(POOL_WINDOWS):
        lanes = slice(g * POOL_GROUP_DIM, (g + 1) * POOL_GROUP_DIM)
        ug = u[:, :, lanes]
        upg = u_prev[:, :, lanes]
        tot = ug
        for j in range(1, w):
            tot = tot + jnp.concatenate([upg[RES - j:], ug[:RES - j]], axis=0)
        rg = tot / jnp.minimum(p1, float(w)) - ug
        y = _dot(rg.reshape(rows, POOL_GROUP_DIM).astype(BF16), wp_ref[g])
        pools.append((y * ps_ref[:, lanes]).astype(BF16))
    mix = jnp.concatenate([att_ref[0].reshape(rows, ATT_WIDTH)] + pools, axis=1)
    x = x_ref[0].reshape(rows, D_MODEL) + _dot(mix, wob_ref[...])
    xo_ref[0] = x.reshape(RES, nb, D_MODEL)

    ms = jnp.mean(x * x, axis=-1, keepdims=True)
    h = x * lax.rsqrt(ms + RMS_EPS) * g2_ref[...]

    hh = h.astype(BF16)
    hl = (h - hh.astype(F32)).astype(BF16)
    by_hh = _dot(hh, rw_ref[...])
    logits = by_hh[:, :LANES] + (_dot(hl, rw_ref[:, :LANES]) + by_hh[:, LANES:]) + br_ref[...]
    lane = lax.broadcasted_iota(I32, (rows, LANES), 1)
    lane_f = lane.astype(F32)
    is_g = lane < N_EXPERT_GROUPS
    gl = jnp.where(is_g, logits, -jnp.inf)
    gm = jnp.max(gl, axis=1, keepdims=True)
    g_idx = _row_min_index(is_g & (gl == gm), lane_f)
    p_top = 1.0 / jnp.sum(jnp.where(is_g, jnp.exp(logits - gm), 0.0), axis=1, keepdims=True)
    e_lane = lane - N_EXPERT_GROUPS
    in_grp = (e_lane >= 0) & (e_lane < N_EXPERT_GROUPS * EXPERTS_PER_GROUP) & \
             ((e_lane >> 2).astype(F32) == g_idx)
    el = jnp.where(in_grp, logits, -jnp.inf)
    v1 = jnp.max(el, axis=1, keepdims=True)
    i1 = _row_min_index(in_grp & (el == v1), lane_f)
    rest = in_grp & (lane_f != i1)
    el2 = jnp.where(rest, logits, -jnp.inf)
    v2 = jnp.max(el2, axis=1, keepdims=True)
    i2 = _row_min_index(rest & (el2 == v2), lane_f)
    e21 = jnp.exp(v2 - v1)
    w1 = p_top / (1.0 + e21)
    w2 = p_top * e21 / (1.0 + e21)
    a1 = i1 - N_EXPERT_GROUPS - EXPERTS_PER_GROUP * g_idx
    a2 = i2 - N_EXPERT_GROUPS - EXPERTS_PER_GROUP * g_idx
    first_low = a1 < a2
    lo = jnp.where(first_low, a1, a2)
    hi = jnp.where(first_low, a2, a1)
    w_lo = jnp.where(first_low, w1, w2)
    w_hi = jnp.where(first_low, w2, w1)
    bucket = g_idx * PAIRS_PER_GROUP + lo * 3.0 - lo * (lo - 1.0) * 0.5 + hi - lo - 1.0
    meta = jnp.where(lane == 0, w_lo, jnp.where(lane == 1, w_hi, bucket))
    bk_ref[0] = meta.reshape(RES, nb, LANES)
    cnt_ref[0] = jnp.sum((lane_f == bucket).astype(F32), axis=0, keepdims=True)

    for s in range(ROW_PITCH):
        part = h[:, s * LANES:(s + 1) * LANES] if s < SUBLANES else meta
        for r in range(RES):
            he_ref[0, r, _row_part(nb, s), :] = part[r * nb:(r + 1) * nb]


def _out_call(att, u, x, wp, ps, wo, layer, g2, rw, br, seq):
    t = x.shape[0]
    b = t // seq
    n_per = seq // RES
    nb = ROW_TILE // RES
    halo_rows = 16
    v4 = lambda a: a.reshape(b, RES, n_per, a.shape[-1])
    tile = lambda bi, i: (bi, 0, i, 0)
    halo = lambda bi, i: (bi, 0, jnp.maximum(i * (nb // halo_rows) - 1, 0), 0)
    fix2 = lambda bi, i: (0, 0)
    steps = n_per // nb
    xo, he, bk, cnt = pl.pallas_call(
        _out_kernel,
        grid=(b, steps),
        in_specs=[
            pl.BlockSpec((1, RES, nb, ATT_WIDTH), tile),
            pl.BlockSpec((1, RES, nb, POOL_WIDTH), tile),
            pl.BlockSpec((1, RES, halo_rows, POOL_WIDTH), halo),
            pl.BlockSpec((1, RES, nb, D_MODEL), tile),
            pl.BlockSpec((None, POOL_GROUPS, POOL_GROUP_DIM, POOL_GROUP_DIM), lambda bi, i: (layer, 0, 0, 0)),
            pl.BlockSpec((1, POOL_WIDTH), fix2),
            pl.BlockSpec((None, D_MODEL, D_MODEL), lambda bi, i: (layer, 0, 0)),
            pl.BlockSpec((1, D_MODEL), fix2),
            pl.BlockSpec((D_MODEL, 2 * LANES), fix2),
            pl.BlockSpec((1, LANES), fix2),
        ],
        out_specs=[pl.BlockSpec((1, RES, nb, D_MODEL), tile),
                   pl.BlockSpec((1, RES, nb * ROW_PITCH, LANES), tile),
                   pl.BlockSpec((1, RES, nb, LANES), tile),
                   pl.BlockSpec((1, 1, LANES), lambda bi, i: (bi * steps + i, 0, 0))],
        out_shape=[jax.ShapeDtypeStruct((b, RES, n_per, D_MODEL), F32),
                   jax.ShapeDtypeStruct((b, RES, n_per * ROW_PITCH, LANES), F32),
                   jax.ShapeDtypeStruct((b, RES, n_per, LANES), F32),
                   jax.ShapeDtypeStruct((b * steps, 1, LANES), F32)],
        scratch_shapes=[pltpu.VMEM((D_MODEL, D_MODEL), BF16)],
        compiler_params=_params("arbitrary", "arbitrary"),
        name="out_proj_router",
    )(v4(att), v4(u), v4(u), v4(x), wp, ps, wo, g2, rw, br)
    return xo.reshape(t, D_MODEL), he.reshape(t * ROW_PITCH, LANES), bk.reshape(t, LANES), cnt


def _sort_kernel(meta_ref, counts_ref, pos_ref, tinfo_ref, cnt_s, off_s):
    i = pl.program_id(0)
    rows = meta_ref.shape[0]
    lane = lax.broadcasted_iota(I32, (rows, LANES), 1)
    onehot = lane.astype(F32) == meta_ref[:, 2:3]
    oh = onehot.astype(F32)
    tile_count = jnp.sum(oh, axis=0, keepdims=True)

    @pl.when(i == 0)
    def _():
        totals = jnp.sum(counts_ref[...], axis=0)
        tiles = jnp.floor((totals + (MOE_TILE - 1.0)) * (1.0 / MOE_TILE))
        tiles8 = jnp.broadcast_to(tiles, (SUBLANES, LANES)).astype(BF16)
        sq = (LANES, LANES)
        before = lax.broadcasted_iota(I32, sq, 0) < lax.broadcasted_iota(I32, sq, 1)
        start = _dot(tiles8, before.astype(BF16))
        off_s[...] = start[0:1] * float(MOE_TILE)
        cnt_s[...] = jnp.zeros_like(cnt_s)
        end = (start + tiles8.astype(F32)).astype(BF16)
        eye = (lax.broadcasted_iota(I32, sq, 0) == lax.broadcasted_iota(I32, sq, 1)).astype(BF16)
        end_col = _dot_nt(eye, end)[:, 0:1]
        tile_id = lax.broadcasted_iota(I32, sq, 1).astype(F32)
        tile_bucket = jnp.sum((end_col <= tile_id).astype(F32), axis=0, keepdims=True)
        total = jnp.max(end.astype(F32)[0:1], axis=1, keepdims=True)
        row_lane = lax.broadcasted_iota(I32, (1, LANES), 1)
        ends = pltpu.roll(end.astype(F32)[0:1], END_LANE, 1)
        is_end = (row_lane >= END_LANE) & (row_lane < END_LANE + N_BUCKETS)
        tinfo_ref[...] = jnp.where(row_lane == LANES - 1, total, jnp.where(is_end, ends, tile_bucket)).astype(I32)

    sq = (rows, rows)
    upto = lax.broadcasted_iota(I32, sq, 1) <= lax.broadcasted_iota(I32, sq, 0)
    prefix = _dot(upto.astype(BF16), oh.astype(BF16))
    posv = jnp.where(onehot, prefix - 1.0 + cnt_s[...] + off_s[...], 0.0)
    hi = jnp.floor(posv * (1.0 / 256.0))
    lo = posv - hi * 256.0
    ones = jnp.ones((SUBLANES, LANES), BF16)
    pos = _dot_nt(ones, hi.astype(BF16)) * 256.0 + _dot_nt(ones, lo.astype(BF16))
    pos_ref[0] = pos[0:1].astype(I32)
    cnt_s[...] += tile_count


def _sort_call(meta, counts):
    t = meta.shape[0]
    n_tiles = t // ROW_TILE
    return pl.pallas_call(
        _sort_kernel,
        grid=(n_tiles,),
        in_specs=[pl.BlockSpec((ROW_TILE, LANES), lambda i: (i, 0)),
                  pl.BlockSpec(counts.shape, lambda i: (0, 0, 0))],
        out_specs=[pl.BlockSpec((1, 1, ROW_TILE), lambda i: (i, 0, 0)),
                   pl.BlockSpec((1, LANES), lambda i: (0, 0))],
        out_shape=[jax.ShapeDtypeStruct((n_tiles, 1, ROW_TILE), I32),
                   jax.ShapeDtypeStruct((1, LANES), I32)],
        scratch_shapes=[pltpu.VMEM((1, LANES), F32), pltpu.VMEM((1, LANES), F32)],
        compiler_params=_params("arbitrary"),
        name="bucket_sort",
    )(meta, counts)


def _scatter_kernel(tinfo_ref, pos_ref, he_ref, hs_ref, zeros_ref, sem, zsem):
    rows = he_ref.shape[0] // ROW_PITCH

    @pl.when(pl.program_id(0) == 0)
    def _():
        zeros_ref[...] = jnp.zeros_like(zeros_ref)

        def tile_fill(j):
            return pltpu.make_async_copy(zeros_ref, _row_tile(hs_ref, j * MOE_TILE, MOE_TILE), zsem)

        def fill(b):
            end = tinfo_ref[0, END_LANE + b]
            begin = tinfo_ref[0, END_LANE + b - 1] if b else 0
            return end > begin, tile_fill(end - 1)

        unused = tile_fill
        tiles = hs_ref.shape[0] // (MOE_TILE * ROW_PITCH)
        used = tinfo_ref[0, LANES - 1]
        for b in range(N_BUCKETS):
            nonempty, copy = fill(b)
            pl.when(nonempty)(copy.start)
        lax.fori_loop(used, tiles, lambda j, c: (unused(j).start(), c)[1], 0)
        for b in range(N_BUCKETS):
            nonempty, copy = fill(b)
            pl.when(nonempty)(copy.wait)
        lax.fori_loop(used, tiles, lambda j, c: (unused(j).wait(), c)[1], 0)

    def start(g, c):
        for u in range(ISSUE_UNROLL):
            t = g * ISSUE_UNROLL + u
            pltpu.make_async_copy(_row_tile(he_ref, t), _row_tile(hs_ref, pos_ref[0, 0, t]),
                                  sem).start(priority=u % 2)
        return c

    lax.fori_loop(0, rows // ISSUE_UNROLL, start, 0)
    pltpu.make_async_copy(he_ref, _row_tile(hs_ref, 0, rows), sem).wait()


def _scatter_call(tinfo, pos, he, sorted_rows):
    t = he.shape[0] // ROW_PITCH
    return pl.pallas_call(
        _scatter_kernel,
        grid=(t // ROW_TILE,),
        in_specs=[pl.BlockSpec(memory_space=pltpu.SMEM),
                  pl.BlockSpec((1, 1, ROW_TILE), lambda i: (i, 0, 0), memory_space=pltpu.SMEM),
                  pl.BlockSpec((ROW_TILE * ROW_PITCH, LANES), lambda i: (i, 0))],
        out_specs=pl.BlockSpec(memory_space=pl.ANY),
        out_shape=jax.ShapeDtypeStruct((sorted_rows * ROW_PITCH, LANES), F32),
        scratch_shapes=[pltpu.VMEM((MOE_TILE * ROW_PITCH, LANES), F32), pltpu.SemaphoreType.DMA(()),
                        pltpu.SemaphoreType.DMA(())],
        compiler_params=_params("arbitrary"),
        name="row_scatter",
    )(tinfo, pos, he)


def _gather_kernel(pos_ref, x_ref, ys_ref, o_ref, buf, sem):
    rows = buf.shape[0] // ROW_PITCH
    _issue_row_gather(ys_ref, pos_ref, 0, buf, 0, sem)
    pltpu.make_async_copy(_row_tile(ys_ref, 0, rows), buf, sem).wait()
    for s in range(SUBLANES):
        lanes = slice(s * LANES, (s + 1) * LANES)
        o_ref[:, lanes] = x_ref[:, lanes] + buf[_row_part(rows, s), :]


def _gather_call(pos, x, ys):
    t = x.shape[0]
    return pl.pallas_call(
        _gather_kernel,
        grid=(t // ROW_TILE,),
        in_specs=[pl.BlockSpec((1, 1, ROW_TILE), lambda i: (i, 0, 0), memory_space=pltpu.SMEM),
                  pl.BlockSpec((ROW_TILE, D_MODEL), lambda i: (i, 0)),
                  pl.BlockSpec(memory_space=pl.ANY)],
        out_specs=pl.BlockSpec((ROW_TILE, D_MODEL), lambda i: (i, 0)),
        out_shape=jax.ShapeDtypeStruct((t, D_MODEL), F32),
        scratch_shapes=[pltpu.VMEM((ROW_TILE * ROW_PITCH, LANES), F32), pltpu.SemaphoreType.DMA(())],
        compiler_params=_params("arbitrary"),
        name="row_gather_residual",
    )(pos, x, ys)


def _tile_group(j, tinfo):
    used = tinfo[LANES - 1]
    return tinfo[jnp.minimum(j, used - 1)] // PAIRS_PER_GROUP


def _moe_kernel(tinfo, hs_ref, wg32_ref, wu32_ref, wd32_ref, ys_ref, wg_ref, wu_ref, wd_ref):
    j = pl.program_id(0)
    used = tinfo[LANES - 1]

    @pl.when((j == 0) | ((j < used) & (_tile_group(j, tinfo) != _tile_group(jnp.maximum(j, 1) - 1, tinfo))))
    def _():
        wg_ref[...] = wg32_ref[...].astype(BF16)
        wu_ref[...] = wu32_ref[...].astype(BF16)
        wd_ref[...] = wd32_ref[...].astype(BF16)

    @pl.when(j < used)
    def _():
        pair = tinfo[j] % PAIRS_PER_GROUP
        e_lo = (pair >= 3).astype(I32) + (pair >= 5).astype(I32)
        e_hi = pair - (e_lo * 3 - (e_lo * (e_lo - 1)) // 2) + e_lo + 1
        xt = jnp.concatenate([hs_ref[_row_part(MOE_TILE, s), :].astype(BF16) for s in range(SUBLANES)], axis=1)

        scalars = hs_ref[_row_part(MOE_TILE, SUBLANES), :]

        def expert(e, gate):
            hg = _dot(xt, wg_ref[e])
            hu = _dot(xt, wu_ref[e])
            act = hg * (1.0 / (1.0 + jnp.exp(-hg))) * hu * gate
            return _dot(act.astype(BF16), wd_ref[e])

        y = expert(e_lo, scalars[:, 0:1]) + expert(e_hi, scalars[:, 1:2])
        for s in range(SUBLANES):
            ys_ref[_row_part(MOE_TILE, s), :] = y[:, s * LANES:(s + 1) * LANES]
        ys_ref[_row_part(MOE_TILE, SUBLANES), :] = jnp.zeros((MOE_TILE, LANES), F32)

    @pl.when(j >= used)
    def _():
        ys_ref[...] = jnp.zeros_like(ys_ref)


def _moe_call(tinfo, hs, wg, wu, wd, layer):
    sorted_rows = hs.shape[0] // ROW_PITCH
    row = lambda j, ti: (jnp.minimum(j, ti[LANES - 1] - 1), 0)
    grp = lambda j, ti: (layer, _tile_group(j, ti), 0, 0, 0)
    return pl.pallas_call(
        _moe_kernel,
        grid_spec=pltpu.PrefetchScalarGridSpec(
            num_scalar_prefetch=1,
            grid=(sorted_rows // MOE_TILE,),
            in_specs=[pl.BlockSpec((MOE_TILE * ROW_PITCH, LANES), row),
                      pl.BlockSpec((None, None, EXPERTS_PER_GROUP, D_MODEL, D_EXPERT), grp),
                      pl.BlockSpec((None, None, EXPERTS_PER_GROUP, D_MODEL, D_EXPERT), grp),
                      pl.BlockSpec((None, None, EXPERTS_PER_GROUP, D_EXPERT, D_MODEL), grp)],
            out_specs=pl.BlockSpec((MOE_TILE * ROW_PITCH, LANES), lambda j, ti: (j, 0)),
            scratch_shapes=[pltpu.VMEM((EXPERTS_PER_GROUP, D_MODEL, D_EXPERT), BF16),
                            pltpu.VMEM((EXPERTS_PER_GROUP, D_MODEL, D_EXPERT), BF16),
                            pltpu.VMEM((EXPERTS_PER_GROUP, D_EXPERT, D_MODEL), BF16)],
        ),
        out_shape=jax.ShapeDtypeStruct((sorted_rows * ROW_PITCH, LANES), F32),
        compiler_params=_params("arbitrary"),
        name="grouped_experts",
    )(tinfo, hs, wg, wu, wd)


def _rope_tables(seq):
    half = ROT_DIM // 2
    inv_freq = ROPE_THETA ** (-jnp.arange(0, ROT_DIM, 2, dtype=F32) / ROT_DIM)
    row = jnp.arange(seq)
    pos = (RES * (row % (seq // RES)) + row // (seq // RES)).astype(F32)
    ang = pos[:, None] * inv_freq[None, :]
    cos, sin = jnp.cos(ang), jnp.sin(ang)
    d = jnp.arange(LANES) % HEAD_DIM
    cos_l = jnp.where(d[None, :] < ROT_DIM, cos[:, d % half], 1.0)
    sin_l = sin[:, d % half]
    sa = jnp.where(d[None, :] < half, -sin_l, 0.0)
    sb = jnp.where((d[None, :] >= half) & (d[None, :] < ROT_DIM), sin_l, 0.0)
    return cos_l, sa, sb


def kernel(x, norm1_gain, w_in, q_norm_gain, k_norm_gain, w_pool, pool_scale, w_out, norm2_gain, w_group, b_group, w_router, b_router, w_gate, w_up, w_down):
    b, seq, d = x.shape
    depth = w_in.shape[0]
    t = b * seq
    n_per = seq // RES
    assert d == D_MODEL and seq % (RES * ATT_BLK) == 0 and t % ROW_TILE == 0
    sorted_rows = t + N_BUCKETS * MOE_TILE

    cos, sa, sb = _rope_tables(seq)
    lane_head = jnp.arange(MXU_WIDTH) // HEAD_DIM
    block_diag = (lane_head[:, None] == lane_head[None, :]).astype(BF16)

    w_pool_b = w_pool.astype(BF16)
    grouped = lambda w: w.reshape(depth, N_EXPERT_GROUPS, EXPERTS_PER_GROUP, *w.shape[2:])
    w_gate_g, w_up_g, w_down_g = grouped(w_gate), grouped(w_up), grouped(w_down)
    n_logits = N_EXPERT_GROUPS * (1 + EXPERTS_PER_GROUP)
    w_r = jnp.pad(jnp.concatenate([w_group, w_router], axis=-1), ((0, 0), (0, 0), (0, LANES - n_logits)))
    w_r_hi = w_r.astype(BF16)
    w_r_lo = (w_r - w_r_hi.astype(F32)).astype(BF16)
    w_r_split = jnp.concatenate([w_r_hi, w_r_lo], axis=-1)
    b_r = jnp.pad(jnp.concatenate([b_group, b_router], axis=-1), ((0, 0), (0, LANES - n_logits)))
    two_heads = lambda g: jnp.tile(g, (1, LANES // HEAD_DIM))

    xr = x.reshape(b, n_per, RES, d).transpose(0, 2, 1, 3).reshape(t, d)
    moe = None
    for l in range(depth):
        q, k, v, u, *x_new = _in_call(xr, norm1_gain[l:l + 1], w_in, l, two_heads(q_norm_gain[l:l + 1]),
                                      two_heads(k_norm_gain[l:l + 1]), cos, sa, sb, block_diag, moe)
        xr = x_new[0] if x_new else xr
        att = _attn_call(q, k, v, seq)
        xr, he, meta, counts = _out_call(att, u, xr, w_pool_b, pool_scale[l:l + 1], w_out, l, norm2_gain[l:l + 1],
                                         w_r_split[l], b_r[l:l + 1], seq)
        pos, tinfo = _sort_call(meta, counts)
        hs = _scatter_call(tinfo, pos, he, sorted_rows)
        ys = _moe_call(tinfo.reshape(LANES), hs, w_gate_g, w_up_g, w_down_g, l)
        moe = (pos, ys)
    natural = lambda a: a.reshape(b, RES, n_per, -1).transpose(0, 2, 1, 3)
    out = _gather_call(natural(pos).reshape(pos.shape), natural(xr).reshape(t, d), ys)
    return out.reshape(b, seq, d)
```
